```python
import math
import jax, jax.numpy as jnp
from jax import lax
import numpy as np

D_MODEL = 1024
BATCH = 16
SEQ = 4096
DEPTH = 4

N_MIXERS = 4
N_A = (DEPTH + 3) // 4
N_B = (DEPTH + 2) // 4
N_C = (DEPTH + 1) // 4
N_D = DEPTH // 4
A_HEADS = 4
A_DV = D_MODEL // A_HEADS
A_DK = A_DV // 2
A_CHUNK = 64
A_IN = 2 * A_HEADS * A_DK + 2 * A_HEADS * A_DV + 2 * A_HEADS
B_HEAD_DIM = 64
B_Q_HEADS = D_MODEL // B_HEAD_DIM
B_KV_HEADS = B_Q_HEADS // 8
B_WINDOW = 128
B_BLOCK = 128
B_IN = (B_Q_HEADS + 2 * B_KV_HEADS) * B_HEAD_DIM
C_EXPAND = 128
C_HEADS = D_MODEL // C_EXPAND
C_DK = C_EXPAND
C_DV = D_MODEL // C_HEADS
C_CHUNK = 16
C_IN = 2 * C_HEADS * C_DK + 2 * C_HEADS * C_DV
D_HEAD_DIM = 64
D_HEADS = D_MODEL // (2 * D_HEAD_DIM)
D_QBLOCK = 128
D_IN = 3 * D_HEADS * 2 * D_HEAD_DIM
ROPE_THETA = 500000.0
ROT_FRAC = 4
FFN_HIDDEN = -(-8 * D_MODEL // (3 * 256)) * 256
DEEPNORM_ALPHA = (2 * DEPTH) ** 0.25
DEEPNORM_BETA = (8 * DEPTH) ** -0.25
F32 = jnp.float32

kernel_name = "hybrid_interleaved_mlstm_swa_hgrn2_diffattn"


def layer_norm(x, g, b, eps=1e-5):
    xf = x.astype(F32)
    mu = xf.mean(-1, keepdims=True)
    var = jnp.square(xf - mu).mean(-1, keepdims=True)
    return ((xf - mu) * lax.rsqrt(var + eps) * g.astype(F32) + b.astype(F32)).astype(x.dtype)


def head_rms_norm(h, w, eps=1e-6):
    hf = h.astype(F32)
    return hf * lax.rsqrt(jnp.mean(hf * hf, -1, keepdims=True) + eps) * w.astype(F32)


def partial_rope(x, positions):
    hd = x.shape[-1]
    rot = hd // ROT_FRAC
    half = rot // 2
    inv = jnp.power(ROPE_THETA, -jnp.arange(half, dtype=F32) * 2.0 / rot)
    ang = positions.astype(F32)[..., None] * inv
    cos = jnp.cos(ang)[:, :, None, :]
    sin = jnp.sin(ang)[:, :, None, :]
    xf = x.astype(F32)
    x1, x2 = xf[..., :half], xf[..., half:rot]
    out = jnp.concatenate([x1 * cos - x2 * sin, x2 * cos + x1 * sin, xf[..., rot:]], axis=-1)
    return out.astype(x.dtype)


def to_chunks(a, L):
    r = a.reshape(a.shape[0], a.shape[1] // L, L, *a.shape[2:])
    return jnp.swapaxes(jnp.moveaxis(r, 1, 0), 2, 3)


def from_chunks(a):
    r = jnp.moveaxis(jnp.swapaxes(a, 2, 3), 0, 1)
    return r.reshape(r.shape[0], r.shape[1] * r.shape[2], *r.shape[3:])


def mlstm_mixer(h, w_in, b_gate, norm_w, w_out):
    bsz, seq, _ = h.shape
    qk, vw = A_HEADS * A_DK, A_HEADS * A_DV
    proj = (h @ w_in).astype(F32)
    q = proj[..., :qk].reshape(bsz, seq, A_HEADS, A_DK)
    k = proj[..., qk:2 * qk].reshape(bsz, seq, A_HEADS, A_DK) * (A_DK ** -0.5)
    v = proj[..., 2 * qk:2 * qk + vw].reshape(bsz, seq, A_HEADS, A_DV)
    o = jax.nn.sigmoid(proj[..., 2 * qk + vw:2 * qk + 2 * vw]).reshape(bsz, seq, A_HEADS, A_DV)
    gates = proj[..., 2 * qk + 2 * vw:] + b_gate.astype(F32)
    i_pre = gates[..., :A_HEADS]
    log_f = jax.nn.log_sigmoid(gates[..., A_HEADS:])
    causal = jnp.tril(jnp.ones((A_CHUNK, A_CHUNK), bool))

    def step(carry, xs):
        C, n, m = carry
        qc, kc, vc, ic, fc = xs
        b = jnp.cumsum(fc, axis=-1)
        dmat = jnp.where(causal, b[..., :, None] - b[..., None, :] + ic[..., None, :], -jnp.inf)
        m_inter = b + m[..., None]
        m_t = jnp.maximum(m_inter, dmat.max(-1))
        a = jnp.einsum('bhtd,bhsd->bhts', qc, kc) * jnp.exp(dmat - m_t[..., None])
        w_inter = jnp.exp(m_inter - m_t)
        num = jnp.einsum('bhts,bhse->bhte', a, vc) + w_inter[..., None] * jnp.einsum('bhtd,bhde->bhte', qc, C)
        den = a.sum(-1) + w_inter * jnp.einsum('bhtd,bhd->bht', qc, n)
        h_out = num / jnp.maximum(jnp.abs(den), jnp.exp(-m_t))[..., None]
        b_last = b[..., -1]
        g = b_last[..., None] - b + ic
        m_new = jnp.maximum(b_last + m, g.max(-1))
        decay = jnp.exp(b_last + m - m_new)
        ws = jnp.exp(g - m_new[..., None])
        C_new = decay[..., None, None] * C + jnp.einsum('bhs,bhsd,bhse->bhde', ws, kc, vc)
        n_new = decay[..., None] * n + jnp.einsum('bhs,bhsd->bhd', ws, kc)
        return (C_new, n_new, m_new), h_out

    init = (jnp.zeros((bsz, A_HEADS, A_DK, A_DV), F32),
            jnp.zeros((bsz, A_HEADS, A_DK), F32),
            jnp.zeros((bsz, A_HEADS), F32))
    xs = (to_chunks(q, A_CHUNK), to_chunks(k, A_CHUNK), to_chunks(v, A_CHUNK),
          to_chunks(i_pre, A_CHUNK), to_chunks(log_f, A_CHUNK))
    _, ys = lax.scan(step, init, xs)
    hs = from_chunks(ys)
    y = head_rms_norm(hs, norm_w.reshape(A_HEADS, A_DV)) * o
    return y.reshape(bsz, seq, vw).astype(w_out.dtype) @ w_out


def swa_mixer(h, positions, w_in, sinks, w_out):
    bsz, seq, _ = h.shape
    nb = seq // B_BLOCK
    grp = B_Q_HEADS // B_KV_HEADS
    qw, kw = B_Q_HEADS * B_HEAD_DIM, B_KV_HEADS * B_HEAD_DIM
    proj = h @ w_in
    q = partial_rope(proj[..., :qw].reshape(bsz, seq, B_Q_HEADS, B_HEAD_DIM), positions)
    k = partial_rope(proj[..., qw:qw + kw].reshape(bsz, seq, B_KV_HEADS, B_HEAD_DIM), positions)
    v = proj[..., qw + kw:].reshape(bsz, seq, B_KV_HEADS, B_HEAD_DIM)
    qb = jnp.moveaxis(q.reshape(bsz, nb, B_BLOCK, B_KV_HEADS, grp, B_HEAD_DIM), 1, 0)

    def band(a):
        ap = jnp.pad(a, ((0, 0), (B_BLOCK, 0), (0, 0), (0, 0)))
        ap = ap.reshape(bsz, nb + 1, B_BLOCK, B_KV_HEADS, B_HEAD_DIM)
        return jnp.moveaxis(jnp.concatenate([ap[:, :-1], ap[:, 1:]], axis=2), 1, 0)

    kb, vb = band(k), band(v)
    sink = sinks.astype(F32).reshape(1, B_KV_HEADS, grp, 1, 1)
    scale = B_HEAD_DIM ** -0.5

    def block(args):
        qi, ki, vi, n = args
        s = jnp.einsum('bqhgd,bkhd->bhgqk', qi, ki, preferred_element_type=F32) * scale
        qpos = n * B_BLOCK + jnp.arange(B_BLOCK)
        kpos = (n - 1) * B_BLOCK + jnp.arange(2 * B_BLOCK)
        rel = qpos[:, None] - kpos[None, :]
        valid = (rel >= 0) & (rel < B_WINDOW) & (kpos >= 0)[None, :]
        s = jnp.where(valid, s, -jnp.inf)
        mx = jnp.maximum(s.max(-1, keepdims=True), sink)
        p = jnp.exp(s - mx)
        p = p / (p.sum(-1, keepdims=True) + jnp.exp(sink - mx))
        return jnp.einsum('bhgqk,bkhd->bqhgd', p, vi.astype(F32))

    out = lax.map(block, (qb, kb, vb, jnp.arange(nb)))
    out = jnp.moveaxis(out, 0, 1).reshape(bsz, seq, qw)
    return out.astype(w_out.dtype) @ w_out


def hgrn2_mixer(h, lower_bound, w_in, norm_w, w_out):
    bsz, seq, _ = h.shape
    kw, vw = C_HEADS * C_DK, C_HEADS * C_DV
    proj = (h @ w_in).astype(F32)
    q = proj[..., :kw]
    f_pre = proj[..., kw:2 * kw]
    i_in = proj[..., 2 * kw:2 * kw + vw]
    g_out = proj[..., 2 * kw + vw:]
    lb = lower_bound.astype(F32)
    log_f = jnp.logaddexp(jnp.log(lb), jnp.log1p(-lb) + jax.nn.log_sigmoid(f_pre))
    k = (1.0 - lb) * jax.nn.sigmoid(-f_pre)
    heads = lambda a, d: a.reshape(bsz, seq, C_HEADS, d)
    causal = jnp.tril(jnp.ones((C_CHUNK, C_CHUNK), bool))[:, :, None]

    def step(S, xs):
        qc, kc, vc, fc = xs
        cf = jnp.cumsum(fc, axis=2)
        diff = cf[:, :, :, None, :] - cf[:, :, None, :, :]
        decay = jnp.exp(jnp.where(causal, diff, -jnp.inf))
        a = jnp.einsum('bhtd,bhtsd,bhsd->bhts', qc, decay, kc)
        o = jnp.einsum('bhts,bhse->bhte', a, vc) + jnp.einsum('bhtd,bhde->bhte', qc * jnp.exp(cf), S)
        last = cf[:, :, -1:, :]
        S_new = jnp.exp(last[:, :, 0, :])[..., None] * S + jnp.einsum('bhsd,bhse->bhde', kc * jnp.exp(last - cf), vc)
        return S_new, o

    xs = (to_chunks(heads(q, C_DK), C_CHUNK), to_chunks(heads(k, C_DK), C_CHUNK),
          to_chunks(heads(i_in, C_DV), C_CHUNK), to_chunks(heads(log_f, C_DK), C_CHUNK))
    _, ys = lax.scan(step, jnp.zeros((bsz, C_HEADS, C_DK, C_DV), F32), xs)
    o = from_chunks(ys)
    y = head_rms_norm(o, norm_w.reshape(C_HEADS, C_DV)) * jax.nn.silu(heads(g_out, C_DV))
    return y.reshape(bsz, seq, vw).astype(w_out.dtype) @ w_out


def diff_mixer(h, positions, lam_init, w_in, lam_vec, norm_w, w_out):
    bsz, seq, _ = h.shape
    nb = seq // D_QBLOCK
    w = D_HEADS * 2 * D_HEAD_DIM
    proj = h @ w_in
    q = partial_rope(proj[..., :w].reshape(bsz, seq, 2 * D_HEADS, D_HEAD_DIM), positions)
    k = partial_rope(proj[..., w:2 * w].reshape(bsz, seq, 2 * D_HEADS, D_HEAD_DIM), positions)
    q = q.reshape(bsz, seq, D_HEADS, 2, D_HEAD_DIM)
    k = k.reshape(bsz, seq, D_HEADS, 2, D_HEAD_DIM)
    v = proj[..., 2 * w:].reshape(bsz, seq, D_HEADS, 2 * D_HEAD_DIM).astype(F32)
    lv = lam_vec.astype(F32)
    lam = jnp.exp(jnp.sum(lv[0] * lv[1])) - jnp.exp(jnp.sum(lv[2] * lv[3])) + lam_init
    qb = jnp.moveaxis(q.reshape(bsz, nb, D_QBLOCK, D_HEADS, 2, D_HEAD_DIM), 1, 0)
    kpos = jnp.arange(seq)
    scale = D_HEAD_DIM ** -0.5

    def block(args):
        qi, n = args
        s = jnp.einsum('bqhcd,bkhcd->bhcqk', qi, k, preferred_element_type=F32) * scale
        qpos = n * D_QBLOCK + jnp.arange(D_QBLOCK)
        s = jnp.where(kpos[None, :] <= qpos[:, None], s, -jnp.inf)
        p = jax.nn.softmax(s, axis=-1)
        a = p[:, :, 0] - lam * p[:, :, 1]
        return jnp.einsum('bhqk,bkhe->bqhe', a, v)

    out = lax.map(block, (qb, jnp.arange(nb)))
    out = jnp.moveaxis(out, 0, 1).reshape(bsz, seq, D_HEADS, 2 * D_HEAD_DIM)
    out = head_rms_norm(out, norm_w) * (1.0 - lam_init)
    return out.reshape(bsz, seq, w).astype(w_out.dtype) @ w_out


def swiglu(h, w_in, w_out):
    gu = h @ w_in
    return (jax.nn.silu(gu[..., :FFN_HIDDEN]) * gu[..., FFN_HIDDEN:]) @ w_out


def setup_inputs(seed: int = 0) -> dict:
    key = jax.random.key(seed)
    ks = jax.random.split(key, 26)
    nrm = lambda k, shape, s: jax.random.normal(k, shape, F32) * s
    D = D_MODEL
    beta = DEEPNORM_BETA
    x = nrm(ks[0], (BATCH, SEQ, D), 1.0)
    c = nrm(ks[1], (BATCH, D), 1.0)
    positions = jnp.broadcast_to(jnp.arange(SEQ, dtype=jnp.int32), (BATCH, SEQ))
    ada_w = nrm(ks[2], (DEPTH, 2, D, 3 * D), 0.1 * D ** -0.5)
    ada_b = nrm(ks[3], (DEPTH, 2, 3 * D), 0.01)
    ln_g = 1.0 + nrm(ks[4], (DEPTH, 2, D), 0.02)
    ln_b = nrm(ks[5], (DEPTH, 2, D), 0.01)
    mlstm_w_in = nrm(ks[6], (N_A, D, A_IN), D ** -0.5)
    mlstm_b_gate = jnp.concatenate(
        [nrm(ks[7], (N_A, A_HEADS), 0.1),
         jnp.linspace(3.0, 6.0, A_HEADS, dtype=F32) + nrm(ks[8], (N_A, A_HEADS), 0.1)], axis=-1)
    mlstm_norm = 1.0 + nrm(ks[9], (N_A, A_HEADS * A_DV), 0.02)
    mlstm_w_out = nrm(ks[10], (N_A, A_HEADS * A_DV, D), beta * (A_HEADS * A_DV) ** -0.5)
    swa_w_in = nrm(ks[11], (N_B, D, B_IN), D ** -0.5)
    swa_sinks = nrm(ks[12], (N_B, B_Q_HEADS), 0.5)
    swa_w_out = nrm(ks[13], (N_B, B_Q_HEADS * B_HEAD_DIM, D), beta * (B_Q_HEADS * B_HEAD_DIM) ** -0.5)
    hgrn_w_in = nrm(ks[14], (N_C, D, C_IN), D ** -0.5)
    hgrn_lower_bounds = nrm(ks[15], (DEPTH, C_HEADS * C_DK), 0.1)
    hgrn_norm = 1.0 + nrm(ks[16], (N_C, C_HEADS * C_DV), 0.02)
    hgrn_w_out = nrm(ks[17], (N_C, C_HEADS * C_DV, D), beta * (C_HEADS * C_DV) ** -0.5)
    diff_w_in = nrm(ks[18], (N_D, D, D_IN), D ** -0.5)
    diff_lambda = nrm(ks[19], (N_D, 4, D_HEAD_DIM), 0.1)
    diff_norm = 1.0 + nrm(ks[20], (N_D, 2 * D_HEAD_DIM), 0.02)
    diff_w_out = nrm(ks[21], (N_D, D_HEADS * 2 * D_HEAD_DIM, D), beta * (D_HEADS * 2 * D_HEAD_DIM) ** -0.5)
    ffn_w_in = nrm(ks[22], (DEPTH, D, 2 * FFN_HIDDEN), D ** -0.5)
    ffn_w_out = nrm(ks[23], (DEPTH, FFN_HIDDEN, D), beta * FFN_HIDDEN ** -0.5)
    return {"x": x, "c": c, "positions": positions, "ada_w": ada_w, "ada_b": ada_b,
            "ln_g": ln_g, "ln_b": ln_b,
            "mlstm_w_in": mlstm_w_in, "mlstm_b_gate": mlstm_b_gate, "mlstm_norm": mlstm_norm,
            "mlstm_w_out": mlstm_w_out,
            "swa_w_in": swa_w_in, "swa_sinks": swa_sinks, "swa_w_out": swa_w_out,
            "hgrn_w_in": hgrn_w_in, "hgrn_lower_bounds": hgrn_lower_bounds, "hgrn_norm": hgrn_norm,
            "hgrn_w_out": hgrn_w_out,
            "diff_w_in": diff_w_in, "diff_lambda": diff_lambda, "diff_norm": diff_norm,
            "diff_w_out": diff_w_out,
            "ffn_w_in": ffn_w_in, "ffn_w_out": ffn_w_out}


def reference(x, c, positions, ada_w, ada_b, ln_g, ln_b,
              mlstm_w_in, mlstm_b_gate, mlstm_norm, mlstm_w_out,
              swa_w_in, swa_sinks, swa_w_out,
              hgrn_w_in, hgrn_lower_bounds, hgrn_norm, hgrn_w_out,
              diff_w_in, diff_lambda, diff_norm, diff_w_out,
              ffn_w_in, ffn_w_out):
    cond = jax.nn.silu(c.astype(F32))
    lbs = jnp.cumsum(jax.nn.softmax(hgrn_lower_bounds.astype(F32), axis=0), axis=0)
    lbs = lbs - lbs[0]
    for i in range(DEPTH):
        mixer, j = i % N_MIXERS, i // N_MIXERS
        mod = jnp.einsum('bd,sde->sbe', cond, ada_w[i].astype(F32)) + ada_b[i][:, None, :].astype(F32)
        shift, scale, gate = jnp.split(mod.astype(x.dtype), 3, axis=-1)
        h = x * (1 + scale[0][:, None]) + shift[0][:, None]
        if mixer == 0:
            y = mlstm_mixer(h, mlstm_w_in[j], mlstm_b_gate[j], mlstm_norm[j], mlstm_w_out[j])
        elif mixer == 1:
            y = swa_mixer(h, positions, swa_w_in[j], swa_sinks[j], swa_w_out[j])
        elif mixer == 2:
            y = hgrn2_mixer(h, lbs[i], hgrn_w_in[j], hgrn_norm[j], hgrn_w_out[j])
        else:
            lam_init = 0.8 - 0.6 * math.exp(-0.3 * i)
            y = diff_mixer(h, positions, lam_init, diff_w_in[j], diff_lambda[j], diff_norm[j], diff_w_out[j])
        x = layer_norm(DEEPNORM_ALPHA * x + (1 + gate[0][:, None]) * y.astype(x.dtype), ln_g[i, 0], ln_b[i, 0])
        h = x * (1 + scale[1][:, None]) + shift[1][:, None]
        y = swiglu(h, ffn_w_in[i], ffn_w_out[i])
        x = layer_norm(DEEPNORM_ALPHA * x + (1 + gate[1][:, None]) * y.astype(x.dtype), ln_g[i, 1], ln_b[i, 1])
    return x
```

```python
import functools
import math

import jax
import jax.numpy as jnp
from jax import lax
from jax.experimental import pallas as pl
from jax.experimental.pallas import tpu as pltpu

F32 = jnp.float32
BF16 = jnp.bfloat16

A_HEADS = 4
B_HEAD_DIM = 64
B_GROUP = 8
B_BLOCK = 128
C_HEADS = 8
D_HEAD_DIM = 64
ROPE_THETA = 500000.0
ROT_FRAC = 4
LN_EPS = 1e-5
RMS_EPS = 1e-6
LANES = 128
VMEM_LIMIT = 52 * 2 ** 20

TOKEN_TILE = 512
MLSTM_CHUNK = 64
MLSTM_STEP = 512
HGRN_CHUNK = 16
HGRN_STEP = 512
DIFF_BLOCK = 512


def _params(*sem):
    return pltpu.CompilerParams(dimension_semantics=sem, vmem_limit_bytes=VMEM_LIMIT)


def _dot(a, b):
    return jnp.dot(a, b, preferred_element_type=F32)


def _dot_nt(a, b):
    return lax.dot_general(a, b, (((1,), (1,)), ((), ())), preferred_element_type=F32)


def _split3(x):
    hi = x.astype(BF16)
    r = x - hi.astype(F32)
    mid = r.astype(BF16)
    lo = (r - mid.astype(F32)).astype(BF16)
    return hi, mid, lo


def _dot_exact_lhs01(mask_bf16, x):
    hi, mid, lo = _split3(x)
    return _dot(mask_bf16, hi) + (_dot(mask_bf16, mid) + _dot(mask_bf16, lo))


def _sigmoid(x):
    return 1.0 / (1.0 + jnp.exp(-x))


def _log_sigmoid(x):
    return jnp.minimum(x, 0.0) - jnp.log1p(jnp.exp(-jnp.abs(x)))


def _layer_norm(z, g, b):
    mu = jnp.mean(z, axis=-1, keepdims=True)
    zc = z - mu
    var = jnp.mean(zc * zc, axis=-1, keepdims=True)
    return zc * lax.rsqrt(var + LN_EPS) * g + b


def _rms_norm(h, w):
    return h * lax.rsqrt(jnp.mean(h * h, axis=-1, keepdims=True) + RMS_EPS) * w


def _const_spec(shape):
    nd = len(shape)
    return pl.BlockSpec(shape, lambda *_: (0,) * nd)


def _mod_spec(sub, part, tiles_per_batch, d):
    return pl.BlockSpec((None, None, None, 1, d), lambda i, *_: (sub, part, i // tiles_per_batch, 0, 0))


def _mod_kernel(c_ref, w_ref, b_ref, o_ref):
    c = c_ref[...]
    cond = c * _sigmoid(c)
    w = w_ref[...]
    a_hi = cond.astype(BF16)
    a_lo = (cond - a_hi.astype(F32)).astype(BF16)
    w_hi = w.astype(BF16)
    w_lo = (w - w_hi.astype(F32)).astype(BF16)
    acc = _dot(a_hi, w_hi) + (_dot(a_hi, w_lo) + _dot(a_lo, w_hi))
    o_ref[...] = acc + b_ref[...]


def _modulation(c, ada_w, ada_b):
    depth, _, d, d3 = ada_w.shape
    bsz = c.shape[0]
    nsub = depth * 2
    w = ada_w.reshape(nsub, d, d3)
    b = ada_b.reshape(nsub, 1, d3)
    out = pl.pallas_call(
        _mod_kernel,
        grid=(nsub, 3),
        in_specs=[
            pl.BlockSpec((bsz, d), lambda l, j: (0, 0)),
            pl.BlockSpec((None, d, d), lambda l, j: (l, 0, j)),
            pl.BlockSpec((None, 1, d), lambda l, j: (l, 0, j)),
        ],
        out_specs=pl.BlockSpec((None, None, bsz, d), lambda l, j: (l, j, 0, 0)),
        out_shape=jax.ShapeDtypeStruct((nsub, 3, bsz, d), F32),
        compiler_params=_params("arbitrary", "arbitrary"),
        name="adaln_mod",
    )(c, w, b)
    return out.reshape(nsub, 3, bsz, 1, d)


def _rope_kernel(pos_ref, inv_ref, sign_ref, cos_ref, sin_ref):
    ang = pos_ref[...].astype(F32) * inv_ref[...]
    cos_ref[...] = jnp.cos(ang)
    sin_ref[...] = jnp.sin(ang) * sign_ref[...]


def _rope_tables(positions):
    t = positions.size
    rot = B_HEAD_DIM // ROT_FRAC
    half = rot // 2
    inv = jnp.power(ROPE_THETA, -jnp.arange(half, dtype=F32) * 2.0 / rot)
    lane = jnp.arange(LANES) % B_HEAD_DIM
    inv_l = jnp.where(lane < rot, inv[lane % half], 0.0).reshape(1, LANES)
    sign_l = jnp.where(lane < half, -1.0, jnp.where(lane < rot, 1.0, 0.0)).astype(F32).reshape(1, LANES)
    tm = min(TOKEN_TILE, t)
    return pl.pallas_call(
        _rope_kernel,
        grid=(t // tm,),
        in_specs=[pl.BlockSpec((tm, 1), lambda i: (i, 0)), _const_spec((1, LANES)), _const_spec((1, LANES))],
        out_specs=[pl.BlockSpec((tm, LANES), lambda i: (i, 0))] * 2,
        out_shape=[jax.ShapeDtypeStruct((t, LANES), F32)] * 2,
        compiler_params=_params("arbitrary"),
        name="rope_tables",
    )(positions.reshape(t, 1), inv_l, sign_l)


def _rope_slab(slab, cos, sin_signed, lo_mask):
    fwd = pltpu.roll(slab, LANES - 8, axis=1)
    bwd = pltpu.roll(slab, 8, axis=1)
    return slab * cos + jnp.where(lo_mask, fwd, bwd) * sin_signed


def _rope_lo_mask(tm):
    lane = lax.broadcasted_iota(jnp.int32, (tm, LANES), 1) % B_HEAD_DIM
    return lane < (B_HEAD_DIM // ROT_FRAC // 2)


def _modulated(x_ref, sh_ref, sc_ref):
    return (x_ref[...] * (1.0 + sc_ref[...]) + sh_ref[...]).astype(BF16)


def _mlstm_in_kernel(x_ref, sh_ref, sc_ref, w_ref, bg_ref, q_ref, k_ref, v_ref, o_ref, g_ref, *, qk, vw):
    h = _modulated(x_ref, sh_ref, sc_ref)
    dk = qk // A_HEADS
    q_ref[...] = _dot(h, w_ref[:, :qk]).astype(BF16)
    k_ref[...] = (_dot(h, w_ref[:, qk:2 * qk]) * (dk ** -0.5)).astype(BF16)
    v_ref[...] = _dot(h, w_ref[:, 2 * qk:2 * qk + vw]).astype(BF16)
    o_ref[...] = _sigmoid(_dot(h, w_ref[:, 2 * qk + vw:2 * qk + 2 * vw]))
    g_ref[...] = _dot(h, w_ref[:, 2 * qk + 2 * vw:]) + bg_ref[...]


def _attn_in_kernel(x_ref, sh_ref, sc_ref, w_ref, cos_ref, sin_ref, q_ref, k_ref, v_ref, *, qw, kw, scale):
    h = _modulated(x_ref, sh_ref, sc_ref)
    cos = cos_ref[...]
    sin = sin_ref[...]
    lo = _rope_lo_mask(h.shape[0])
    for j in range(qw // LANES):
        slab = _dot(h, w_ref[:, j * LANES:(j + 1) * LANES])
        q_ref[:, j * LANES:(j + 1) * LANES] = (_rope_slab(slab, cos, sin, lo) * scale).astype(BF16)
    for j in range(kw // LANES):
        slab = _dot(h, w_ref[:, qw + j * LANES:qw + (j + 1) * LANES])
        k_ref[:, j * LANES:(j + 1) * LANES] = _rope_slab(slab, cos, sin, lo).astype(BF16)
    v_ref[...] = _dot(h, w_ref[:, qw + kw:]).astype(BF16)


def _hgrn_in_kernel(x_ref, sh_ref, sc_ref, w_ref, q_ref, f_ref, v_ref, g_ref, *, kw, vw):
    h = _modulated(x_ref, sh_ref, sc_ref)
    q_ref[...] = _dot(h, w_ref[:, :kw])
    f_ref[...] = _dot(h, w_ref[:, kw:2 * kw])
    v_ref[...] = _dot(h, w_ref[:, 2 * kw:2 * kw + vw]).astype(BF16)
    g = _dot(h, w_ref[:, 2 * kw + vw:])
    g_ref[...] = g * _sigmoid(g)


def _in_proj(body, x, mod, sub, seq, w, extra, extra_specs, outs):
    t, d = x.shape
    tm = min(TOKEN_TILE, seq)
    tpb = seq // tm
    row = lambda width: pl.BlockSpec((tm, width), lambda i: (i, 0))
    return pl.pallas_call(
        body,
        grid=(t // tm,),
        in_specs=[row(d), _mod_spec(sub, 0, tpb, d), _mod_spec(sub, 1, tpb, d), _const_spec(w.shape)] + extra_specs,
        out_specs=[row(wd) for wd, _ in outs],
        out_shape=[jax.ShapeDtypeStruct((t, wd), dt) for wd, dt in outs],
        compiler_params=_params("arbitrary"),
        name=body.func.__name__.strip("_") if isinstance(body, functools.partial) else body.__name__.strip("_"),
    )(x, mod, mod, w, *extra)


def _out_kernel(y_ref, w_ref, x_ref, gate_ref, g_ref, b_ref, o_ref, *, alpha):
    y = _dot(y_ref[...], w_ref[...])
    z = alpha * x_ref[...] + (1.0 + gate_ref[...]) * y
    o_ref[...] = _layer_norm(z, g_ref[...], b_ref[...])


def _out_proj(y, w, x, mod, sub, seq, ln_g, ln_b, alpha):
    t, d = x.shape
    tm = min(TOKEN_TILE, seq)
    tpb = seq // tm
    row = lambda width: pl.BlockSpec((tm, width), lambda i: (i, 0))
    return pl.pallas_call(
        functools.partial(_out_kernel, alpha=alpha),
        grid=(t // tm,),
        in_specs=[row(y.shape[1]), _const_spec(w.shape), row(d), _mod_spec(sub, 2, tpb, d),
                  _const_spec((1, d)), _const_spec((1, d))],
        out_specs=row(d),
        out_shape=jax.ShapeDtypeStruct((t, d), F32),
        compiler_params=_params("arbitrary"),
        name="out_proj_ln",
    )(y, w, x, mod, ln_g.reshape(1, d), ln_b.reshape(1, d))


def _ffn_kernel(x_ref, sh_ref, sc_ref, gate_ref, wi_ref, wo_ref, g_ref, b_ref, o_ref, *, hidden, chunk, alpha):
    x = x_ref[...]
    h = (x * (1.0 + sc_ref[...]) + sh_ref[...]).astype(BF16)
    acc = jnp.zeros(x.shape, F32)
    for j in range(hidden // chunk):
        a = _dot(h, wi_ref[:, j * chunk:(j + 1) * chunk])
        u = _dot(h, wi_ref[:, hidden + j * chunk:hidden + (j + 1) * chunk])
        act = (a * _sigmoid(a) * u).astype(BF16)
        acc = acc + _dot(act, wo_ref[j * chunk:(j + 1) * chunk, :])
    z = alpha * x + (1.0 + gate_ref[...]) * acc
    o_ref[...] = _layer_norm(z, g_ref[...], b_ref[...])


def _ffn_chunk(hidden):
    best = LANES
    for c in range(LANES, hidden + 1, LANES):
        if hidden % c == 0 and c <= 1536:
            best = c
    return best


def _ffn(x, mod, sub, seq, w_in, w_out, ln_g, ln_b, alpha):
    t, d = x.shape
    hidden = w_out.shape[0]
    tm = min(TOKEN_TILE, seq)
    tpb = seq // tm
    row = pl.BlockSpec((tm, d), lambda i: (i, 0))
    return pl.pallas_call(
        functools.partial(_ffn_kernel, hidden=hidden, chunk=_ffn_chunk(hidden), alpha=alpha),
        grid=(t // tm,),
        in_specs=[row, _mod_spec(sub, 0, tpb, d), _mod_spec(sub, 1, tpb, d), _mod_spec(sub, 2, tpb, d),
                  _const_spec(w_in.shape), _const_spec(w_out.shape), _const_spec((1, d)), _const_spec((1, d))],
        out_specs=row,
        out_shape=jax.ShapeDtypeStruct((t, d), F32),
        compiler_params=_params("arbitrary"),
        name="ffn_swiglu_ln",
    )(x, mod, mod, mod, w_in, w_out, ln_g.reshape(1, d), ln_b.reshape(1, d))


def _mlstm_kernel(q_ref, k_ref, v_ref, o_ref, g_ref, nw_ref, y_ref, c_ref, n_ref, m_ref, *, chunk, dk, dv):
    L = chunk
    nchunks = q_ref.shape[0] // L

    @pl.when(pl.program_id(1) == 0)
    def _():
        c_ref[...] = jnp.zeros_like(c_ref)
        n_ref[...] = jnp.zeros_like(n_ref)
        m_ref[...] = jnp.zeros_like(m_ref)

    row_i = lax.broadcasted_iota(jnp.int32, (L, L), 0)
    col_i = lax.broadcasted_iota(jnp.int32, (L, L), 1)
    causal = col_i <= row_i
    tril = jnp.where(causal, 1.0, 0.0).astype(BF16)
    lane = lax.broadcasted_iota(jnp.int32, (L, LANES), 1)

    def body(c, carry):
        rows = pl.ds(pl.multiple_of(c * L, L), L)
        g = g_ref[rows, :]
        logf = jnp.where(lane >= A_HEADS, _log_sigmoid(g), 0.0)
        bcum = _dot_exact_lhs01(tril, logf)
        z = jnp.where(lane < A_HEADS, g, bcum)
        zt = z.T
        for h in range(A_HEADS):
            i_col = z[:, h:h + 1]
            b_col = z[:, A_HEADS + h:A_HEADS + h + 1]
            i_row = zt[h:h + 1, :]
            b_row = zt[A_HEADS + h:A_HEADS + h + 1, :]
            m_prev = m_ref[h, 0:1, 0:1]
            dmat = jnp.where(causal, b_col - b_row + i_row, -jnp.inf)
            m_inter = b_col + m_prev
            m_t = jnp.maximum(m_inter, jnp.max(dmat, axis=-1, keepdims=True))
            qh = q_ref[rows, h * dk:(h + 1) * dk]
            kh = k_ref[rows, h * dk:(h + 1) * dk]
            vh = v_ref[rows, h * dv:(h + 1) * dv]
            a = _dot_nt(qh, kh) * jnp.exp(dmat - m_t)
            w_inter = jnp.exp(m_inter - m_t)
            c_prev = c_ref[h]
            n_prev = n_ref[h, 0:1, :]
            num = _dot(a.astype(BF16), vh) + w_inter * _dot(qh, c_prev.astype(BF16))
            qn = jnp.sum(qh.astype(F32) * n_prev, axis=-1, keepdims=True)
            den = jnp.sum(a, axis=-1, keepdims=True) + w_inter * qn
            hout = num / jnp.maximum(jnp.abs(den), jnp.exp(-m_t))
            y = _rms_norm(hout, nw_ref[:, h * dv:(h + 1) * dv]) * o_ref[rows, h * dv:(h + 1) * dv]
            y_ref[rows, h * dv:(h + 1) * dv] = y.astype(y_ref.dtype)
            b_last = b_col[L - 1:L, :]
            g_col = b_last - b_col + i_col
            m_new = jnp.maximum(b_last + m_prev, jnp.max(g_col, axis=0, keepdims=True))
            decay = jnp.exp(b_last + m_prev - m_new)
            kws = kh.astype(F32) * jnp.exp(g_col - m_new)
            c_ref[h] = decay * c_prev + _dot(kws.T.astype(BF16), vh)
            n_ref[h, 0:1, :] = decay * n_prev + jnp.sum(kws, axis=0, keepdims=True)
            m_ref[h] = jnp.broadcast_to(m_new, m_ref.shape[1:])
        return carry

    lax.fori_loop(0, nchunks, body, 0)


def _mlstm(q, k, v, o, g, norm_w, bsz, seq):
    t = q.shape[0]
    qk, vw = q.shape[1], v.shape[1]
    dk, dv = qk // A_HEADS, vw // A_HEADS
    ts = min(MLSTM_STEP, seq)
    spb = seq // ts
    row = lambda width: pl.BlockSpec((ts, width), lambda b, s: (b * spb + s, 0))
    return pl.pallas_call(
        functools.partial(_mlstm_kernel, chunk=min(MLSTM_CHUNK, ts), dk=dk, dv=dv),
        grid=(bsz, spb),
        in_specs=[row(qk), row(qk), row(vw), row(vw), row(LANES), _const_spec((1, vw))],
        out_specs=row(vw),
        out_shape=jax.ShapeDtypeStruct((t, vw), BF16),
        scratch_shapes=[pltpu.VMEM((A_HEADS, dk, dv), F32), pltpu.VMEM((A_HEADS, 8, dk), F32),
                        pltpu.VMEM((A_HEADS, 8, LANES), F32)],
        compiler_params=_params("arbitrary", "arbitrary"),
        name="mlstm_mixer",
    )(q, k, v, o, g, norm_w.reshape(1, vw))


def _swa_kernel(sink_ref, q_ref, kc_ref, kp_ref, vc_ref, vp_ref, o_ref, *, heads, hd, group):
    n = pl.program_id(1)
    blk = q_ref.shape[0]
    kb = jnp.concatenate([kp_ref[...], kc_ref[...]], axis=0)
    vb = jnp.concatenate([vp_ref[...], vc_ref[...]], axis=0)
    q_i = lax.broadcasted_iota(jnp.int32, (blk, 2 * blk), 0)
    k_i = lax.broadcasted_iota(jnp.int32, (blk, 2 * blk), 1)
    valid = (k_i > q_i) & (k_i <= q_i + blk) & ((k_i >= blk) | (n > 0))
    outs = []
    for j in range(heads):
        gi = j // group
        s = _dot_nt(q_ref[:, j * hd:(j + 1) * hd], kb[:, gi * hd:(gi + 1) * hd])
        s = jnp.where(valid, s, -jnp.inf)
        sink = sink_ref[j]
        mx = jnp.maximum(jnp.max(s, axis=-1, keepdims=True), sink)
        p = jnp.exp(s - mx)
        denom = jnp.sum(p, axis=-1, keepdims=True) + jnp.exp(sink - mx)
        outs.append(_dot(p.astype(BF16), vb[:, gi * hd:(gi + 1) * hd]) / denom)
    o_ref[...] = jnp.concatenate(outs, axis=-1).astype(o_ref.dtype)


def _swa(q, k, v, sinks, bsz, seq):
    t, qw = q.shape
    kw = k.shape[1]
    heads = qw // B_HEAD_DIM
    blk = min(B_BLOCK, seq)
    nb = seq // blk
    cur = lambda width: pl.BlockSpec((blk, width), lambda b, n: (b * nb + n, 0))
    prev = lambda width: pl.BlockSpec((blk, width), lambda b, n: (b * nb + jnp.maximum(n - 1, 0), 0))
    return pl.pallas_call(
        functools.partial(_swa_kernel, heads=heads, hd=B_HEAD_DIM, group=B_GROUP),
        grid=(bsz, nb),
        in_specs=[pl.BlockSpec(memory_space=pltpu.SMEM), cur(qw), cur(kw), prev(kw), cur(kw), prev(kw)],
        out_specs=cur(qw),
        out_shape=jax.ShapeDtypeStruct((t, qw), BF16),
        compiler_params=_params("arbitrary", "arbitrary"),
        name="swa_mixer",
    )(sinks.astype(F32), q, k, k, v, v)


def _hgrn_kernel(q_ref, f_ref, v_ref, g_ref, lbp_ref, nw_ref, y_ref, st_ref, cf_ref, qs_ref, ks_ref, *,
                 chunk, dk, dv, layer):
    L = chunk
    nchunks = q_ref.shape[0] // L
    heads = q_ref.shape[1] // dk

    @pl.when(pl.program_id(1) == 0)
    def _():
        st_ref[...] = jnp.zeros_like(st_ref)

    lbp = lbp_ref[...]
    e = jnp.exp(lbp - jnp.max(lbp, axis=0, keepdims=True))
    sm = e / jnp.sum(e, axis=0, keepdims=True)
    lb = jnp.zeros_like(sm[0:1])
    for r in range(1, layer + 1):
        lb = lb + sm[r:r + 1]
    log_lb = jnp.log(lb)
    log_1m_lb = jnp.log1p(-lb)

    row_i = lax.broadcasted_iota(jnp.int32, (L, L), 0)
    col_i = lax.broadcasted_iota(jnp.int32, (L, L), 1)
    tril = jnp.where(col_i <= row_i, 1.0, 0.0).astype(BF16)
    ones = jnp.ones((dk, LANES), BF16)
    srow = lax.broadcasted_iota(jnp.int32, (L, dv), 0)

    def body(c, carry):
        rows = pl.ds(pl.multiple_of(c * L, L), L)
        fp = f_ref[rows, :]
        x2 = log_1m_lb + _log_sigmoid(fp)
        mx = jnp.maximum(log_lb, x2)
        logf = mx + jnp.log1p(jnp.exp(-jnp.abs(log_lb - x2)))
        cf = _dot_exact_lhs01(tril, logf)
        cf_ref[...] = cf
        qs_ref[...] = q_ref[rows, :]
        ks_ref[...] = (1.0 - lb) * _sigmoid(-fp)
        for h in range(heads):
            cs = slice(h * dk, (h + 1) * dk)
            cfh = cf_ref[:, cs]
            kh = ks_ref[:, cs]
            qh = qs_ref[:, cs]
            vh = v_ref[rows, h * dv:(h + 1) * dv]
            vf = vh.astype(F32)
            ws = []
            for tt in range(L):
                dec = jnp.exp(jnp.minimum(cf_ref[tt:tt + 1, cs] - cfh, 0.0))
                ws.append(dec * (kh * qs_ref[tt:tt + 1, cs]))
            w_all = jnp.concatenate(ws, axis=0).astype(BF16)
            r_all = _dot(w_all, ones)
            o_rows = []
            for tt in range(L):
                a_t = jnp.where(srow <= tt, r_all[tt * L:(tt + 1) * L, :], 0.0)
                o_rows.append(jnp.sum(a_t * vf, axis=0, keepdims=True))
            o_intra = jnp.concatenate(o_rows, axis=0)
            st = st_ref[h]
            o = o_intra + _dot_nt((qh * jnp.exp(cfh)).astype(BF16), st.astype(BF16))
            last = cf_ref[L - 1:L, cs]
            kdec = (kh * jnp.exp(last - cfh)).astype(BF16)
            st_ref[h] = jnp.exp(last) * st + _dot(vf.T.astype(BF16), kdec)
            y = _rms_norm(o, nw_ref[:, h * dv:(h + 1) * dv]) * g_ref[rows, h * dv:(h + 1) * dv]
            y_ref[rows, h * dv:(h + 1) * dv] = y.astype(y_ref.dtype)
        return carry

    lax.fori_loop(0, nchunks, body, 0)


def _hgrn(q, f, v, g, lower_bounds, layer, norm_w, bsz, seq):
    t, kw = q.shape
    vw = v.shape[1]
    dk, dv = kw // C_HEADS, vw // C_HEADS
    ts = min(HGRN_STEP, seq)
    spb = seq // ts
    chunk = min(HGRN_CHUNK, ts)
    row = lambda width: pl.BlockSpec((ts, width), lambda b, s: (b * spb + s, 0))
    return pl.pallas_call(
        functools.partial(_hgrn_kernel, chunk=chunk, dk=dk, dv=dv, layer=layer),
        grid=(bsz, spb),
        in_specs=[row(kw), row(kw), row(vw), row(vw), _const_spec(lower_bounds.shape), _const_spec((1, vw))],
        out_specs=row(vw),
        out_shape=jax.ShapeDtypeStruct((t, vw), BF16),
        scratch_shapes=[pltpu.VMEM((C_HEADS, dv, dk), F32), pltpu.VMEM((chunk, kw), F32),
                        pltpu.VMEM((chunk, kw), F32), pltpu.VMEM((chunk, kw), F32)],
        compiler_params=_params("arbitrary", "arbitrary"),
        name="hgrn2_mixer",
    )(q, f, v, g, lower_bounds, norm_w.reshape(1, vw))


def _diff_kernel(q_ref, k_ref, v_ref, lam_ref, nw_ref, o_ref, m_ref, l_ref, acc_ref, *, blk, hd, lam_init):
    qi = pl.program_id(2)
    lv = lam_ref[...]
    lam = (jnp.exp(jnp.sum(lv[0:1] * lv[1:2], axis=-1, keepdims=True))
           - jnp.exp(jnp.sum(lv[2:3] * lv[3:4], axis=-1, keepdims=True)) + lam_init)
    m_ref[...] = jnp.full_like(m_ref, -jnp.inf)
    l_ref[...] = jnp.zeros_like(l_ref)
    acc_ref[...] = jnp.zeros_like(acc_ref)
    q = q_ref[...]

    def step(j, masked):
        rows = pl.ds(pl.multiple_of(j * blk, blk), blk)
        kb = k_ref[rows, :]
        vb = v_ref[rows, :]
        if masked:
            r_i = lax.broadcasted_iota(jnp.int32, (blk, blk), 0)
            c_i = lax.broadcasted_iota(jnp.int32, (blk, blk), 1)
            keep = c_i <= r_i
        for c in range(2):
            s = _dot_nt(q[:, c * hd:(c + 1) * hd], kb[:, c * hd:(c + 1) * hd])
            if masked:
                s = jnp.where(keep, s, -jnp.inf)
            m_old = m_ref[c]
            m_new = jnp.maximum(m_old, jnp.max(s, axis=-1, keepdims=True))
            alpha = jnp.exp(m_old - m_new)
            p = jnp.exp(s - m_new)
            l_ref[c] = alpha * l_ref[c] + jnp.sum(p, axis=-1, keepdims=True)
            acc_ref[c] = alpha * acc_ref[c] + _dot(p.astype(BF16), vb)
            m_ref[c] = m_new

    def body(j, carry):
        step(j, False)
        return carry

    lax.fori_loop(0, qi, body, 0)
    step(qi, True)
    out = acc_ref[0] / l_ref[0] - lam * (acc_ref[1] / l_ref[1])
    o_ref[...] = (_rms_norm(out, nw_ref[...]) * (1.0 - lam_init)).astype(o_ref.dtype)


def _diff_attn(q, k, v, lam_vec, norm_w, lam_init, bsz, seq):
    t, w = q.shape
    hw = 2 * D_HEAD_DIM
    heads = w // hw
    blk = min(DIFF_BLOCK, seq)
    nq = seq // blk
    return pl.pallas_call(
        functools.partial(_diff_kernel, blk=blk, hd=D_HEAD_DIM, lam_init=lam_init),
        grid=(bsz, heads, nq),
        in_specs=[pl.BlockSpec((blk, hw), lambda b, h, i: (b * nq + i, h)),
                  pl.BlockSpec((seq, hw), lambda b, h, i: (b, h)),
                  pl.BlockSpec((seq, hw), lambda b, h, i: (b, h)),
                  _const_spec(lam_vec.shape), _const_spec((1, hw))],
        out_specs=pl.BlockSpec((blk, hw), lambda b, h, i: (b * nq + i, h)),
        out_shape=jax.ShapeDtypeStruct((t, w), BF16),
        scratch_shapes=[pltpu.VMEM((2, blk, 1), F32), pltpu.VMEM((2, blk, 1), F32), pltpu.VMEM((2, blk, hw), F32)],
        compiler_params=_params("arbitrary", "arbitrary", "arbitrary"),
        name="diff_attn_mixer",
    )(q, k, v, lam_vec, norm_w.reshape(1, hw))


def _pad_cols(w, width):
    return jnp.pad(w, ((0, 0), (0, width - w.shape[1])))


def kernel(x, c, positions, ada_w, ada_b, ln_g, ln_b, mlstm_w_in, mlstm_b_gate, mlstm_norm, mlstm_w_out, swa_w_in, swa_sinks, swa_w_out, hgrn_w_in, hgrn_lower_bounds, hgrn_norm, hgrn_w_out, diff_w_in, diff_lambda, diff_norm, diff_w_out, ffn_w_in, ffn_w_out):
    bsz, seq, d = x.shape
    depth = ada_w.shape[0]
    alpha = (2 * depth) ** 0.25
    t = bsz * seq
    xf = x.reshape(t, d)
    mod = _modulation(c, ada_w, ada_b)
    cos_t, sin_t = _rope_tables(positions)
    for i in range(depth):
        mixer, j = i % 4, i // 4
        sub = 2 * i
        if mixer == 0:
            vw = mlstm_w_out.shape[1]
            qk = vw // 2
            n_gate = 2 * A_HEADS
            w = _pad_cols(mlstm_w_in[j], 2 * qk + 2 * vw + LANES).astype(BF16)
            bg = _pad_cols(mlstm_b_gate[j].reshape(1, n_gate), LANES)
            q, k, v, o, g = _in_proj(
                functools.partial(_mlstm_in_kernel, qk=qk, vw=vw), xf, mod, sub, seq, w, [bg],
                [_const_spec((1, LANES))], [(qk, BF16), (qk, BF16), (vw, BF16), (vw, F32), (LANES, F32)])
            y = _mlstm(q, k, v, o, g, mlstm_norm[j], bsz, seq)
            w_out = mlstm_w_out[j]
        elif mixer == 1:
            qw = swa_w_out.shape[1]
            kw = qw // B_GROUP
            q, k, v = _in_proj(
                functools.partial(_attn_in_kernel, qw=qw, kw=kw, scale=B_HEAD_DIM ** -0.5), xf, mod, sub, seq,
                swa_w_in[j].astype(BF16), [cos_t, sin_t],
                [pl.BlockSpec((min(TOKEN_TILE, seq), LANES), lambda i: (i, 0))] * 2,
                [(qw, BF16), (kw, BF16), (kw, BF16)])
            y = _swa(q, k, v, swa_sinks[j], bsz, seq)
            w_out = swa_w_out[j]
        elif mixer == 2:
            vw = hgrn_w_out.shape[1]
            q, f, v, g = _in_proj(
                functools.partial(_hgrn_in_kernel, kw=vw, vw=vw), xf, mod, sub, seq,
                hgrn_w_in[j].astype(BF16), [], [], [(vw, F32), (vw, F32), (vw, BF16), (vw, F32)])
            y = _hgrn(q, f, v, g, hgrn_lower_bounds.astype(F32), i, hgrn_norm[j], bsz, seq)
            w_out = hgrn_w_out[j]
        else:
            w_ = diff_w_out.shape[1]
            lam_init = 0.8 - 0.6 * math.exp(-0.3 * i)
            q, k, v = _in_proj(
                functools.partial(_attn_in_kernel, qw=w_, kw=w_, scale=D_HEAD_DIM ** -0.5), xf, mod, sub, seq,
                diff_w_in[j].astype(BF16), [cos_t, sin_t],
                [pl.BlockSpec((min(TOKEN_TILE, seq), LANES), lambda i: (i, 0))] * 2,
                [(w_, BF16), (w_, BF16), (w_, BF16)])
            y = _diff_attn(q, k, v, diff_lambda[j].astype(F32), diff_norm[j], lam_init, bsz, seq)
            w_out = diff_w_out[j]
        xf = _out_proj(y, w_out.astype(BF16), xf, mod, sub, seq, ln_g[i, 0], ln_b[i, 0], alpha)
        xf = _ffn(xf, mod, sub + 1, seq, ffn_w_in[i].astype(BF16), ffn_w_out[i].astype(BF16),
                  ln_g[i, 1], ln_b[i, 1], alpha)
    return xf.reshape(bsz, seq, d)
```

```python
import functools
import math

import jax
import jax.numpy as jnp
from jax import lax
from jax.experimental import pallas as pl
from jax.experimental.pallas import tpu as pltpu

F32 = jnp.float32
BF16 = jnp.bfloat16

A_HEADS = 4
B_HEAD_DIM = 64
B_GROUP = 8
B_BLOCK = 128
C_HEADS = 8
D_HEAD_DIM = 64
ROPE_THETA = 500000.0
ROT_FRAC = 4
LN_EPS = 1e-5
RMS_EPS = 1e-6
LANES = 128
LOG2_E = math.log2(math.e)
VMEM_LIMIT = 52 * 2 ** 20

TOKEN_TILE = 512
MLSTM_CHUNK = 64
MLSTM_STEP = 512
HGRN_CHUNK = 16
HGRN_STEP = 512
HGRN_CUMSUM_ROWS = 128
HGRN_UNROLL = 1
DIFF_BLOCK = 512
DIFF_ONES_ROWS = 16


def _params(*sem):
    return pltpu.CompilerParams(dimension_semantics=sem, vmem_limit_bytes=VMEM_LIMIT)


def _dot(a, b):
    return jnp.dot(a, b, preferred_element_type=F32)


def _dot_nt(a, b):
    return lax.dot_general(a, b, (((1,), (1,)), ((), ())), preferred_element_type=F32)


def _split3(x):
    hi = x.astype(BF16)
    r = x - hi.astype(F32)
    mid = r.astype(BF16)
    lo = (r - mid.astype(F32)).astype(BF16)
    return hi, mid, lo


def _dot_exact_lhs01(mask_bf16, x):
    hi, mid, lo = _split3(x)
    return _dot(mask_bf16, hi) + (_dot(mask_bf16, mid) + _dot(mask_bf16, lo))


def _sigmoid(x):
    return 1.0 / (1.0 + jnp.exp(-x))


def _log_sigmoid(x):
    return jnp.minimum(x, 0.0) - jnp.log1p(jnp.exp(-jnp.abs(x)))


def _layer_norm(z, g, b):
    mu = jnp.mean(z, axis=-1, keepdims=True)
    zc = z - mu
    var = jnp.mean(zc * zc, axis=-1, keepdims=True)
    return zc * lax.rsqrt(var + LN_EPS) * g + b


def _rms_norm(h, w):
    return h * lax.rsqrt(jnp.mean(h * h, axis=-1, keepdims=True) + RMS_EPS) * w


def _const_spec(shape):
    nd = len(shape)
    return pl.BlockSpec(shape, lambda *_: (0,) * nd)


def _mod_spec(sub, part, tiles_per_batch, d):
    return pl.BlockSpec((None, None, None, 1, d), lambda i, *_: (sub, part, i // tiles_per_batch, 0, 0))


def _mod_kernel(c_ref, w_ref, b_ref, o_ref):
    c = c_ref[...]
    cond = c * _sigmoid(c)
    w = w_ref[...]
    a_hi = cond.astype(BF16)
    a_lo = (cond - a_hi.astype(F32)).astype(BF16)
    w_hi = w.astype(BF16)
    w_lo = (w - w_hi.astype(F32)).astype(BF16)
    acc = _dot(a_hi, w_hi) + (_dot(a_hi, w_lo) + _dot(a_lo, w_hi))
    o_ref[...] = acc + b_ref[...]


def _modulation(c, ada_w, ada_b):
    depth, _, d, d3 = ada_w.shape
    bsz = c.shape[0]
    nsub = depth * 2
    w = ada_w.reshape(nsub, d, d3)
    b = ada_b.reshape(nsub, 1, d3)
    out = pl.pallas_call(
        _mod_kernel,
        grid=(nsub, 3),
        in_specs=[
            pl.BlockSpec((bsz, d), lambda l, j: (0, 0)),
            pl.BlockSpec((None, d, d), lambda l, j: (l, 0, j)),
            pl.BlockSpec((None, 1, d), lambda l, j: (l, 0, j)),
        ],
        out_specs=pl.BlockSpec((None, None, bsz, d), lambda l, j: (l, j, 0, 0)),
        out_shape=jax.ShapeDtypeStruct((nsub, 3, bsz, d), F32),
        compiler_params=_params("arbitrary", "arbitrary"),
        name="adaln_mod",
    )(c, w, b)
    return out.reshape(nsub, 3, bsz, 1, d)


def _rope_kernel(pos_ref, inv_ref, sign_ref, cos_ref, sin_ref):
    ang = pos_ref[...].astype(F32) * inv_ref[...]
    cos_ref[...] = jnp.cos(ang)
    sin_ref[...] = jnp.sin(ang) * sign_ref[...]


def _rope_tables(positions):
    t = positions.size
    rot = B_HEAD_DIM // ROT_FRAC
    half = rot // 2
    inv = jnp.power(ROPE_THETA, -jnp.arange(half, dtype=F32) * 2.0 / rot)
    lane = jnp.arange(LANES) % B_HEAD_DIM
    inv_l = jnp.where(lane < rot, inv[lane % half], 0.0).reshape(1, LANES)
    sign_l = jnp.where(lane < half, -1.0, jnp.where(lane < rot, 1.0, 0.0)).astype(F32).reshape(1, LANES)
    tm = min(TOKEN_TILE, t)
    return pl.pallas_call(
        _rope_kernel,
        grid=(t // tm,),
        in_specs=[pl.BlockSpec((tm, 1), lambda i: (i, 0)), _const_spec((1, LANES)), _const_spec((1, LANES))],
        out_specs=[pl.BlockSpec((tm, LANES), lambda i: (i, 0))] * 2,
        out_shape=[jax.ShapeDtypeStruct((t, LANES), F32)] * 2,
        compiler_params=_params("arbitrary"),
        name="rope_tables",
    )(positions.reshape(t, 1), inv_l, sign_l)


def _rope_slab(slab, cos, sin_signed, lo_mask):
    fwd = pltpu.roll(slab, LANES - 8, axis=1)
    bwd = pltpu.roll(slab, 8, axis=1)
    return slab * cos + jnp.where(lo_mask, fwd, bwd) * sin_signed


def _rope_lo_mask(tm):
    lane = lax.broadcasted_iota(jnp.int32, (tm, LANES), 1) % B_HEAD_DIM
    return lane < (B_HEAD_DIM // ROT_FRAC // 2)


def _modulated(x_ref, sh_ref, sc_ref):
    return (x_ref[...] * (1.0 + sc_ref[...]) + sh_ref[...]).astype(BF16)


def _mlstm_in_kernel(x_ref, sh_ref, sc_ref, w_ref, bg_ref, q_ref, k_ref, v_ref, o_ref, g_ref, *, qk, vw):
    h = _modulated(x_ref, sh_ref, sc_ref)
    dk = qk // A_HEADS
    q_ref[...] = _dot(h, w_ref[:, :qk]).astype(BF16)
    k_ref[...] = (_dot(h, w_ref[:, qk:2 * qk]) * (dk ** -0.5)).astype(BF16)
    v_ref[...] = _dot(h, w_ref[:, 2 * qk:2 * qk + vw]).astype(BF16)
    o_ref[...] = _sigmoid(_dot(h, w_ref[:, 2 * qk + vw:2 * qk + 2 * vw]))
    g_ref[...] = _dot(h, w_ref[:, 2 * qk + 2 * vw:]) + bg_ref[...]


def _attn_in_body(x_ref, sh_ref, sc_ref, w_ref, wvt_ref, cos_ref, sin_ref, q_ref, k_ref, v_ref, *, qw, kw, scale):
    h = _modulated(x_ref, sh_ref, sc_ref)
    cos = cos_ref[...]
    sin = sin_ref[...]
    lo = _rope_lo_mask(h.shape[0])
    for j in range(qw // LANES):
        slab = _dot(h, w_ref[:, j * LANES:(j + 1) * LANES])
        q_ref[:, j * LANES:(j + 1) * LANES] = (_rope_slab(slab, cos, sin, lo) * scale).astype(BF16)
    for j in range(kw // LANES):
        slab = _dot(h, w_ref[:, qw + j * LANES:qw + (j + 1) * LANES])
        k_ref[:, j * LANES:(j + 1) * LANES] = _rope_slab(slab, cos, sin, lo).astype(BF16)
    if wvt_ref is None:
        v_ref[...] = _dot(h, w_ref[:, qw + kw:]).astype(BF16)
    else:
        v_ref[...] = _dot_nt(wvt_ref[...], h).astype(BF16)


def _attn_in_kernel(x_ref, sh_ref, sc_ref, w_ref, cos_ref, sin_ref, q_ref, k_ref, v_ref, **kw):
    _attn_in_body(x_ref, sh_ref, sc_ref, w_ref, None, cos_ref, sin_ref, q_ref, k_ref, v_ref, **kw)


def _attn_in_vt_kernel(x_ref, sh_ref, sc_ref, w_ref, wvt_ref, cos_ref, sin_ref, q_ref, k_ref, v_ref, **kw):
    _attn_in_body(x_ref, sh_ref, sc_ref, w_ref, wvt_ref, cos_ref, sin_ref, q_ref, k_ref, v_ref, **kw)


def _hgrn_in_kernel(x_ref, sh_ref, sc_ref, w_ref, q_ref, f_ref, v_ref, g_ref, *, kw, vw):
    h = _modulated(x_ref, sh_ref, sc_ref)
    q_ref[...] = _dot(h, w_ref[:, :kw])
    f_ref[...] = _dot(h, w_ref[:, kw:2 * kw])
    v_ref[...] = _dot(h, w_ref[:, 2 * kw:2 * kw + vw]).astype(BF16)
    g = _dot(h, w_ref[:, 2 * kw + vw:])
    g_ref[...] = g * _sigmoid(g)


def _in_proj(body, x, mod, sub, seq, w, extra, extra_specs, outs):
    t, d = x.shape
    tm = min(TOKEN_TILE, seq)
    tpb = seq // tm
    row = lambda width: pl.BlockSpec((tm, width), lambda i: (i, 0))
    col = lambda width: pl.BlockSpec((width, tm), lambda i: (0, i))
    return pl.pallas_call(
        body,
        grid=(t // tm,),
        in_specs=[row(d), _mod_spec(sub, 0, tpb, d), _mod_spec(sub, 1, tpb, d), _const_spec(w.shape)] + extra_specs,
        out_specs=[col(wd) if tr else row(wd) for wd, _, tr in outs],
        out_shape=[jax.ShapeDtypeStruct((wd, t) if tr else (t, wd), dt) for wd, dt, tr in outs],
        compiler_params=_params("arbitrary"),
        name=body.func.__name__.strip("_") if isinstance(body, functools.partial) else body.__name__.strip("_"),
    )(x, mod, mod, w, *extra)


def _out_kernel(y_ref, w_ref, x_ref, gate_ref, g_ref, b_ref, o_ref, *, alpha):
    y = _dot(y_ref[...], w_ref[...])
    z = alpha * x_ref[...] + (1.0 + gate_ref[...]) * y
    o_ref[...] = _layer_norm(z, g_ref[...], b_ref[...])


def _out_proj(y, w, x, mod, sub, seq, ln_g, ln_b, alpha):
    t, d = x.shape
    tm = min(TOKEN_TILE, seq)
    tpb = seq // tm
    row = lambda width: pl.BlockSpec((tm, width), lambda i: (i, 0))
    return pl.pallas_call(
        functools.partial(_out_kernel, alpha=alpha),
        grid=(t // tm,),
        in_specs=[row(y.shape[1]), _const_spec(w.shape), row(d), _mod_spec(sub, 2, tpb, d),
                  _const_spec((1, d)), _const_spec((1, d))],
        out_specs=row(d),
        out_shape=jax.ShapeDtypeStruct((t, d), F32),
        compiler_params=_params("arbitrary"),
        name="out_proj_ln",
    )(y, w, x, mod, ln_g.reshape(1, d), ln_b.reshape(1, d))


def _ffn_kernel(x_ref, sh_ref, sc_ref, gate_ref, wi_ref, wo_ref, g_ref, b_ref, o_ref, *, hidden, chunk, alpha):
    x = x_ref[...]
    h = (x * (1.0 + sc_ref[...]) + sh_ref[...]).astype(BF16)
    acc = jnp.zeros(x.shape, F32)
    for j in range(hidden // chunk):
        a = _dot(h, wi_ref[:, j * chunk:(j + 1) * chunk])
        u = _dot(h, wi_ref[:, hidden + j * chunk:hidden + (j + 1) * chunk])
        act = (a * _sigmoid(a) * u).astype(BF16)
        acc = acc + _dot(act, wo_ref[j * chunk:(j + 1) * chunk, :])
    z = alpha * x + (1.0 + gate_ref[...]) * acc
    o_ref[...] = _layer_norm(z, g_ref[...], b_ref[...])


def _ffn_chunk(hidden):
    best = LANES
    for c in range(LANES, hidden + 1, LANES):
        if hidden % c == 0 and c <= 1536:
            best = c
    return best


def _ffn(x, mod, sub, seq, w_in, w_out, ln_g, ln_b, alpha):
    t, d = x.shape
    hidden = w_out.shape[0]
    tm = min(TOKEN_TILE, seq)
    tpb = seq // tm
    row = pl.BlockSpec((tm, d), lambda i: (i, 0))
    return pl.pallas_call(
        functools.partial(_ffn_kernel, hidden=hidden, chunk=_ffn_chunk(hidden), alpha=alpha),
        grid=(t // tm,),
        in_specs=[row, _mod_spec(sub, 0, tpb, d), _mod_spec(sub, 1, tpb, d), _mod_spec(sub, 2, tpb, d),
                  _const_spec(w_in.shape), _const_spec(w_out.shape), _const_spec((1, d)), _const_spec((1, d))],
        out_specs=row,
        out_shape=jax.ShapeDtypeStruct((t, d), F32),
        compiler_params=_params("arbitrary"),
        name="ffn_swiglu_ln",
    )(x, mod, mod, mod, w_in, w_out, ln_g.reshape(1, d), ln_b.reshape(1, d))


def _mlstm_kernel(q_ref, k_ref, v_ref, o_ref, g_ref, nw_ref, y_ref, c_ref, n_ref, m_ref, *, chunk, dk, dv):
    L = chunk
    nchunks = q_ref.shape[0] // L

    @pl.when(pl.program_id(1) == 0)
    def _():
        c_ref[...] = jnp.zeros_like(c_ref)
        n_ref[...] = jnp.zeros_like(n_ref)
        m_ref[...] = jnp.zeros_like(m_ref)

    row_i = lax.broadcasted_iota(jnp.int32, (L, L), 0)
    col_i = lax.broadcasted_iota(jnp.int32, (L, L), 1)
    causal = col_i <= row_i
    tril = jnp.where(causal, 1.0, 0.0).astype(BF16)
    lane = lax.broadcasted_iota(jnp.int32, (L, LANES), 1)

    def body(c, carry):
        rows = pl.ds(pl.multiple_of(c * L, L), L)
        g = g_ref[rows, :]
        logf = jnp.where(lane >= A_HEADS, _log_sigmoid(g), 0.0)
        bcum = _dot_exact_lhs01(tril, logf)
        z = jnp.where(lane < A_HEADS, g, bcum)
        zt = z.T
        for h in range(A_HEADS):
            i_col = z[:, h:h + 1]
            b_col = z[:, A_HEADS + h:A_HEADS + h + 1]
            i_row = zt[h:h + 1, :]
            b_row = zt[A_HEADS + h:A_HEADS + h + 1, :]
            m_prev = m_ref[h, 0:1, 0:1]
            dmat = jnp.where(causal, b_col - b_row + i_row, -jnp.inf)
            m_inter = b_col + m_prev
            m_t = jnp.maximum(m_inter, jnp.max(dmat, axis=-1, keepdims=True))
            qh = q_ref[rows, h * dk:(h + 1) * dk]
            kh = k_ref[rows, h * dk:(h + 1) * dk]
            vh = v_ref[rows, h * dv:(h + 1) * dv]
            a = _dot_nt(qh, kh) * jnp.exp(dmat - m_t)
            w_inter = jnp.exp(m_inter - m_t)
            c_prev = c_ref[h]
            n_prev = n_ref[h, 0:1, :]
            num = _dot(a.astype(BF16), vh) + w_inter * _dot(qh, c_prev.astype(BF16))
            qn = jnp.sum(qh.astype(F32) * n_prev, axis=-1, keepdims=True)
            den = jnp.sum(a, axis=-1, keepdims=True) + w_inter * qn
            hout = num / jnp.maximum(jnp.abs(den), jnp.exp(-m_t))
            y = _rms_norm(hout, nw_ref[:, h * dv:(h + 1) * dv]) * o_ref[rows, h * dv:(h + 1) * dv]
            y_ref[rows, h * dv:(h + 1) * dv] = y.astype(y_ref.dtype)
            b_last = b_col[L - 1:L, :]
            g_col = b_last - b_col + i_col
            m_new = jnp.maximum(b_last + m_prev, jnp.max(g_col, axis=0, keepdims=True))
            decay = jnp.exp(b_last + m_prev - m_new)
            kws = kh.astype(F32) * jnp.exp(g_col - m_new)
            c_ref[h] = decay * c_prev + _dot(kws.T.astype(BF16), vh)
            n_ref[h, 0:1, :] = decay * n_prev + jnp.sum(kws, axis=0, keepdims=True)
            m_ref[h] = jnp.broadcast_to(m_new, m_ref.shape[1:])
        return carry

    lax.fori_loop(0, nchunks, body, 0)


def _mlstm(q, k, v, o, g, norm_w, bsz, seq):
    t = q.shape[0]
    qk, vw = q.shape[1], v.shape[1]
    dk, dv = qk // A_HEADS, vw // A_HEADS
    ts = min(MLSTM_STEP, seq)
    spb = seq // ts
    row = lambda width: pl.BlockSpec((ts, width), lambda b, s: (b * spb + s, 0))
    return pl.pallas_call(
        functools.partial(_mlstm_kernel, chunk=min(MLSTM_CHUNK, ts), dk=dk, dv=dv),
        grid=(bsz, spb),
        in_specs=[row(qk), row(qk), row(vw), row(vw), row(LANES), _const_spec((1, vw))],
        out_specs=row(vw),
        out_shape=jax.ShapeDtypeStruct((t, vw), BF16),
        scratch_shapes=[pltpu.VMEM((A_HEADS, dk, dv), F32), pltpu.VMEM((A_HEADS, 8, dk), F32),
                        pltpu.VMEM((A_HEADS, 8, LANES), F32)],
        compiler_params=_params("arbitrary", "arbitrary"),
        name="mlstm_mixer",
    )(q, k, v, o, g, norm_w.reshape(1, vw))


def _swa_kernel(sink_ref, q_ref, kc_ref, kp_ref, vc_ref, vp_ref, o_ref, *, heads, hd, group):
    n = pl.program_id(1)
    blk = q_ref.shape[0]
    kb = jnp.concatenate([kp_ref[...], kc_ref[...]], axis=0)
    vb = jnp.concatenate([vp_ref[...], vc_ref[...]], axis=0)
    q_i = lax.broadcasted_iota(jnp.int32, (blk, 2 * blk), 0)
    k_i = lax.broadcasted_iota(jnp.int32, (blk, 2 * blk), 1)
    valid = (k_i > q_i) & (k_i <= q_i + blk) & ((k_i >= blk) | (n > 0))
    outs = []
    for j in range(heads):
        gi = j // group
        s = _dot_nt(q_ref[:, j * hd:(j + 1) * hd], kb[:, gi * hd:(gi + 1) * hd])
        s = jnp.where(valid, s, -jnp.inf)
        sink = sink_ref[j]
        mx = jnp.maximum(jnp.max(s, axis=-1, keepdims=True), sink)
        p = jnp.exp(s - mx)
        denom = jnp.sum(p, axis=-1, keepdims=True) + jnp.exp(sink - mx)
        outs.append(_dot(p.astype(BF16), vb[:, gi * hd:(gi + 1) * hd]) / denom)
    o_ref[...] = jnp.concatenate(outs, axis=-1).astype(o_ref.dtype)


def _swa(q, k, v, sinks, bsz, seq):
    t, qw = q.shape
    kw = k.shape[1]
    heads = qw // B_HEAD_DIM
    blk = min(B_BLOCK, seq)
    nb = seq // blk
    cur = lambda width: pl.BlockSpec((blk, width), lambda b, n: (b * nb + n, 0))
    prev = lambda width: pl.BlockSpec((blk, width), lambda b, n: (b * nb + jnp.maximum(n - 1, 0), 0))
    return pl.pallas_call(
        functools.partial(_swa_kernel, heads=heads, hd=B_HEAD_DIM, group=B_GROUP),
        grid=(bsz, nb),
        in_specs=[pl.BlockSpec(memory_space=pltpu.SMEM), cur(qw), cur(kw), prev(kw), cur(kw), prev(kw)],
        out_specs=cur(qw),
        out_shape=jax.ShapeDtypeStruct((t, qw), BF16),
        compiler_params=_params("arbitrary", "arbitrary"),
        name="swa_mixer",
    )(sinks.astype(F32), q, k, k, v, v)


def _hgrn_kernel(q_ref, f_ref, v_ref, g_ref, lbp_ref, nw_ref, y_ref, st_ref, cf_ref, ck_ref,
                 cfc_ref, ckc_ref, vsc_ref, *,
                 chunk, dk, dv, layer):
    L = chunk
    nchunks = q_ref.shape[0] // L
    heads = q_ref.shape[1] // dk

    @pl.when(pl.program_id(1) == 0)
    def _():
        st_ref[...] = jnp.zeros_like(st_ref)

    lbp = lbp_ref[...]
    e = jnp.exp(lbp - jnp.max(lbp, axis=0, keepdims=True))
    sm = e / jnp.sum(e, axis=0, keepdims=True)
    lb = jnp.zeros_like(sm[0:1])
    for r in range(1, layer + 1):
        lb = lb + sm[r:r + 1]
    log_lb = jnp.log(lb)
    log_1m_lb = jnp.log1p(-lb)

    ones = jnp.ones((dk, LANES), BF16)
    trow = lax.broadcasted_iota(jnp.int32, (L, dv), 0)

    rb = min(HGRN_CUMSUM_ROWS, q_ref.shape[0])
    row_i = lax.broadcasted_iota(jnp.int32, (rb, rb), 0)
    col_i = lax.broadcasted_iota(jnp.int32, (rb, rb), 1)
    tril = jnp.where((col_i <= row_i) & (col_i // L == row_i // L), 1.0, 0.0).astype(BF16)
    for r0 in range(0, q_ref.shape[0], rb):
        fp = f_ref[r0:r0 + rb, :]
        ls = jnp.minimum(fp, 0.0) - jnp.log(1.0 + jnp.exp(-jnp.abs(fp)))
        x2 = log_1m_lb + ls
        logf = jnp.maximum(log_lb, x2) + jnp.log(1.0 + jnp.exp(-jnp.abs(log_lb - x2)))
        cf = _dot_exact_lhs01(tril, logf) * LOG2_E
        cf_ref[r0:r0 + rb, :] = cf
        ck_ref[r0:r0 + rb, :] = cf - (x2 - fp) * LOG2_E

    unroll = cfc_ref.shape[0]

    def body(c, carry):
        rows = [pl.ds(pl.multiple_of((c * unroll + u) * L, L), L) for u in range(unroll)]
        for u in range(unroll):
            cfc_ref[u] = cf_ref[rows[u], :]
            ckc_ref[u] = ck_ref[rows[u], :]
            vsc_ref[u] = v_ref[rows[u], :].astype(F32)
        for h in range(heads):
            cs = slice(h * dk, (h + 1) * dk)
            vs = slice(h * dv, (h + 1) * dv)
            for u in range(unroll):
                cfh = cfc_ref[u, :, cs]
                qh = q_ref[rows[u], cs]
                ws = []
                for s in range(L):
                    ws.append(qh * jnp.exp2(cfh - ckc_ref[u, s:s + 1, cs]))
                w_all = jnp.concatenate(ws, axis=0).astype(BF16)
                r_all = _dot(w_all, ones)
                st = st_ref[h]
                o = _dot_nt((qh * jnp.exp2(cfh)).astype(BF16), st.astype(BF16))
                for s in range(L):
                    a_s = jnp.where(trow >= s, r_all[s * L:(s + 1) * L, :], 0.0)
                    o = o + a_s * vsc_ref[u, s:s + 1, vs]
                last = cfc_ref[u, L - 1:L, cs]
                kdec = jnp.exp2(last - ckc_ref[u, :, cs]).astype(BF16)
                st_ref[h] = jnp.exp2(last) * st + _dot(vsc_ref[u, :, vs].T.astype(BF16), kdec)
                y = _rms_norm(o, nw_ref[:, vs]) * g_ref[rows[u], vs]
                y_ref[rows[u], vs] = y.astype(y_ref.dtype)
        return carry

    lax.fori_loop(0, nchunks // unroll, body, 0)


def _hgrn(q, f, v, g, lower_bounds, layer, norm_w, bsz, seq):
    t, kw = q.shape
    vw = v.shape[1]
    dk, dv = kw // C_HEADS, vw // C_HEADS
    ts = min(HGRN_STEP, seq)
    spb = seq // ts
    chunk = min(HGRN_CHUNK, ts)
    unroll = math.gcd(HGRN_UNROLL, ts // chunk)
    row = lambda width: pl.BlockSpec((ts, width), lambda b, s: (b * spb + s, 0))
    return pl.pallas_call(
        functools.partial(_hgrn_kernel, chunk=chunk, dk=dk, dv=dv, layer=layer),
        grid=(bsz, spb),
        in_specs=[row(kw), row(kw), row(vw), row(vw), _const_spec(lower_bounds.shape), _const_spec((1, vw))],
        out_specs=row(vw),
        out_shape=jax.ShapeDtypeStruct((t, vw), BF16),
        scratch_shapes=[pltpu.VMEM((C_HEADS, dv, dk), F32), pltpu.VMEM((ts, kw), F32), pltpu.VMEM((ts, kw), F32),
                        pltpu.VMEM((unroll, chunk, kw), F32), pltpu.VMEM((unroll, chunk, kw), F32),
                        pltpu.VMEM((unroll, chunk, vw), F32)],
        compiler_params=_params("arbitrary", "arbitrary"),
        name="hgrn2_mixer",
    )(q, f, v, g, lower_bounds, norm_w.reshape(1, vw))


def _diff_kernel(q_ref, k_ref, vt_ref, lam_ref, nw_ref, o_ref, acc_ref, *, blk, hd, lam_init):
    qi = pl.program_id(2)
    lv = lam_ref[...]
    lam = (jnp.exp(jnp.sum(lv[0:1] * lv[1:2], axis=-1, keepdims=True))
           - jnp.exp(jnp.sum(lv[2:3] * lv[3:4], axis=-1, keepdims=True)) + lam_init)
    q = q_ref[...]
    lane = lax.broadcasted_iota(jnp.int32, q.shape, 1)
    zero = jnp.zeros_like(q)
    qz = (jnp.where(lane < hd, q, zero), jnp.where(lane >= hd, q, zero))
    acc_ref[...] = jnp.zeros_like(acc_ref)

    hw = 2 * hd
    ones = jnp.ones((DIFF_ONES_ROWS, blk), BF16)

    def step(j, m, masked):
        start = pl.multiple_of(j * blk, blk)
        kb = k_ref[pl.ds(start, blk), :]
        vte = jnp.concatenate([vt_ref[:, pl.ds(start, blk)], ones], axis=0)
        if masked:
            key_i = lax.broadcasted_iota(jnp.int32, (blk, blk), 0)
            qry_i = lax.broadcasted_iota(jnp.int32, (blk, blk), 1)
            keep = key_i <= qry_i
        sts = [_dot_nt(kb, qz[c]) for c in range(2)]
        m_out = []
        for c in range(2):
            st = sts[c]
            if masked:
                st = jnp.where(keep, st, -jnp.inf)
            m_new = jnp.maximum(m[c], jnp.max(st, axis=0, keepdims=True))
            alpha = jnp.exp2(m[c] - m_new)
            p = jnp.exp2(st - m_new).astype(BF16)
            acc_ref[c] = alpha * acc_ref[c] + _dot(vte, p)
            m_out.append(m_new)
        return tuple(m_out)

    neg = jnp.full((1, blk), -jnp.inf, F32)
    m = lax.fori_loop(0, qi, lambda j, cr: step(j, cr, False), (neg, neg))
    step(qi, m, True)
    norm = [acc_ref[c, :hw, :] * (1.0 / acc_ref[c, hw:hw + 1, :]) for c in range(2)]
    out_t = norm[0] - lam * norm[1]
    o_ref[...] = (_rms_norm(out_t.T, nw_ref[...]) * (1.0 - lam_init)).astype(o_ref.dtype)


def _diff_attn(q, k, vt, lam_vec, norm_w, lam_init, bsz, seq):
    t, w = q.shape
    hw = 2 * D_HEAD_DIM
    heads = w // hw
    blk = min(DIFF_BLOCK, seq)
    nq = seq // blk
    return pl.pallas_call(
        functools.partial(_diff_kernel, blk=blk, hd=D_HEAD_DIM, lam_init=lam_init),
        grid=(bsz, heads, nq),
        in_specs=[pl.BlockSpec((blk, hw), lambda b, h, i: (b * nq + i, h)),
                  pl.BlockSpec((seq, hw), lambda b, h, i: (b, h)),
                  pl.BlockSpec((hw, seq), lambda b, h, i: (h, b)),
                  _const_spec(lam_vec.shape), _const_spec((1, hw))],
        out_specs=pl.BlockSpec((blk, hw), lambda b, h, i: (b * nq + i, h)),
        out_shape=jax.ShapeDtypeStruct((t, w), BF16),
        scratch_shapes=[pltpu.VMEM((2, hw + DIFF_ONES_ROWS, blk), F32)],
        compiler_params=_params("arbitrary", "arbitrary", "arbitrary"),
        name="diff_attn_mixer",
    )(q, k, vt, lam_vec, norm_w.reshape(1, hw))


def _pad_cols(w, width):
    return jnp.pad(w, ((0, 0), (0, width - w.shape[1])))


def kernel(x, c, positions, ada_w, ada_b, ln_g, ln_b, mlstm_w_in, mlstm_b_gate, mlstm_norm, mlstm_w_out, swa_w_in, swa_sinks, swa_w_out, hgrn_w_in, hgrn_lower_bounds, hgrn_norm, hgrn_w_out, diff_w_in, diff_lambda, diff_norm, diff_w_out, ffn_w_in, ffn_w_out):
    bsz, seq, d = x.shape
    depth = ada_w.shape[0]
    alpha = (2 * depth) ** 0.25
    t = bsz * seq
    xf = x.reshape(t, d)
    mod = _modulation(c, ada_w, ada_b)
    cos_t, sin_t = _rope_tables(positions)
    for i in range(depth):
        mixer, j = i % 4, i // 4
        sub = 2 * i
        if mixer == 0:
            vw = mlstm_w_out.shape[1]
            qk = vw // 2
            n_gate = 2 * A_HEADS
            w = _pad_cols(mlstm_w_in[j], 2 * qk + 2 * vw + LANES).astype(BF16)
            bg = _pad_cols(mlstm_b_gate[j].reshape(1, n_gate), LANES)
            q, k, v, o, g = _in_proj(
                functools.partial(_mlstm_in_kernel, qk=qk, vw=vw), xf, mod, sub, seq, w, [bg],
                [_const_spec((1, LANES))],
                [(qk, BF16, False), (qk, BF16, False), (vw, BF16, False), (vw, F32, False), (LANES, F32, False)])
            y = _mlstm(q, k, v, o, g, mlstm_norm[j], bsz, seq)
            w_out = mlstm_w_out[j]
        elif mixer == 1:
            qw = swa_w_out.shape[1]
            kw = qw // B_GROUP
            q, k, v = _in_proj(
                functools.partial(_attn_in_kernel, qw=qw, kw=kw, scale=B_HEAD_DIM ** -0.5), xf, mod, sub, seq,
                swa_w_in[j].astype(BF16), [cos_t, sin_t],
                [pl.BlockSpec((min(TOKEN_TILE, seq), LANES), lambda i: (i, 0))] * 2,
                [(qw, BF16, False), (kw, BF16, False), (kw, BF16, False)])
            y = _swa(q, k, v, swa_sinks[j], bsz, seq)
            w_out = swa_w_out[j]
        elif mixer == 2:
            vw = hgrn_w_out.shape[1]
            q, f, v, g = _in_proj(
                functools.partial(_hgrn_in_kernel, kw=vw, vw=vw), xf, mod, sub, seq,
                hgrn_w_in[j].astype(BF16), [], [],
                [(vw, F32, False), (vw, F32, False), (vw, BF16, False), (vw, F32, False)])
            y = _hgrn(q, f, v, g, hgrn_lower_bounds.astype(F32), i, hgrn_norm[j], bsz, seq)
            w_out = hgrn_w_out[j]
        else:
            w_ = diff_w_out.shape[1]
            lam_init = 0.8 - 0.6 * math.exp(-0.3 * i)
            w_qk = diff_w_in[j][:, :2 * w_].astype(BF16)
            w_vt = diff_w_in[j][:, 2 * w_:].T.astype(BF16)
            q, k, vt = _in_proj(
                functools.partial(_attn_in_vt_kernel, qw=w_, kw=w_, scale=D_HEAD_DIM ** -0.5 * math.log2(math.e)),
                xf, mod, sub, seq,
                w_qk, [w_vt, cos_t, sin_t],
                [_const_spec(w_vt.shape)] + [pl.BlockSpec((min(TOKEN_TILE, seq), LANES), lambda i: (i, 0))] * 2,
                [(w_, BF16, False), (w_, BF16, False), (w_, BF16, True)])
            y = _diff_attn(q, k, vt, diff_lambda[j].astype(F32), diff_norm[j], lam_init, bsz, seq)
            w_out = diff_w_out[j]
        xf = _out_proj(y, w_out.astype(BF16), xf, mod, sub, seq, ln_g[i, 0], ln_b[i, 0], alpha)
        xf = _ffn(xf, mod, sub + 1, seq, ffn_w_in[i].astype(BF16), ffn_w_out[i].astype(BF16),
                  ln_g[i, 1], ln_b[i, 1], alpha)
    return xf.reshape(bsz, seq, d)
```

```python
import functools
import math

import jax
import jax.numpy as jnp
from jax import lax
from jax.experimental import pallas as pl
from jax.experimental.pallas import tpu as pltpu

F32 = jnp.float32
BF16 = jnp.bfloat16

A_HEADS = 4
B_HEAD_DIM = 64
B_GROUP = 8
B_BLOCK = 128
C_HEADS = 8
D_HEAD_DIM = 64
ROPE_THETA = 500000.0
ROT_FRAC = 4
LN_EPS = 1e-5
RMS_EPS = 1e-6
LANES = 128
LOG2_E = math.log2(math.e)
VMEM_LIMIT = 52 * 2 ** 20

TOKEN_TILE = 512
MLSTM_CHUNK = 128
MLSTM_STEP = 512
HGRN_CHUNK = 16
HGRN_STEP = 512
HGRN_CUMSUM_ROWS = 128
HGRN_UNROLL = 1
DIFF_BLOCK = 512
DIFF_ONES_ROWS = 16


def _params(*sem):
    return pltpu.CompilerParams(dimension_semantics=sem, vmem_limit_bytes=VMEM_LIMIT)


def _dot(a, b):
    return jnp.dot(a, b, preferred_element_type=F32)


def _dot_nt(a, b):
    return lax.dot_general(a, b, (((1,), (1,)), ((), ())), preferred_element_type=F32)


def _split3(x):
    hi = x.astype(BF16)
    r = x - hi.astype(F32)
    mid = r.astype(BF16)
    lo = (r - mid.astype(F32)).astype(BF16)
    return hi, mid, lo


def _dot_exact_lhs01(mask_bf16, x):
    hi, mid, lo = _split3(x)
    return _dot(mask_bf16, hi) + (_dot(mask_bf16, mid) + _dot(mask_bf16, lo))


def _sigmoid(x):
    return 1.0 / (1.0 + jnp.exp(-x))


def _log_sigmoid(x):
    return jnp.minimum(x, 0.0) - jnp.log1p(jnp.exp(-jnp.abs(x)))


def _layer_norm(z, g, b):
    mu = jnp.mean(z, axis=-1, keepdims=True)
    zc = z - mu
    var = jnp.mean(zc * zc, axis=-1, keepdims=True)
    return zc * lax.rsqrt(var + LN_EPS) * g + b


def _rms_norm(h, w):
    return h * lax.rsqrt(jnp.mean(h * h, axis=-1, keepdims=True) + RMS_EPS) * w


def _const_spec(shape):
    nd = len(shape)
    return pl.BlockSpec(shape, lambda *_: (0,) * nd)


def _mod_spec(sub, part, tiles_per_batch, d):
    return pl.BlockSpec((None, None, None, 1, d), lambda i, *_: (sub, part, i // tiles_per_batch, 0, 0))


def _mod_kernel(c_ref, w_ref, b_ref, o_ref):
    c = c_ref[...]
    cond = c * _sigmoid(c)
    w = w_ref[...]
    a_hi = cond.astype(BF16)
    a_lo = (cond - a_hi.astype(F32)).astype(BF16)
    w_hi = w.astype(BF16)
    w_lo = (w - w_hi.astype(F32)).astype(BF16)
    acc = _dot(a_hi, w_hi) + (_dot(a_hi, w_lo) + _dot(a_lo, w_hi))
    o_ref[...] = acc + b_ref[...]


def _modulation(c, ada_w, ada_b):
    depth, _, d, d3 = ada_w.shape
    bsz = c.shape[0]
    nsub = depth * 2
    w = ada_w.reshape(nsub, d, d3)
    b = ada_b.reshape(nsub, 1, d3)
    out = pl.pallas_call(
        _mod_kernel,
        grid=(nsub, 3),
        in_specs=[
            pl.BlockSpec((bsz, d), lambda l, j: (0, 0)),
            pl.BlockSpec((None, d, d), lambda l, j: (l, 0, j)),
            pl.BlockSpec((None, 1, d), lambda l, j: (l, 0, j)),
        ],
        out_specs=pl.BlockSpec((None, None, bsz, d), lambda l, j: (l, j, 0, 0)),
        out_shape=jax.ShapeDtypeStruct((nsub, 3, bsz, d), F32),
        compiler_params=_params("arbitrary", "arbitrary"),
        name="adaln_mod",
    )(c, w, b)
    return out.reshape(nsub, 3, bsz, 1, d)


def _rope_kernel(pos_ref, inv_ref, sign_ref, cos_ref, sin_ref):
    ang = pos_ref[...].astype(F32) * inv_ref[...]
    cos_ref[...] = jnp.cos(ang)
    sin_ref[...] = jnp.sin(ang) * sign_ref[...]


def _rope_tables(positions):
    t = positions.size
    rot = B_HEAD_DIM // ROT_FRAC
    half = rot // 2
    inv = jnp.power(ROPE_THETA, -jnp.arange(half, dtype=F32) * 2.0 / rot)
    lane = jnp.arange(LANES) % B_HEAD_DIM
    inv_l = jnp.where(lane < rot, inv[lane % half], 0.0).reshape(1, LANES)
    sign_l = jnp.where(lane < half, -1.0, jnp.where(lane < rot, 1.0, 0.0)).astype(F32).reshape(1, LANES)
    tm = min(TOKEN_TILE, t)
    return pl.pallas_call(
        _rope_kernel,
        grid=(t // tm,),
        in_specs=[pl.BlockSpec((tm, 1), lambda i: (i, 0)), _const_spec((1, LANES)), _const_spec((1, LANES))],
        out_specs=[pl.BlockSpec((tm, LANES), lambda i: (i, 0))] * 2,
        out_shape=[jax.ShapeDtypeStruct((t, LANES), F32)] * 2,
        compiler_params=_params("arbitrary"),
        name="rope_tables",
    )(positions.reshape(t, 1), inv_l, sign_l)


def _rope_slab(slab, cos, sin_signed, lo_mask):
    fwd = pltpu.roll(slab, LANES - 8, axis=1)
    bwd = pltpu.roll(slab, 8, axis=1)
    return slab * cos + jnp.where(lo_mask, fwd, bwd) * sin_signed


def _rope_lo_mask(tm):
    lane = lax.broadcasted_iota(jnp.int32, (tm, LANES), 1) % B_HEAD_DIM
    return lane < (B_HEAD_DIM // ROT_FRAC // 2)


def _modulated(x_ref, sh_ref, sc_ref):
    return (x_ref[...] * (1.0 + sc_ref[...]) + sh_ref[...]).astype(BF16)


def _mlstm_in_kernel(x_ref, sh_ref, sc_ref, w_ref, bg_ref, q_ref, k_ref, v_ref, o_ref, g_ref, *, qk, vw):
    h = _modulated(x_ref, sh_ref, sc_ref)
    dk = qk // A_HEADS
    q_ref[...] = _dot(h, w_ref[:, :qk]).astype(BF16)
    k_ref[...] = (_dot(h, w_ref[:, qk:2 * qk]) * (dk ** -0.5)).astype(BF16)
    v_ref[...] = _dot(h, w_ref[:, 2 * qk:2 * qk + vw]).astype(BF16)
    o_ref[...] = _sigmoid(_dot(h, w_ref[:, 2 * qk + vw:2 * qk + 2 * vw]))
    g_ref[...] = _dot(h, w_ref[:, 2 * qk + 2 * vw:]) + bg_ref[...]


def _attn_in_body(x_ref, sh_ref, sc_ref, w_ref, wvt_ref, cos_ref, sin_ref, q_ref, k_ref, v_ref, *, qw, kw, scale):
    h = _modulated(x_ref, sh_ref, sc_ref)
    cos = cos_ref[...]
    sin = sin_ref[...]
    lo = _rope_lo_mask(h.shape[0])
    for j in range(qw // LANES):
        slab = _dot(h, w_ref[:, j * LANES:(j + 1) * LANES])
        q_ref[:, j * LANES:(j + 1) * LANES] = (_rope_slab(slab, cos, sin, lo) * scale).astype(BF16)
    for j in range(kw // LANES):
        slab = _dot(h, w_ref[:, qw + j * LANES:qw + (j + 1) * LANES])
        k_ref[:, j * LANES:(j + 1) * LANES] = _rope_slab(slab, cos, sin, lo).astype(BF16)
    if wvt_ref is None:
        v_ref[...] = _dot(h, w_ref[:, qw + kw:]).astype(BF16)
    else:
        v_ref[...] = _dot_nt(wvt_ref[...], h).astype(BF16)


def _attn_in_kernel(x_ref, sh_ref, sc_ref, w_ref, cos_ref, sin_ref, q_ref, k_ref, v_ref, **kw):
    _attn_in_body(x_ref, sh_ref, sc_ref, w_ref, None, cos_ref, sin_ref, q_ref, k_ref, v_ref, **kw)


def _attn_in_vt_kernel(x_ref, sh_ref, sc_ref, w_ref, wvt_ref, cos_ref, sin_ref, q_ref, k_ref, v_ref, **kw):
    _attn_in_body(x_ref, sh_ref, sc_ref, w_ref, wvt_ref, cos_ref, sin_ref, q_ref, k_ref, v_ref, **kw)


def _hgrn_in_kernel(x_ref, sh_ref, sc_ref, w_ref, q_ref, f_ref, v_ref, g_ref, *, kw, vw):
    h = _modulated(x_ref, sh_ref, sc_ref)
    q_ref[...] = _dot(h, w_ref[:, :kw])
    f_ref[...] = _dot(h, w_ref[:, kw:2 * kw])
    v_ref[...] = _dot(h, w_ref[:, 2 * kw:2 * kw + vw]).astype(BF16)
    g = _dot(h, w_ref[:, 2 * kw + vw:])
    g_ref[...] = g * _sigmoid(g)


def _in_proj(body, x, mod, sub, seq, w, extra, extra_specs, outs):
    t, d = x.shape
    tm = min(TOKEN_TILE, seq)
    tpb = seq // tm
    row = lambda width: pl.BlockSpec((tm, width), lambda i: (i, 0))
    col = lambda width: pl.BlockSpec((width, tm), lambda i: (0, i))
    return pl.pallas_call(
        body,
        grid=(t // tm,),
        in_specs=[row(d), _mod_spec(sub, 0, tpb, d), _mod_spec(sub, 1, tpb, d), _const_spec(w.shape)] + extra_specs,
        out_specs=[col(wd) if tr else row(wd) for wd, _, tr in outs],
        out_shape=[jax.ShapeDtypeStruct((wd, t) if tr else (t, wd), dt) for wd, dt, tr in outs],
        compiler_params=_params("arbitrary"),
        name=body.func.__name__.strip("_") if isinstance(body, functools.partial) else body.__name__.strip("_"),
    )(x, mod, mod, w, *extra)


def _tail_kernel(y_ref, wp_ref, x_ref, gate0_ref, g0_ref, b0_ref, sh_ref, sc_ref, gate1_ref, wi_ref, wo_ref,
                 g1_ref, b1_ref, o_ref, *, hidden, chunk, alpha):
    y = _dot(y_ref[...], wp_ref[...])
    x = _layer_norm(alpha * x_ref[...] + (1.0 + gate0_ref[...]) * y, g0_ref[...], b0_ref[...])
    h = (x * (1.0 + sc_ref[...]) + sh_ref[...]).astype(BF16)
    acc = jnp.zeros(x.shape, F32)
    for j in range(hidden // chunk):
        a = _dot(h, wi_ref[:, j * chunk:(j + 1) * chunk])
        u = _dot(h, wi_ref[:, hidden + j * chunk:hidden + (j + 1) * chunk])
        act = (a * _sigmoid(a) * u).astype(BF16)
        acc = acc + _dot(act, wo_ref[j * chunk:(j + 1) * chunk, :])
    z = alpha * x + (1.0 + gate1_ref[...]) * acc
    o_ref[...] = _layer_norm(z, g1_ref[...], b1_ref[...])


def _ffn_chunk(hidden):
    best = LANES
    for c in range(LANES, hidden + 1, LANES):
        if hidden % c == 0 and c <= 1536:
            best = c
    return best


def _resident_spec(shape):
    nd = len(shape)
    return pl.BlockSpec(shape, lambda *_: (0,) * nd, pipeline_mode=pl.Buffered(1))


def _layer_tail(y, w_proj, x, mod, sub, seq, w_in, w_out, ln_g, ln_b, alpha):
    t, d = x.shape
    hidden = w_out.shape[0]
    tm = min(TOKEN_TILE, seq)
    tpb = seq // tm
    row = lambda width: pl.BlockSpec((tm, width), lambda i: (i, 0))
    vec = _const_spec((1, d))
    return pl.pallas_call(
        functools.partial(_tail_kernel, hidden=hidden, chunk=_ffn_chunk(hidden), alpha=alpha),
        grid=(t // tm,),
        in_specs=[row(y.shape[1]), _resident_spec(w_proj.shape), row(d), _mod_spec(sub, 2, tpb, d), vec, vec,
                  _mod_spec(sub + 1, 0, tpb, d), _mod_spec(sub + 1, 1, tpb, d), _mod_spec(sub + 1, 2, tpb, d),
                  _resident_spec(w_in.shape), _resident_spec(w_out.shape), vec, vec],
        out_specs=row(d),
        out_shape=jax.ShapeDtypeStruct((t, d), F32),
        compiler_params=_params("arbitrary"),
        name="out_proj_ffn",
    )(y, w_proj, x, mod, ln_g[0].reshape(1, d), ln_b[0].reshape(1, d), mod, mod, mod, w_in, w_out,
      ln_g[1].reshape(1, d), ln_b[1].reshape(1, d))


def _mlstm_kernel(q_ref, k_ref, v_ref, o_ref, g_ref, nw_ref, y_ref, c_ref, n_ref, m_ref, *, chunk, dk, dv):
    L = chunk
    nchunks = q_ref.shape[0] // L

    @pl.when(pl.program_id(1) == 0)
    def _():
        c_ref[...] = jnp.zeros_like(c_ref)
        n_ref[...] = jnp.zeros_like(n_ref)
        m_ref[...] = jnp.zeros_like(m_ref)

    row_i = lax.broadcasted_iota(jnp.int32, (L, L), 0)
    col_i = lax.broadcasted_iota(jnp.int32, (L, L), 1)
    causal = col_i <= row_i
    tril = jnp.where(causal, 1.0, 0.0).astype(BF16)
    lane = lax.broadcasted_iota(jnp.int32, (L, LANES), 1)

    def body(c, carry):
        rows = pl.ds(pl.multiple_of(c * L, L), L)
        g = g_ref[rows, :]
        logf = jnp.where(lane >= A_HEADS, _log_sigmoid(g), 0.0)
        bcum = _dot_exact_lhs01(tril, logf)
        z = jnp.where(lane < A_HEADS, g, bcum)
        zt = z.T
        for h in range(A_HEADS):
            i_col = z[:, h:h + 1]
            b_col = z[:, A_HEADS + h:A_HEADS + h + 1]
            i_row = zt[h:h + 1, :]
            b_row = zt[A_HEADS + h:A_HEADS + h + 1, :]
            m_prev = m_ref[h, 0:1, 0:1]
            dmat = jnp.where(causal, b_col - b_row + i_row, -jnp.inf)
            m_inter = b_col + m_prev
            m_t = jnp.maximum(m_inter, jnp.max(dmat, axis=-1, keepdims=True))
            qh = q_ref[rows, h * dk:(h + 1) * dk]
            kh = k_ref[rows, h * dk:(h + 1) * dk]
            vh = v_ref[rows, h * dv:(h + 1) * dv]
            a = _dot_nt(qh, kh) * jnp.exp(dmat - m_t)
            w_inter = jnp.exp(m_inter - m_t)
            c_prev = c_ref[h]
            n_prev = n_ref[h, 0:1, :]
            num = _dot(a.astype(BF16), vh) + w_inter * _dot(qh, c_prev.astype(BF16))
            qn = jnp.sum(qh.astype(F32) * n_prev, axis=-1, keepdims=True)
            den = jnp.sum(a, axis=-1, keepdims=True) + w_inter * qn
            hout = num / jnp.maximum(jnp.abs(den), jnp.exp(-m_t))
            y = _rms_norm(hout, nw_ref[:, h * dv:(h + 1) * dv]) * o_ref[rows, h * dv:(h + 1) * dv]
            y_ref[rows, h * dv:(h + 1) * dv] = y.astype(y_ref.dtype)
            b_last = b_col[L - 1:L, :]
            g_col = b_last - b_col + i_col
            m_new = jnp.maximum(b_last + m_prev, jnp.max(g_col, axis=0, keepdims=True))
            decay = jnp.exp(b_last + m_prev - m_new)
            kws = kh.astype(F32) * jnp.exp(g_col - m_new)
            c_ref[h] = decay * c_prev + _dot(kws.T.astype(BF16), vh)
            n_ref[h, 0:1, :] = decay * n_prev + jnp.sum(kws, axis=0, keepdims=True)
            m_ref[h] = jnp.broadcast_to(m_new, m_ref.shape[1:])
        return carry

    lax.fori_loop(0, nchunks, body, 0)


def _mlstm(q, k, v, o, g, norm_w, bsz, seq):
    t = q.shape[0]
    qk, vw = q.shape[1], v.shape[1]
    dk, dv = qk // A_HEADS, vw // A_HEADS
    ts = min(MLSTM_STEP, seq)
    spb = seq // ts
    row = lambda width: pl.BlockSpec((ts, width), lambda b, s: (b * spb + s, 0))
    return pl.pallas_call(
        functools.partial(_mlstm_kernel, chunk=min(MLSTM_CHUNK, ts), dk=dk, dv=dv),
        grid=(bsz, spb),
        in_specs=[row(qk), row(qk), row(vw), row(vw), row(LANES), _const_spec((1, vw))],
        out_specs=row(vw),
        out_shape=jax.ShapeDtypeStruct((t, vw), BF16),
        scratch_shapes=[pltpu.VMEM((A_HEADS, dk, dv), F32), pltpu.VMEM((A_HEADS, 8, dk), F32),
                        pltpu.VMEM((A_HEADS, 8, LANES), F32)],
        compiler_params=_params("arbitrary", "arbitrary"),
        name="mlstm_mixer",
    )(q, k, v, o, g, norm_w.reshape(1, vw))


def _swa_kernel(sink_ref, q_ref, kc_ref, kp_ref, vc_ref, vp_ref, o_ref, *, heads, hd, group):
    n = pl.program_id(1)
    blk = q_ref.shape[0]
    kb = jnp.concatenate([kp_ref[...], kc_ref[...]], axis=0)
    vb = jnp.concatenate([vp_ref[...], vc_ref[...]], axis=0)
    q_i = lax.broadcasted_iota(jnp.int32, (blk, 2 * blk), 0)
    k_i = lax.broadcasted_iota(jnp.int32, (blk, 2 * blk), 1)
    valid = (k_i > q_i) & (k_i <= q_i + blk) & ((k_i >= blk) | (n > 0))
    outs = []
    for j in range(heads):
        gi = j // group
        s = _dot_nt(q_ref[:, j * hd:(j + 1) * hd], kb[:, gi * hd:(gi + 1) * hd])
        s = jnp.where(valid, s, -jnp.inf)
        sink = sink_ref[j]
        mx = jnp.maximum(jnp.max(s, axis=-1, keepdims=True), sink)
        p = jnp.exp(s - mx)
        denom = jnp.sum(p, axis=-1, keepdims=True) + jnp.exp(sink - mx)
        outs.append(_dot(p.astype(BF16), vb[:, gi * hd:(gi + 1) * hd]) / denom)
    o_ref[...] = jnp.concatenate(outs, axis=-1).astype(o_ref.dtype)


def _swa(q, k, v, sinks, bsz, seq):
    t, qw = q.shape
    kw = k.shape[1]
    heads = qw // B_HEAD_DIM
    blk = min(B_BLOCK, seq)
    nb = seq // blk
    cur = lambda width: pl.BlockSpec((blk, width), lambda b, n: (b * nb + n, 0))
    prev = lambda width: pl.BlockSpec((blk, width), lambda b, n: (b * nb + jnp.maximum(n - 1, 0), 0))
    return pl.pallas_call(
        functools.partial(_swa_kernel, heads=heads, hd=B_HEAD_DIM, group=B_GROUP),
        grid=(bsz, nb),
        in_specs=[pl.BlockSpec(memory_space=pltpu.SMEM), cur(qw), cur(kw), prev(kw), cur(kw), prev(kw)],
        out_specs=cur(qw),
        out_shape=jax.ShapeDtypeStruct((t, qw), BF16),
        compiler_params=_params("arbitrary", "arbitrary"),
        name="swa_mixer",
    )(sinks.astype(F32), q, k, k, v, v)


def _hgrn_kernel(q_ref, f_ref, v_ref, g_ref, lbp_ref, nw_ref, y_ref, st_ref, cf_ref, ck_ref,
                 cfc_ref, ckc_ref, vsc_ref, *,
                 chunk, dk, dv, layer):
    L = chunk
    nchunks = q_ref.shape[0] // L
    heads = q_ref.shape[1] // dk

    @pl.when(pl.program_id(1) == 0)
    def _():
        st_ref[...] = jnp.zeros_like(st_ref)

    lbp = lbp_ref[...]
    e = jnp.exp(lbp - jnp.max(lbp, axis=0, keepdims=True))
    sm = e / jnp.sum(e, axis=0, keepdims=True)
    lb = jnp.zeros_like(sm[0:1])
    for r in range(1, layer + 1):
        lb = lb + sm[r:r + 1]
    log_lb = jnp.log(lb)
    log_1m_lb = jnp.log1p(-lb)

    ones = jnp.ones((dk, LANES), BF16)
    trow = lax.broadcasted_iota(jnp.int32, (L, dv), 0)

    rb = min(HGRN_CUMSUM_ROWS, q_ref.shape[0])
    row_i = lax.broadcasted_iota(jnp.int32, (rb, rb), 0)
    col_i = lax.broadcasted_iota(jnp.int32, (rb, rb), 1)
    tril = jnp.where((col_i <= row_i) & (col_i // L == row_i // L), 1.0, 0.0).astype(BF16)
    for r0 in range(0, q_ref.shape[0], rb):
        fp = f_ref[r0:r0 + rb, :]
        ls = jnp.minimum(fp, 0.0) - jnp.log(1.0 + jnp.exp(-jnp.abs(fp)))
        x2 = log_1m_lb + ls
        logf = jnp.maximum(log_lb, x2) + jnp.log(1.0 + jnp.exp(-jnp.abs(log_lb - x2)))
        cf = _dot_exact_lhs01(tril, logf) * LOG2_E
        cf_ref[r0:r0 + rb, :] = cf
        ck_ref[r0:r0 + rb, :] = cf - (x2 - fp) * LOG2_E

    unroll = cfc_ref.shape[0]

    def body(c, carry):
        rows = [pl.ds(pl.multiple_of((c * unroll + u) * L, L), L) for u in range(unroll)]
        for u in range(unroll):
            cfc_ref[u] = cf_ref[rows[u], :]
            ckc_ref[u] = ck_ref[rows[u], :]
            vsc_ref[u] = v_ref[rows[u], :].astype(F32)
        for h in range(heads):
            cs = slice(h * dk, (h + 1) * dk)
            vs = slice(h * dv, (h + 1) * dv)
            for u in range(unroll):
                cfh = cfc_ref[u, :, cs]
                qh = q_ref[rows[u], cs]
                ws = []
                for s in range(L):
                    ws.append(qh * jnp.exp2(cfh - ckc_ref[u, s:s + 1, cs]))
                w_all = jnp.concatenate(ws, axis=0).astype(BF16)
                r_all = _dot(w_all, ones)
                st = st_ref[h]
                o = _dot_nt((qh * jnp.exp2(cfh)).astype(BF16), st.astype(BF16))
                for s in range(L):
                    a_s = jnp.where(trow >= s, r_all[s * L:(s + 1) * L, :], 0.0)
                    o = o + a_s * vsc_ref[u, s:s + 1, vs]
                last = cfc_ref[u, L - 1:L, cs]
                kdec = jnp.exp2(last - ckc_ref[u, :, cs]).astype(BF16)
                st_ref[h] = jnp.exp2(last) * st + _dot(vsc_ref[u, :, vs].T.astype(BF16), kdec)
                y = _rms_norm(o, nw_ref[:, vs]) * g_ref[rows[u], vs]
                y_ref[rows[u], vs] = y.astype(y_ref.dtype)
        return carry

    lax.fori_loop(0, nchunks // unroll, body, 0)


def _hgrn(q, f, v, g, lower_bounds, layer, norm_w, bsz, seq):
    t, kw = q.shape
    vw = v.shape[1]
    dk, dv = kw // C_HEADS, vw // C_HEADS
    ts = min(HGRN_STEP, seq)
    spb = seq // ts
    chunk = min(HGRN_CHUNK, ts)
    unroll = math.gcd(HGRN_UNROLL, ts // chunk)
    row = lambda width: pl.BlockSpec((ts, width), lambda b, s: (b * spb + s, 0))
    return pl.pallas_call(
        functools.partial(_hgrn_kernel, chunk=chunk, dk=dk, dv=dv, layer=layer),
        grid=(bsz, spb),
        in_specs=[row(kw), row(kw), row(vw), row(vw), _const_spec(lower_bounds.shape), _const_spec((1, vw))],
        out_specs=row(vw),
        out_shape=jax.ShapeDtypeStruct((t, vw), BF16),
        scratch_shapes=[pltpu.VMEM((C_HEADS, dv, dk), F32), pltpu.VMEM((ts, kw), F32), pltpu.VMEM((ts, kw), F32),
                        pltpu.VMEM((unroll, chunk, kw), F32), pltpu.VMEM((unroll, chunk, kw), F32),
                        pltpu.VMEM((unroll, chunk, vw), F32)],
        compiler_params=_params("arbitrary", "arbitrary"),
        name="hgrn2_mixer",
    )(q, f, v, g, lower_bounds, norm_w.reshape(1, vw))


def _diff_kernel(q_ref, k_ref, vt_ref, lam_ref, nw_ref, o_ref, acc_ref, st_ref, *, blk, hd, lam_init):
    qi = pl.program_id(2)
    lv = lam_ref[...]
    lam = (jnp.exp(jnp.sum(lv[0:1] * lv[1:2], axis=-1, keepdims=True))
           - jnp.exp(jnp.sum(lv[2:3] * lv[3:4], axis=-1, keepdims=True)) + lam_init)
    q = q_ref[...]
    lane = lax.broadcasted_iota(jnp.int32, q.shape, 1)
    zero = jnp.zeros_like(q)
    qz = (jnp.where(lane < hd, q, zero), jnp.where(lane >= hd, q, zero))
    acc_ref[...] = jnp.zeros_like(acc_ref)

    hw = 2 * hd
    ones = jnp.ones((DIFF_ONES_ROWS, blk), BF16)

    def scores(j):
        kb = k_ref[pl.ds(pl.multiple_of(j * blk, blk), blk), :]
        return [_dot_nt(kb, qz[c]) for c in range(2)]

    def step(j, m, masked):
        if not masked:
            nxt = scores(j + 1)
        vte = jnp.concatenate([vt_ref[:, pl.ds(pl.multiple_of(j * blk, blk), blk)], ones], axis=0)
        if masked:
            key_i = lax.broadcasted_iota(jnp.int32, (blk, blk), 0)
            qry_i = lax.broadcasted_iota(jnp.int32, (blk, blk), 1)
            keep = key_i <= qry_i
        m_out = []
        for c in range(2):
            st = st_ref[c]
            if masked:
                st = jnp.where(keep, st, -jnp.inf)
            m_new = jnp.maximum(m[c], jnp.max(st, axis=0, keepdims=True))
            alpha = jnp.exp2(m[c] - m_new)
            p = jnp.exp2(st - m_new).astype(BF16)
            acc_ref[c] = alpha * acc_ref[c] + _dot(vte, p)
            m_out.append(m_new)
        if not masked:
            for c in range(2):
                st_ref[c] = nxt[c]
        return tuple(m_out)

    first = scores(0)
    for c in range(2):
        st_ref[c] = first[c]
    neg = jnp.full((1, blk), -jnp.inf, F32)
    m = lax.fori_loop(0, qi, lambda j, cr: step(j, cr, False), (neg, neg))
    step(qi, m, True)
    norm = [acc_ref[c, :hw, :] * (1.0 / acc_ref[c, hw:hw + 1, :]) for c in range(2)]
    out_t = norm[0] - lam * norm[1]
    o_ref[...] = (_rms_norm(out_t.T, nw_ref[...]) * (1.0 - lam_init)).astype(o_ref.dtype)


def _diff_attn(q, k, vt, lam_vec, norm_w, lam_init, bsz, seq):
    t, w = q.shape
    hw = 2 * D_HEAD_DIM
    heads = w // hw
    blk = min(DIFF_BLOCK, seq)
    nq = seq // blk
    return pl.pallas_call(
        functools.partial(_diff_kernel, blk=blk, hd=D_HEAD_DIM, lam_init=lam_init),
        grid=(bsz, heads, nq),
        in_specs=[pl.BlockSpec((blk, hw), lambda b, h, i: (b * nq + i, h)),
                  pl.BlockSpec((seq, hw), lambda b, h, i: (b, h)),
                  pl.BlockSpec((hw, seq), lambda b, h, i: (h, b)),
                  _const_spec(lam_vec.shape), _const_spec((1, hw))],
        out_specs=pl.BlockSpec((blk, hw), lambda b, h, i: (b * nq + i, h)),
        out_shape=jax.ShapeDtypeStruct((t, w), BF16),
        scratch_shapes=[pltpu.VMEM((2, hw + DIFF_ONES_ROWS, blk), F32), pltpu.VMEM((2, blk, blk), F32)],
        compiler_params=_params("arbitrary", "arbitrary", "arbitrary"),
        name="diff_attn_mixer",
    )(q, k, vt, lam_vec, norm_w.reshape(1, hw))


def _pad_cols(w, width):
    return jnp.pad(w, ((0, 0), (0, width - w.shape[1])))


def kernel(x, c, positions, ada_w, ada_b, ln_g, ln_b, mlstm_w_in, mlstm_b_gate, mlstm_norm, mlstm_w_out, swa_w_in, swa_sinks, swa_w_out, hgrn_w_in, hgrn_lower_bounds, hgrn_norm, hgrn_w_out, diff_w_in, diff_lambda, diff_norm, diff_w_out, ffn_w_in, ffn_w_out):
    bsz, seq, d = x.shape
    depth = ada_w.shape[0]
    alpha = (2 * depth) ** 0.25
    t = bsz * seq
    xf = x.reshape(t, d)
    mod = _modulation(c, ada_w, ada_b)
    cos_t, sin_t = _rope_tables(positions)
    for i in range(depth):
        mixer, j = i % 4, i // 4
        sub = 2 * i
        if mixer == 0:
            vw = mlstm_w_out.shape[1]
            qk = vw // 2
            n_gate = 2 * A_HEADS
            w = _pad_cols(mlstm_w_in[j], 2 * qk + 2 * vw + LANES).astype(BF16)
            bg = _pad_cols(mlstm_b_gate[j].reshape(1, n_gate), LANES)
            q, k, v, o, g = _in_proj(
                functools.partial(_mlstm_in_kernel, qk=qk, vw=vw), xf, mod, sub, seq, w, [bg],
                [_const_spec((1, LANES))],
                [(qk, BF16, False), (qk, BF16, False), (vw, BF16, False), (vw, F32, False), (LANES, F32, False)])
            y = _mlstm(q, k, v, o, g, mlstm_norm[j], bsz, seq)
            w_out = mlstm_w_out[j]
        elif mixer == 1:
            qw = swa_w_out.shape[1]
            kw = qw // B_GROUP
            q, k, v = _in_proj(
                functools.partial(_attn_in_kernel, qw=qw, kw=kw, scale=B_HEAD_DIM ** -0.5), xf, mod, sub, seq,
                swa_w_in[j].astype(BF16), [cos_t, sin_t],
                [pl.BlockSpec((min(TOKEN_TILE, seq), LANES), lambda i: (i, 0))] * 2,
                [(qw, BF16, False), (kw, BF16, False), (kw, BF16, False)])
            y = _swa(q, k, v, swa_sinks[j], bsz, seq)
            w_out = swa_w_out[j]
        elif mixer == 2:
            vw = hgrn_w_out.shape[1]
            q, f, v, g = _in_proj(
                functools.partial(_hgrn_in_kernel, kw=vw, vw=vw), xf, mod, sub, seq,
                hgrn_w_in[j].astype(BF16), [], [],
                [(vw, F32, False), (vw, F32, False), (vw, BF16, False), (vw, F32, False)])
            y = _hgrn(q, f, v, g, hgrn_lower_bounds.astype(F32), i, hgrn_norm[j], bsz, seq)
            w_out = hgrn_w_out[j]
        else:
            w_ = diff_w_out.shape[1]
            lam_init = 0.8 - 0.6 * math.exp(-0.3 * i)
            w_qk = diff_w_in[j][:, :2 * w_].astype(BF16)
            w_vt = diff_w_in[j][:, 2 * w_:].T.astype(BF16)
            q, k, vt = _in_proj(
                functools.partial(_attn_in_vt_kernel, qw=w_, kw=w_, scale=D_HEAD_DIM ** -0.5 * math.log2(math.e)),
                xf, mod, sub, seq,
                w_qk, [w_vt, cos_t, sin_t],
                [_const_spec(w_vt.shape)] + [pl.BlockSpec((min(TOKEN_TILE, seq), LANES), lambda i: (i, 0))] * 2,
                [(w_, BF16, False), (w_, BF16, False), (w_, BF16, True)])
            y = _diff_attn(q, k, vt, diff_lambda[j].astype(F32), diff_norm[j], lam_init, bsz, seq)
            w_out = diff_w_out[j]
        xf = _layer_tail(y, w_out.astype(BF16), xf, mod, sub, seq, ffn_w_in[i].astype(BF16),
                         ffn_w_out[i].astype(BF16), ln_g[i], ln_b[i], alpha)
    return xf.reshape(bsz, seq, d)
```

```python
import functools
import math

import jax
import jax.numpy as jnp
from jax import lax
from jax.experimental import pallas as pl
from jax.experimental.pallas import tpu as pltpu

F32 = jnp.float32
BF16 = jnp.bfloat16

A_HEADS = 4
B_HEAD_DIM = 64
B_GROUP = 8
B_BLOCK = 128
C_HEADS = 8
D_HEAD_DIM = 64
ROPE_THETA = 500000.0
ROT_FRAC = 4
LN_EPS = 1e-5
RMS_EPS = 1e-6
LANES = 128
MXU_COLS = 256
LOG2_E = math.log2(math.e)
VMEM_LIMIT = 52 * 2 ** 20

TOKEN_TILE = 512
MLSTM_CHUNK = 128
MLSTM_STEP = 512
HGRN_CHUNK = 16
HGRN_STEP = 512
HGRN_CUMSUM_ROWS = 128
HGRN_UNROLL = 8
HGRN_SKEW = 1
DIFF_BLOCK = 512
DIFF_ONES_ROWS = 16


def _params(*sem):
    return pltpu.CompilerParams(dimension_semantics=sem, vmem_limit_bytes=VMEM_LIMIT)


def _dot(a, b):
    return jnp.dot(a, b, preferred_element_type=F32)


def _dot_nt(a, b):
    return lax.dot_general(a, b, (((1,), (1,)), ((), ())), preferred_element_type=F32)


def _split3(x):
    hi = x.astype(BF16)
    r = x - hi.astype(F32)
    mid = r.astype(BF16)
    lo = (r - mid.astype(F32)).astype(BF16)
    return hi, mid, lo


def _dot_exact_lhs01(mask_bf16, x):
    hi, mid, lo = _split3(x)
    return _dot(mask_bf16, hi) + (_dot(mask_bf16, mid) + _dot(mask_bf16, lo))


def _sigmoid(x):
    return 1.0 / (1.0 + jnp.exp(-x))


def _log_sigmoid(x):
    return jnp.minimum(x, 0.0) - jnp.log1p(jnp.exp(-jnp.abs(x)))


def _layer_norm(z, g, b):
    mu = jnp.mean(z, axis=-1, keepdims=True)
    zc = z - mu
    var = jnp.mean(zc * zc, axis=-1, keepdims=True)
    return zc * lax.rsqrt(var + LN_EPS) * g + b


def _rms_norm(h, w):
    return h * lax.rsqrt(jnp.mean(h * h, axis=-1, keepdims=True) + RMS_EPS) * w


def _const_spec(shape):
    nd = len(shape)
    return pl.BlockSpec(shape, lambda *_: (0,) * nd)


def _mod_spec(sub, part, tiles_per_batch, d):
    return pl.BlockSpec((None, None, None, 1, d), lambda i, *_: (sub, part, i // tiles_per_batch, 0, 0))


def _mod_kernel(c_ref, w_ref, b_ref, o_ref):
    c = c_ref[...]
    cond = c * _sigmoid(c)
    w = w_ref[...]
    a_hi = cond.astype(BF16)
    a_lo = (cond - a_hi.astype(F32)).astype(BF16)
    w_hi = w.astype(BF16)
    w_lo = (w - w_hi.astype(F32)).astype(BF16)
    acc = _dot(a_hi, w_hi) + (_dot(a_hi, w_lo) + _dot(a_lo, w_hi))
    o_ref[...] = acc + b_ref[...]


def _modulation(c, ada_w, ada_b):
    depth, _, d, d3 = ada_w.shape
    bsz = c.shape[0]
    nsub = depth * 2
    w = ada_w.reshape(nsub, d, d3)
    b = ada_b.reshape(nsub, 1, d3)
    out = pl.pallas_call(
        _mod_kernel,
        grid=(nsub, 3),
        in_specs=[
            pl.BlockSpec((bsz, d), lambda l, j: (0, 0)),
            pl.BlockSpec((None, d, d), lambda l, j: (l, 0, j)),
            pl.BlockSpec((None, 1, d), lambda l, j: (l, 0, j)),
        ],
        out_specs=pl.BlockSpec((None, None, bsz, d), lambda l, j: (l, j, 0, 0)),
        out_shape=jax.ShapeDtypeStruct((nsub, 3, bsz, d), F32),
        compiler_params=_params("arbitrary", "arbitrary"),
        name="adaln_mod",
    )(c, w, b)
    return out.reshape(nsub, 3, bsz, 1, d)


def _rope_kernel(pos_ref, inv_ref, sign_ref, cos_ref, sin_ref):
    ang = pos_ref[...].astype(F32) * inv_ref[...]
    cos_ref[...] = jnp.cos(ang)
    sin_ref[...] = jnp.sin(ang) * sign_ref[...]


def _rope_tables(positions):
    t = positions.size
    rot = B_HEAD_DIM // ROT_FRAC
    half = rot // 2
    inv = jnp.power(ROPE_THETA, -jnp.arange(half, dtype=F32) * 2.0 / rot)
    lane = jnp.arange(LANES) % B_HEAD_DIM
    inv_l = jnp.where(lane < rot, inv[lane % half], 0.0).reshape(1, LANES)
    sign_l = jnp.where(lane < half, -1.0, jnp.where(lane < rot, 1.0, 0.0)).astype(F32).reshape(1, LANES)
    tm = min(TOKEN_TILE, t)
    return pl.pallas_call(
        _rope_kernel,
        grid=(t // tm,),
        in_specs=[pl.BlockSpec((tm, 1), lambda i: (i, 0)), _const_spec((1, LANES)), _const_spec((1, LANES))],
        out_specs=[pl.BlockSpec((tm, LANES), lambda i: (i, 0))] * 2,
        out_shape=[jax.ShapeDtypeStruct((t, LANES), F32)] * 2,
        compiler_params=_params("arbitrary"),
        name="rope_tables",
    )(positions.reshape(t, 1), inv_l, sign_l)


def _rope_slab(slab, cos, sin_signed, lo_mask):
    fwd = pltpu.roll(slab, LANES - 8, axis=1)
    bwd = pltpu.roll(slab, 8, axis=1)
    return slab * cos + jnp.where(lo_mask, fwd, bwd) * sin_signed


def _rope_lo_mask(tm):
    lane = lax.broadcasted_iota(jnp.int32, (tm, LANES), 1) % B_HEAD_DIM
    return lane < (B_HEAD_DIM // ROT_FRAC // 2)


def _modulated(x_ref, sh_ref, sc_ref):
    return (x_ref[...] * (1.0 + sc_ref[...]) + sh_ref[...]).astype(BF16)


def _mlstm_in_kernel(x_ref, sh_ref, sc_ref, w_ref, bg_ref, q_ref, k_ref, v_ref, o_ref, g_ref, *, qk, vw):
    h = _modulated(x_ref, sh_ref, sc_ref)
    dk = qk // A_HEADS
    q_ref[...] = _dot(h, w_ref[:, :qk]).astype(BF16)
    k_ref[...] = (_dot(h, w_ref[:, qk:2 * qk]) * (dk ** -0.5)).astype(BF16)
    v_ref[...] = _dot(h, w_ref[:, 2 * qk:2 * qk + vw]).astype(BF16)
    o_ref[...] = _sigmoid(_dot(h, w_ref[:, 2 * qk + vw:2 * qk + 2 * vw]))
    g_ref[...] = _dot(h, w_ref[:, 2 * qk + 2 * vw:]) + bg_ref[...]


def _attn_in_body(x_ref, sh_ref, sc_ref, w_ref, wvt_ref, cos_ref, sin_ref, q_ref, k_ref, v_ref, *, qw, kw, scale):
    h = _modulated(x_ref, sh_ref, sc_ref)
    cos = cos_ref[...]
    sin = sin_ref[...]
    lo = _rope_lo_mask(h.shape[0])
    vw = 0 if wvt_ref is not None else w_ref.shape[1] - qw - kw

    def emit(col, slab):
        if col < qw:
            q_ref[:, col:col + LANES] = (_rope_slab(slab, cos, sin, lo) * scale).astype(BF16)
        elif col < qw + kw:
            k_ref[:, col - qw:col - qw + LANES] = _rope_slab(slab, cos, sin, lo).astype(BF16)
        else:
            v_ref[:, col - qw - kw:col - qw - kw + LANES] = slab.astype(BF16)

    total = qw + kw + vw
    for c0 in range(0, total, MXU_COLS):
        width = min(MXU_COLS, total - c0)
        wide = _dot(h, w_ref[:, c0:c0 + width])
        for off in range(0, width, LANES):
            emit(c0 + off, wide[:, off:off + LANES])
    if wvt_ref is not None:
        v_ref[...] = _dot_nt(wvt_ref[...], h).astype(BF16)


def _attn_in_kernel(x_ref, sh_ref, sc_ref, w_ref, cos_ref, sin_ref, q_ref, k_ref, v_ref, **kw):
    _attn_in_body(x_ref, sh_ref, sc_ref, w_ref, None, cos_ref, sin_ref, q_ref, k_ref, v_ref, **kw)


def _attn_in_vt_kernel(x_ref, sh_ref, sc_ref, w_ref, wvt_ref, cos_ref, sin_ref, q_ref, k_ref, v_ref, **kw):
    _attn_in_body(x_ref, sh_ref, sc_ref, w_ref, wvt_ref, cos_ref, sin_ref, q_ref, k_ref, v_ref, **kw)


def _hgrn_in_kernel(x_ref, sh_ref, sc_ref, w_ref, q_ref, f_ref, v_ref, g_ref, *, kw, vw):
    h = _modulated(x_ref, sh_ref, sc_ref)
    q_ref[...] = _dot(h, w_ref[:, :kw])
    f_ref[...] = _dot(h, w_ref[:, kw:2 * kw])
    v_ref[...] = _dot(h, w_ref[:, 2 * kw:2 * kw + vw]).astype(BF16)
    g = _dot(h, w_ref[:, 2 * kw + vw:])
    g_ref[...] = g * _sigmoid(g)


def _in_proj(body, x, mod, sub, seq, w, extra, extra_specs, outs):
    t, d = x.shape
    tm = min(TOKEN_TILE, seq)
    tpb = seq // tm
    row = lambda width: pl.BlockSpec((tm, width), lambda i: (i, 0))
    col = lambda width: pl.BlockSpec((width, tm), lambda i: (0, i))
    return pl.pallas_call(
        body,
        grid=(t // tm,),
        in_specs=[row(d), _mod_spec(sub, 0, tpb, d), _mod_spec(sub, 1, tpb, d), _const_spec(w.shape)] + extra_specs,
        out_specs=[col(wd) if tr else row(wd) for wd, _, tr in outs],
        out_shape=[jax.ShapeDtypeStruct((wd, t) if tr else (t, wd), dt) for wd, dt, tr in outs],
        compiler_params=_params("arbitrary"),
        name=body.func.__name__.strip("_") if isinstance(body, functools.partial) else body.__name__.strip("_"),
    )(x, mod, mod, w, *extra)


def _tail_kernel(y_ref, wp_ref, x_ref, gate0_ref, g0_ref, b0_ref, sh_ref, sc_ref, gate1_ref, wi_ref, wo_ref,
                 g1_ref, b1_ref, o_ref, *, hidden, chunk, alpha):
    y = _dot(y_ref[...], wp_ref[...])
    x = _layer_norm(alpha * x_ref[...] + (1.0 + gate0_ref[...]) * y, g0_ref[...], b0_ref[...])
    h = (x * (1.0 + sc_ref[...]) + sh_ref[...]).astype(BF16)
    acc = jnp.zeros(x.shape, F32)
    for j in range(hidden // chunk):
        a = _dot(h, wi_ref[:, j * chunk:(j + 1) * chunk])
        u = _dot(h, wi_ref[:, hidden + j * chunk:hidden + (j + 1) * chunk])
        act = (a * _sigmoid(a) * u).astype(BF16)
        acc = acc + _dot(act, wo_ref[j * chunk:(j + 1) * chunk, :])
    z = alpha * x + (1.0 + gate1_ref[...]) * acc
    o_ref[...] = _layer_norm(z, g1_ref[...], b1_ref[...])


def _ffn_chunk(hidden):
    best = LANES
    for c in range(LANES, hidden + 1, LANES):
        if hidden % c == 0 and c <= 1536:
            best = c
    return best


def _resident_spec(shape):
    nd = len(shape)
    return pl.BlockSpec(shape, lambda *_: (0,) * nd, pipeline_mode=pl.Buffered(1))


def _layer_tail(y, w_proj, x, mod, sub, seq, w_in, w_out, ln_g, ln_b, alpha):
    t, d = x.shape
    hidden = w_out.shape[0]
    tm = min(TOKEN_TILE, seq)
    tpb = seq // tm
    row = lambda width: pl.BlockSpec((tm, width), lambda i: (i, 0))
    vec = _const_spec((1, d))
    return pl.pallas_call(
        functools.partial(_tail_kernel, hidden=hidden, chunk=_ffn_chunk(hidden), alpha=alpha),
        grid=(t // tm,),
        in_specs=[row(y.shape[1]), _resident_spec(w_proj.shape), row(d), _mod_spec(sub, 2, tpb, d), vec, vec,
                  _mod_spec(sub + 1, 0, tpb, d), _mod_spec(sub + 1, 1, tpb, d), _mod_spec(sub + 1, 2, tpb, d),
                  _resident_spec(w_in.shape), _resident_spec(w_out.shape), vec, vec],
        out_specs=row(d),
        out_shape=jax.ShapeDtypeStruct((t, d), F32),
        compiler_params=_params("arbitrary"),
        name="out_proj_ffn",
    )(y, w_proj, x, mod, ln_g[0].reshape(1, d), ln_b[0].reshape(1, d), mod, mod, mod, w_in, w_out,
      ln_g[1].reshape(1, d), ln_b[1].reshape(1, d))


def _mlstm_kernel(q_ref, k_ref, v_ref, o_ref, g_ref, nw_ref, y_ref, c_ref, n_ref, m_ref, *, chunk, dk, dv):
    L = chunk
    nchunks = q_ref.shape[0] // L

    @pl.when(pl.program_id(1) == 0)
    def _():
        c_ref[...] = jnp.zeros_like(c_ref)
        n_ref[...] = jnp.zeros_like(n_ref)
        m_ref[...] = jnp.zeros_like(m_ref)

    row_i = lax.broadcasted_iota(jnp.int32, (L, L), 0)
    col_i = lax.broadcasted_iota(jnp.int32, (L, L), 1)
    causal = col_i <= row_i
    tril = jnp.where(causal, 1.0, 0.0).astype(BF16)
    lane = lax.broadcasted_iota(jnp.int32, (L, LANES), 1)

    def body(c, carry):
        rows = pl.ds(pl.multiple_of(c * L, L), L)
        g = g_ref[rows, :]
        logf = jnp.where(lane >= A_HEADS, _log_sigmoid(g), 0.0)
        bcum = _dot_exact_lhs01(tril, logf)
        z = jnp.where(lane < A_HEADS, g, bcum)
        zt = z.T
        for h in range(A_HEADS):
            i_col = z[:, h:h + 1]
            b_col = z[:, A_HEADS + h:A_HEADS + h + 1]
            i_row = zt[h:h + 1, :]
            b_row = zt[A_HEADS + h:A_HEADS + h + 1, :]
            m_prev = m_ref[h, 0:1, 0:1]
            dmat = jnp.where(causal, b_col - b_row + i_row, -jnp.inf)
            m_inter = b_col + m_prev
            m_t = jnp.maximum(m_inter, jnp.max(dmat, axis=-1, keepdims=True))
            qh = q_ref[rows, h * dk:(h + 1) * dk]
            kh = k_ref[rows, h * dk:(h + 1) * dk]
            vh = v_ref[rows, h * dv:(h + 1) * dv]
            a = _dot_nt(qh, kh) * jnp.exp(dmat - m_t)
            w_inter = jnp.exp(m_inter - m_t)
            c_prev = c_ref[h]
            n_prev = n_ref[h, 0:1, :]
            num = _dot(a.astype(BF16), vh) + w_inter * _dot(qh, c_prev.astype(BF16))
            qn = jnp.sum(qh.astype(F32) * n_prev, axis=-1, keepdims=True)
            den = jnp.sum(a, axis=-1, keepdims=True) + w_inter * qn
            hout = num / jnp.maximum(jnp.abs(den), jnp.exp(-m_t))
            y = _rms_norm(hout, nw_ref[:, h * dv:(h + 1) * dv]) * o_ref[rows, h * dv:(h + 1) * dv]
            y_ref[rows, h * dv:(h + 1) * dv] = y.astype(y_ref.dtype)
            b_last = b_col[L - 1:L, :]
            g_col = b_last - b_col + i_col
            m_new = jnp.maximum(b_last + m_prev, jnp.max(g_col, axis=0, keepdims=True))
            decay = jnp.exp(b_last + m_prev - m_new)
            kws = kh.astype(F32) * jnp.exp(g_col - m_new)
            c_ref[h] = decay * c_prev + _dot(kws.T.astype(BF16), vh)
            n_ref[h, 0:1, :] = decay * n_prev + jnp.sum(kws, axis=0, keepdims=True)
            m_ref[h] = jnp.broadcast_to(m_new, m_ref.shape[1:])
        return carry

    lax.fori_loop(0, nchunks, body, 0)


def _mlstm(q, k, v, o, g, norm_w, bsz, seq):
    t = q.shape[0]
    qk, vw = q.shape[1], v.shape[1]
    dk, dv = qk // A_HEADS, vw // A_HEADS
    ts = min(MLSTM_STEP, seq)
    spb = seq // ts
    row = lambda width: pl.BlockSpec((ts, width), lambda b, s: (b * spb + s, 0))
    return pl.pallas_call(
        functools.partial(_mlstm_kernel, chunk=min(MLSTM_CHUNK, ts), dk=dk, dv=dv),
        grid=(bsz, spb),
        in_specs=[row(qk), row(qk), row(vw), row(vw), row(LANES), _const_spec((1, vw))],
        out_specs=row(vw),
        out_shape=jax.ShapeDtypeStruct((t, vw), BF16),
        scratch_shapes=[pltpu.VMEM((A_HEADS, dk, dv), F32), pltpu.VMEM((A_HEADS, 8, dk), F32),
                        pltpu.VMEM((A_HEADS, 8, LANES), F32)],
        compiler_params=_params("arbitrary", "arbitrary"),
        name="mlstm_mixer",
    )(q, k, v, o, g, norm_w.reshape(1, vw))


def _swa_kernel(sink_ref, q_ref, kc_ref, kp_ref, vc_ref, vp_ref, o_ref, *, heads, hd, group):
    n = pl.program_id(1)
    blk = q_ref.shape[0]
    kb = jnp.concatenate([kp_ref[...], kc_ref[...]], axis=0)
    vb = jnp.concatenate([vp_ref[...], vc_ref[...]], axis=0)
    q_i = lax.broadcasted_iota(jnp.int32, (blk, 2 * blk), 0)
    k_i = lax.broadcasted_iota(jnp.int32, (blk, 2 * blk), 1)
    valid = (k_i > q_i) & (k_i <= q_i + blk) & ((k_i >= blk) | (n > 0))
    outs = []
    for j in range(heads):
        gi = j // group
        s = _dot_nt(q_ref[:, j * hd:(j + 1) * hd], kb[:, gi * hd:(gi + 1) * hd])
        s = jnp.where(valid, s, -jnp.inf)
        sink = sink_ref[j]
        mx = jnp.maximum(jnp.max(s, axis=-1, keepdims=True), sink)
        p = jnp.exp(s - mx)
        denom = jnp.sum(p, axis=-1, keepdims=True) + jnp.exp(sink - mx)
        outs.append(_dot(p.astype(BF16), vb[:, gi * hd:(gi + 1) * hd]) / denom)
    o_ref[...] = jnp.concatenate(outs, axis=-1).astype(o_ref.dtype)


def _swa(q, k, v, sinks, bsz, seq):
    t, qw = q.shape
    kw = k.shape[1]
    heads = qw // B_HEAD_DIM
    blk = min(B_BLOCK, seq)
    nb = seq // blk
    cur = lambda width: pl.BlockSpec((blk, width), lambda b, n: (b * nb + n, 0))
    prev = lambda width: pl.BlockSpec((blk, width), lambda b, n: (b * nb + jnp.maximum(n - 1, 0), 0))
    return pl.pallas_call(
        functools.partial(_swa_kernel, heads=heads, hd=B_HEAD_DIM, group=B_GROUP),
        grid=(bsz, nb),
        in_specs=[pl.BlockSpec(memory_space=pltpu.SMEM), cur(qw), cur(kw), prev(kw), cur(kw), prev(kw)],
        out_specs=cur(qw),
        out_shape=jax.ShapeDtypeStruct((t, qw), BF16),
        compiler_params=_params("arbitrary", "arbitrary"),
        name="swa_mixer",
    )(sinks.astype(F32), q, k, k, v, v)


def _hgrn_kernel(q_ref, f_ref, v_ref, g_ref, lbp_ref, nw_ref, y_ref, st_ref, cf_ref, ck_ref,
                 cfc_ref, ckc_ref, vsc_ref, *,
                 chunk, dk, dv, layer):
    L = chunk
    nchunks = q_ref.shape[0] // L
    heads = q_ref.shape[1] // dk

    @pl.when(pl.program_id(1) == 0)
    def _():
        st_ref[...] = jnp.zeros_like(st_ref)

    lbp = lbp_ref[...]
    e = jnp.exp(lbp - jnp.max(lbp, axis=0, keepdims=True))
    sm = e / jnp.sum(e, axis=0, keepdims=True)
    lb = jnp.zeros_like(sm[0:1])
    for r in range(1, layer + 1):
        lb = lb + sm[r:r + 1]
    log_lb = jnp.log(lb)
    log_1m_lb = jnp.log1p(-lb)

    ones = jnp.ones((dk, LANES), BF16)
    half = L // 2
    trow = lax.broadcasted_iota(jnp.int32, (half, dv), 0)

    rb = min(HGRN_CUMSUM_ROWS, q_ref.shape[0])
    row_i = lax.broadcasted_iota(jnp.int32, (rb, rb), 0)
    col_i = lax.broadcasted_iota(jnp.int32, (rb, rb), 1)
    tril = jnp.where((col_i <= row_i) & (col_i // L == row_i // L), 1.0, 0.0).astype(BF16)
    for r0 in range(0, q_ref.shape[0], rb):
        fp = f_ref[r0:r0 + rb, :]
        ls = jnp.minimum(fp, 0.0) - jnp.log(1.0 + jnp.exp(-jnp.abs(fp)))
        x2 = log_1m_lb + ls
        logf = jnp.maximum(log_lb, x2) + jnp.log(1.0 + jnp.exp(-jnp.abs(log_lb - x2)))
        cf = _dot_exact_lhs01(tril, logf) * LOG2_E
        cf_ref[r0:r0 + rb, :] = cf
        ck_ref[r0:r0 + rb, :] = cf - (x2 - fp) * LOG2_E

    unroll = cfc_ref.shape[0]

    def body(c, carry):
        rows = [pl.ds(pl.multiple_of((c * unroll + u) * L, L), L) for u in range(unroll)]
        for u in range(unroll):
            cfc_ref[u] = cf_ref[rows[u], :]
            ckc_ref[u] = ck_ref[rows[u], :]
            vsc_ref[u] = v_ref[rows[u], :].astype(F32)
        def issue(h, u):
            cs = slice(h * dk, (h + 1) * dk)
            vs = slice(h * dv, (h + 1) * dv)
            cfh = cfc_ref[u, :, cs]
            qh = q_ref[rows[u], cs]
            ws = []
            for s in range(L):
                t0 = 0 if s < half else half
                ws.append(qh[t0:] * jnp.exp2(cfh[t0:] - ckc_ref[u, s:s + 1, cs]))
            w_all = jnp.concatenate(ws, axis=0).astype(BF16)
            r_all = _dot(w_all, ones)
            st = st_ref[h]
            inter = _dot_nt((qh * jnp.exp2(cfh)).astype(BF16), st.astype(BF16))
            last = cfc_ref[u, L - 1:L, cs]
            kdec = jnp.exp2(last - ckc_ref[u, :, cs]).astype(BF16)
            st_ref[h] = jnp.exp2(last) * st + _dot(vsc_ref[u, :, vs].T.astype(BF16), kdec)
            return r_all, inter

        def finish(h, u, r_all, o):
            vs = slice(h * dv, (h + 1) * dv)
            o_lo, o_hi = o[:half], o[half:]
            for s in range(half):
                r = r_all[s * L:(s + 1) * L, :]
                v_s = vsc_ref[u, s:s + 1, vs]
                o_lo = o_lo + jnp.where(trow >= s, r[:half], 0.0) * v_s
                o_hi = o_hi + r[half:] * v_s
            for s in range(half, L):
                r = r_all[half * L + (s - half) * half:half * L + (s - half + 1) * half, :]
                o_hi = o_hi + jnp.where(trow >= s - half, r, 0.0) * vsc_ref[u, s:s + 1, vs]
            o = jnp.concatenate([o_lo, o_hi], axis=0)
            y = _rms_norm(o, nw_ref[:, vs]) * g_ref[rows[u], vs]
            y_ref[rows[u], vs] = y.astype(y_ref.dtype)

        work = [(h, u) for u in range(unroll) for h in range(heads)]
        pending = []
        for hu in work:
            pending.append((hu, issue(*hu)))
            if len(pending) > HGRN_SKEW:
                (h0, u0), (r0, i0) = pending.pop(0)
                finish(h0, u0, r0, i0)
        for (h0, u0), (r0, i0) in pending:
            finish(h0, u0, r0, i0)
        return carry

    lax.fori_loop(0, nchunks // unroll, body, 0)


def _hgrn(q, f, v, g, lower_bounds, layer, norm_w, bsz, seq):
    t, kw = q.shape
    vw = v.shape[1]
    dk, dv = kw // C_HEADS, vw // C_HEADS
    ts = min(HGRN_STEP, seq)
    spb = seq // ts
    chunk = min(HGRN_CHUNK, ts)
    unroll = math.gcd(HGRN_UNROLL, ts // chunk)
    row = lambda width: pl.BlockSpec((ts, width), lambda b, s: (b * spb + s, 0))
    return pl.pallas_call(
        functools.partial(_hgrn_kernel, chunk=chunk, dk=dk, dv=dv, layer=layer),
        grid=(bsz, spb),
        in_specs=[row(kw), row(kw), row(vw), row(vw), _const_spec(lower_bounds.shape), _const_spec((1, vw))],
        out_specs=row(vw),
        out_shape=jax.ShapeDtypeStruct((t, vw), BF16),
        scratch_shapes=[pltpu.VMEM((C_HEADS, dv, dk), F32), pltpu.VMEM((ts, kw), F32), pltpu.VMEM((ts, kw), F32),
                        pltpu.VMEM((unroll, chunk, kw), F32), pltpu.VMEM((unroll, chunk, kw), F32),
                        pltpu.VMEM((unroll, chunk, vw), F32)],
        compiler_params=_params("arbitrary", "arbitrary"),
        name="hgrn2_mixer",
    )(q, f, v, g, lower_bounds, norm_w.reshape(1, vw))


def _diff_kernel(q_ref, k_ref, vt_ref, lam_ref, nw_ref, o_ref, acc_ref, st_ref, *, blk, hd, lam_init):
    qi = pl.program_id(2)
    lv = lam_ref[...]
    lam = (jnp.exp(jnp.sum(lv[0:1] * lv[1:2], axis=-1, keepdims=True))
           - jnp.exp(jnp.sum(lv[2:3] * lv[3:4], axis=-1, keepdims=True)) + lam_init)
    q = q_ref[...]
    lane = lax.broadcasted_iota(jnp.int32, q.shape, 1)
    zero = jnp.zeros_like(q)
    qz = (jnp.where(lane < hd, q, zero), jnp.where(lane >= hd, q, zero))
    acc_ref[...] = jnp.zeros_like(acc_ref)

    hw = 2 * hd
    ones = jnp.ones((DIFF_ONES_ROWS, blk), BF16)

    def scores(j):
        kb = k_ref[pl.ds(pl.multiple_of(j * blk, blk), blk), :]
        return [_dot_nt(kb, qz[c]) for c in range(2)]

    def step(j, j_next, m, masked):
        nxt = scores(j_next)
        vte = jnp.concatenate([vt_ref[:, pl.ds(pl.multiple_of(j * blk, blk), blk)], ones], axis=0)
        if masked:
            key_i = lax.broadcasted_iota(jnp.int32, (blk, blk), 0)
            qry_i = lax.broadcasted_iota(jnp.int32, (blk, blk), 1)
            keep = key_i <= qry_i
        m_out = []
        for c in range(2):
            st = st_ref[c]
            if masked:
                st = jnp.where(keep, st, -jnp.inf)
            m_new = jnp.maximum(m[c], jnp.max(st, axis=0, keepdims=True))
            alpha = jnp.exp2(m[c] - m_new)
            p = jnp.exp2(st - m_new).astype(BF16)
            acc_ref[c] = alpha * acc_ref[c] + _dot(vte, p)
            m_out.append(m_new)
        for c in range(2):
            st_ref[c] = nxt[c]
        return tuple(m_out)

    first = scores(qi)
    for c in range(2):
        st_ref[c] = first[c]
    neg = jnp.full((1, blk), -jnp.inf, F32)
    m = step(qi, 0, (neg, neg), True)
    lax.fori_loop(0, qi, lambda j, cr: step(j, j + 1, cr, False), m)
    norm = [acc_ref[c, :hw, :] * (1.0 / acc_ref[c, hw:hw + 1, :]) for c in range(2)]
    out_t = norm[0] - lam * norm[1]
    o_ref[...] = (_rms_norm(out_t.T, nw_ref[...]) * (1.0 - lam_init)).astype(o_ref.dtype)


def _diff_attn(q, k, vt, lam_vec, norm_w, lam_init, bsz, seq):
    t, w = q.shape
    hw = 2 * D_HEAD_DIM
    heads = w // hw
    blk = min(DIFF_BLOCK, seq)
    nq = seq // blk
    return pl.pallas_call(
        functools.partial(_diff_kernel, blk=blk, hd=D_HEAD_DIM, lam_init=lam_init),
        grid=(bsz, heads, nq),
        in_specs=[pl.BlockSpec((blk, hw), lambda b, h, i: (b * nq + i, h)),
                  pl.BlockSpec((seq, hw), lambda b, h, i: (b, h)),
                  pl.BlockSpec((hw, seq), lambda b, h, i: (h, b)),
                  _const_spec(lam_vec.shape), _const_spec((1, hw))],
        out_specs=pl.BlockSpec((blk, hw), lambda b, h, i: (b * nq + i, h)),
        out_shape=jax.ShapeDtypeStruct((t, w), BF16),
        scratch_shapes=[pltpu.VMEM((2, hw + DIFF_ONES_ROWS, blk), F32), pltpu.VMEM((2, blk, blk), F32)],
        compiler_params=_params("arbitrary", "arbitrary", "arbitrary"),
        name="diff_attn_mixer",
    )(q, k, vt, lam_vec, norm_w.reshape(1, hw))


def _pad_cols(w, width):
    return jnp.pad(w, ((0, 0), (0, width - w.shape[1])))


def kernel(x, c, positions, ada_w, ada_b, ln_g, ln_b, mlstm_w_in, mlstm_b_gate, mlstm_norm, mlstm_w_out, swa_w_in, swa_sinks, swa_w_out, hgrn_w_in, hgrn_lower_bounds, hgrn_norm, hgrn_w_out, diff_w_in, diff_lambda, diff_norm, diff_w_out, ffn_w_in, ffn_w_out):
    bsz, seq, d = x.shape
    depth = ada_w.shape[0]
    alpha = (2 * depth) ** 0.25
    t = bsz * seq
    xf = x.reshape(t, d)
    mod = _modulation(c, ada_w, ada_b)
    cos_t, sin_t = _rope_tables(positions)
    for i in range(depth):
        mixer, j = i % 4, i // 4
        sub = 2 * i
        if mixer == 0:
            vw = mlstm_w_out.shape[1]
            qk = vw // 2
            n_gate = 2 * A_HEADS
            w = _pad_cols(mlstm_w_in[j], 2 * qk + 2 * vw + LANES).astype(BF16)
            bg = _pad_cols(mlstm_b_gate[j].reshape(1, n_gate), LANES)
            q, k, v, o, g = _in_proj(
                functools.partial(_mlstm_in_kernel, qk=qk, vw=vw), xf, mod, sub, seq, w, [bg],
                [_const_spec((1, LANES))],
                [(qk, BF16, False), (qk, BF16, False), (vw, BF16, False), (vw, F32, False), (LANES, F32, False)])
            y = _mlstm(q, k, v, o, g, mlstm_norm[j], bsz, seq)
            w_out = mlstm_w_out[j]
        elif mixer == 1:
            qw = swa_w_out.shape[1]
            kw = qw // B_GROUP
            q, k, v = _in_proj(
                functools.partial(_attn_in_kernel, qw=qw, kw=kw, scale=B_HEAD_DIM ** -0.5), xf, mod, sub, seq,
                swa_w_in[j].astype(BF16), [cos_t, sin_t],
                [pl.BlockSpec((min(TOKEN_TILE, seq), LANES), lambda i: (i, 0))] * 2,
                [(qw, BF16, False), (kw, BF16, False), (kw, BF16, False)])
            y = _swa(q, k, v, swa_sinks[j], bsz, seq)
            w_out = swa_w_out[j]
        elif mixer == 2:
            vw = hgrn_w_out.shape[1]
            q, f, v, g = _in_proj(
                functools.partial(_hgrn_in_kernel, kw=vw, vw=vw), xf, mod, sub, seq,
                hgrn_w_in[j].astype(BF16), [], [],
                [(vw, F32, False), (vw, F32, False), (vw, BF16, False), (vw, F32, False)])
            y = _hgrn(q, f, v, g, hgrn_lower_bounds.astype(F32), i, hgrn_norm[j], bsz, seq)
            w_out = hgrn_w_out[j]
        else:
            w_ = diff_w_out.shape[1]
            lam_init = 0.8 - 0.6 * math.exp(-0.3 * i)
            w_qk = diff_w_in[j][:, :2 * w_].astype(BF16)
            w_vt = diff_w_in[j][:, 2 * w_:].T.astype(BF16)
            q, k, vt = _in_proj(
                functools.partial(_attn_in_vt_kernel, qw=w_, kw=w_, scale=D_HEAD_DIM ** -0.5 * math.log2(math.e)),
                xf, mod, sub, seq,
                w_qk, [w_vt, cos_t, sin_t],
                [_const_spec(w_vt.shape)] + [pl.BlockSpec((min(TOKEN_TILE, seq), LANES), lambda i: (i, 0))] * 2,
                [(w_, BF16, False), (w_, BF16, False), (w_, BF16, True)])
            y = _diff_attn(q, k, vt, diff_lambda[j].astype(F32), diff_norm[j], lam_init, bsz, seq)
            w_out = diff_w_out[j]
        xf = _layer_tail(y, w_out.astype(BF16), xf, mod, sub, seq, ffn_w_in[i].astype(BF16),
                         ffn_w_out[i].astype(BF16), ln_g[i], ln_b[i], alpha)
    return xf.reshape(bsz, seq, d)
```

```python
import functools
import math

import jax
import jax.numpy as jnp
from jax import lax
from jax.experimental import pallas as pl
from jax.experimental.pallas import tpu as pltpu

F32 = jnp.float32
BF16 = jnp.bfloat16

A_HEADS = 4
B_HEAD_DIM = 64
B_GROUP = 8
B_BLOCK = 128
C_HEADS = 8
D_HEAD_DIM = 64
ROPE_THETA = 500000.0
ROT_FRAC = 4
LN_EPS = 1e-5
RMS_EPS = 1e-6
LANES = 128
MXU_COLS = 256
LOG2_E = math.log2(math.e)
VMEM_LIMIT = 52 * 2 ** 20

TOKEN_TILE = 512
MLSTM_CHUNK = 128
MLSTM_STEP = 512
HGRN_CHUNK = 16
HGRN_STEP = 512
HGRN_CUMSUM_ROWS = 128
HGRN_UNROLL = 8
HGRN_SKEW = 1
DIFF_BLOCK = 512
DIFF_ONES_ROWS = 16
SWA_ONES_ROWS = 16
SWA_HEADS_PER_DOT = 8


def _params(*sem):
    return pltpu.CompilerParams(dimension_semantics=sem, vmem_limit_bytes=VMEM_LIMIT)


def _dot(a, b):
    return jnp.dot(a, b, preferred_element_type=F32)


def _dot_nt(a, b):
    return lax.dot_general(a, b, (((1,), (1,)), ((), ())), preferred_element_type=F32)


def _split3(x):
    hi = x.astype(BF16)
    r = x - hi.astype(F32)
    mid = r.astype(BF16)
    lo = (r - mid.astype(F32)).astype(BF16)
    return hi, mid, lo


def _dot_exact_lhs01(mask_bf16, x):
    hi, mid, lo = _split3(x)
    return _dot(mask_bf16, hi) + (_dot(mask_bf16, mid) + _dot(mask_bf16, lo))


def _sigmoid(x):
    return 1.0 / (1.0 + jnp.exp(-x))


def _log_sigmoid(x):
    return jnp.minimum(x, 0.0) - jnp.log1p(jnp.exp(-jnp.abs(x)))


def _layer_norm(z, g, b):
    mu = jnp.mean(z, axis=-1, keepdims=True)
    zc = z - mu
    var = jnp.mean(zc * zc, axis=-1, keepdims=True)
    return zc * lax.rsqrt(var + LN_EPS) * g + b


def _rms_norm(h, w):
    return h * lax.rsqrt(jnp.mean(h * h, axis=-1, keepdims=True) + RMS_EPS) * w


def _const_spec(shape):
    nd = len(shape)
    return pl.BlockSpec(shape, lambda *_: (0,) * nd)


def _mod_spec(sub, part, tiles_per_batch, d):
    return pl.BlockSpec((None, None, None, 1, d), lambda i, *_: (sub, part, i // tiles_per_batch, 0, 0))


def _mod_kernel(c_ref, w_ref, b_ref, o_ref):
    c = c_ref[...]
    cond = c * _sigmoid(c)
    w = w_ref[...]
    a_hi = cond.astype(BF16)
    a_lo = (cond - a_hi.astype(F32)).astype(BF16)
    w_hi = w.astype(BF16)
    w_lo = (w - w_hi.astype(F32)).astype(BF16)
    acc = _dot(a_hi, w_hi) + (_dot(a_hi, w_lo) + _dot(a_lo, w_hi))
    o_ref[...] = acc + b_ref[...]


def _modulation(c, ada_w, ada_b):
    depth, _, d, d3 = ada_w.shape
    bsz = c.shape[0]
    nsub = depth * 2
    w = ada_w.reshape(nsub, d, d3)
    b = ada_b.reshape(nsub, 1, d3)
    out = pl.pallas_call(
        _mod_kernel,
        grid=(nsub, 3),
        in_specs=[
            pl.BlockSpec((bsz, d), lambda l, j: (0, 0)),
            pl.BlockSpec((None, d, d), lambda l, j: (l, 0, j)),
            pl.BlockSpec((None, 1, d), lambda l, j: (l, 0, j)),
        ],
        out_specs=pl.BlockSpec((None, None, bsz, d), lambda l, j: (l, j, 0, 0)),
        out_shape=jax.ShapeDtypeStruct((nsub, 3, bsz, d), F32),
        compiler_params=_params("arbitrary", "arbitrary"),
        name="adaln_mod",
    )(c, w, b)
    return out.reshape(nsub, 3, bsz, 1, d)


def _rope_kernel(pos_ref, inv_ref, ecos_ref, esin_ref, base_ref, cos_ref, sin_ref):
    ang = inv_ref[...] * pos_ref[...].astype(F32)
    cos_ref[...] = _spread(jnp.cos(ang), ecos_ref[...]) + base_ref[...]
    sin_ref[...] = _spread(jnp.sin(ang), esin_ref[...])


def _spread(vals_t, e):
    hi, mid, lo = _split3(vals_t.T)
    return _dot(hi, e) + (_dot(mid, e) + _dot(lo, e))


def _rope_tables(positions):
    t = positions.size
    rot = B_HEAD_DIM // ROT_FRAC
    half = rot // 2
    rows = 16
    inv = jnp.power(ROPE_THETA, -jnp.arange(half, dtype=F32) * 2.0 / rot)
    inv_c = jnp.zeros((rows, 1), F32).at[:half, 0].set(inv)
    lane = jnp.arange(LANES) % B_HEAD_DIM
    hit = (jnp.arange(rows)[:, None] == (lane % half)[None, :]) & (lane < rot)[None, :]
    e_cos = hit.astype(BF16)
    e_sin = (hit * jnp.where(lane < half, -1.0, 1.0)[None, :]).astype(BF16)
    base = (lane >= rot).astype(F32).reshape(1, LANES)
    tm = min(TOKEN_TILE, t)
    return pl.pallas_call(
        _rope_kernel,
        grid=(t // tm,),
        in_specs=[pl.BlockSpec((1, tm), lambda i: (0, i)), _const_spec((rows, 1)), _const_spec((rows, LANES)),
                  _const_spec((rows, LANES)), _const_spec((1, LANES))],
        out_specs=[pl.BlockSpec((tm, LANES), lambda i: (i, 0))] * 2,
        out_shape=[jax.ShapeDtypeStruct((t, LANES), F32)] * 2,
        compiler_params=_params("arbitrary"),
        name="rope_tables",
    )(positions.reshape(1, t), inv_c, e_cos, e_sin, base)


def _rope_slab(slab, cos, sin_signed, lo_mask):
    fwd = pltpu.roll(slab, LANES - 8, axis=1)
    bwd = pltpu.roll(slab, 8, axis=1)
    return slab * cos + jnp.where(lo_mask, fwd, bwd) * sin_signed


def _rope_lo_mask(tm):
    lane = lax.broadcasted_iota(jnp.int32, (tm, LANES), 1) % B_HEAD_DIM
    return lane < (B_HEAD_DIM // ROT_FRAC // 2)


def _modulated(x_ref, sh_ref, sc_ref):
    return (x_ref[...] * (1.0 + sc_ref[...]) + sh_ref[...]).astype(BF16)


def _mlstm_in_kernel(x_ref, sh_ref, sc_ref, w_ref, bg_ref, q_ref, k_ref, v_ref, o_ref, g_ref, *, qk, vw):
    h = _modulated(x_ref, sh_ref, sc_ref)
    dk = qk // A_HEADS
    q_ref[...] = _dot(h, w_ref[:, :qk]).astype(BF16)
    k_ref[...] = (_dot(h, w_ref[:, qk:2 * qk]) * (dk ** -0.5)).astype(BF16)
    v_ref[...] = _dot(h, w_ref[:, 2 * qk:2 * qk + vw]).astype(BF16)
    o_ref[...] = _sigmoid(_dot(h, w_ref[:, 2 * qk + vw:2 * qk + 2 * vw]))
    g_ref[...] = _dot(h, w_ref[:, 2 * qk + 2 * vw:]) + bg_ref[...]


def _attn_in_body(x_ref, sh_ref, sc_ref, w_ref, wvt_ref, cos_ref, sin_ref, q_ref, k_ref, v_ref, *,
                  qw, kw, scale, kdup=False):
    h = _modulated(x_ref, sh_ref, sc_ref)
    cos = cos_ref[...]
    sin = sin_ref[...]
    lo = _rope_lo_mask(h.shape[0])
    vw = 0 if wvt_ref is not None else w_ref.shape[1] - qw - kw
    first_head = lax.broadcasted_iota(jnp.int32, (h.shape[0], LANES), 1) < B_HEAD_DIM

    def emit(col, slab):
        if col < qw:
            q_ref[:, col:col + LANES] = (_rope_slab(slab, cos, sin, lo) * scale).astype(BF16)
        elif col < qw + kw and kdup:
            r = _rope_slab(slab, cos, sin, lo)
            sw = pltpu.roll(r, B_HEAD_DIM, axis=1)
            c = 2 * (col - qw)
            k_ref[:, c:c + LANES] = jnp.where(first_head, r, sw).astype(BF16)
            k_ref[:, c + LANES:c + 2 * LANES] = jnp.where(first_head, sw, r).astype(BF16)
        elif col < qw + kw:
            k_ref[:, col - qw:col - qw + LANES] = _rope_slab(slab, cos, sin, lo).astype(BF16)
        else:
            v_ref[:, col - qw - kw:col - qw - kw + LANES] = slab.astype(BF16)

    total = qw + kw + vw
    for c0 in range(0, total, MXU_COLS):
        width = min(MXU_COLS, total - c0)
        wide = _dot(h, w_ref[:, c0:c0 + width])
        for off in range(0, width, LANES):
            emit(c0 + off, wide[:, off:off + LANES])
    if wvt_ref is not None:
        v_ref[...] = _dot_nt(wvt_ref[...], h).astype(BF16)


def _attn_in_vt_kernel(x_ref, sh_ref, sc_ref, w_ref, wvt_ref, cos_ref, sin_ref, q_ref, k_ref, v_ref, **kw):
    _attn_in_body(x_ref, sh_ref, sc_ref, w_ref, wvt_ref, cos_ref, sin_ref, q_ref, k_ref, v_ref, **kw)


def _hgrn_in_kernel(x_ref, sh_ref, sc_ref, w_ref, q_ref, f_ref, v_ref, g_ref, *, kw, vw):
    h = _modulated(x_ref, sh_ref, sc_ref)
    q_ref[...] = _dot(h, w_ref[:, :kw])
    f_ref[...] = _dot(h, w_ref[:, kw:2 * kw])
    v_ref[...] = _dot(h, w_ref[:, 2 * kw:2 * kw + vw]).astype(BF16)
    g = _dot(h, w_ref[:, 2 * kw + vw:])
    g_ref[...] = g * _sigmoid(g)


def _in_proj(body, x, mod, sub, seq, w, extra, extra_specs, outs):
    t, d = x.shape
    tm = min(TOKEN_TILE, seq)
    tpb = seq // tm
    row = lambda width: pl.BlockSpec((tm, width), lambda i: (i, 0))
    col = lambda width: pl.BlockSpec((width, tm), lambda i: (0, i))
    return pl.pallas_call(
        body,
        grid=(t // tm,),
        in_specs=[row(d), _mod_spec(sub, 0, tpb, d), _mod_spec(sub, 1, tpb, d), _const_spec(w.shape)] + extra_specs,
        out_specs=[col(wd) if tr else row(wd) for wd, _, tr in outs],
        out_shape=[jax.ShapeDtypeStruct((wd, t) if tr else (t, wd), dt) for wd, dt, tr in outs],
        compiler_params=_params("arbitrary"),
        name=body.func.__name__.strip("_") if isinstance(body, functools.partial) else body.__name__.strip("_"),
    )(x, mod, mod, w, *extra)


def _tail_kernel(y_ref, wp_ref, x_ref, gate0_ref, g0_ref, b0_ref, sh_ref, sc_ref, gate1_ref, wi_ref, wo_ref,
                 g1_ref, b1_ref, o_ref, *, hidden, chunk, alpha):
    y = _dot(y_ref[...], wp_ref[...])
    x = _layer_norm(alpha * x_ref[...] + (1.0 + gate0_ref[...]) * y, g0_ref[...], b0_ref[...])
    h = (x * (1.0 + sc_ref[...]) + sh_ref[...]).astype(BF16)
    acc = jnp.zeros(x.shape, F32)
    for j in range(hidden // chunk):
        a = _dot(h, wi_ref[:, j * chunk:(j + 1) * chunk])
        u = _dot(h, wi_ref[:, hidden + j * chunk:hidden + (j + 1) * chunk])
        act = (a * _sigmoid(a) * u).astype(BF16)
        acc = acc + _dot(act, wo_ref[j * chunk:(j + 1) * chunk, :])
    z = alpha * x + (1.0 + gate1_ref[...]) * acc
    o_ref[...] = _layer_norm(z, g1_ref[...], b1_ref[...])


def _ffn_chunk(hidden):
    best = LANES
    for c in range(LANES, hidden + 1, LANES):
        if hidden % c == 0 and c <= 1536:
            best = c
    return best


def _resident_spec(shape):
    nd = len(shape)
    return pl.BlockSpec(shape, lambda *_: (0,) * nd, pipeline_mode=pl.Buffered(1))


def _layer_tail(y, w_proj, x, mod, sub, seq, w_in, w_out, ln_g, ln_b, alpha):
    t, d = x.shape
    hidden = w_out.shape[0]
    tm = min(TOKEN_TILE, seq)
    tpb = seq // tm
    row = lambda width: pl.BlockSpec((tm, width), lambda i: (i, 0))
    vec = _const_spec((1, d))
    return pl.pallas_call(
        functools.partial(_tail_kernel, hidden=hidden, chunk=_ffn_chunk(hidden), alpha=alpha),
        grid=(t // tm,),
        in_specs=[row(y.shape[1]), _resident_spec(w_proj.shape), row(d), _mod_spec(sub, 2, tpb, d), vec, vec,
                  _mod_spec(sub + 1, 0, tpb, d), _mod_spec(sub + 1, 1, tpb, d), _mod_spec(sub + 1, 2, tpb, d),
                  _resident_spec(w_in.shape), _resident_spec(w_out.shape), vec, vec],
        out_specs=row(d),
        out_shape=jax.ShapeDtypeStruct((t, d), F32),
        compiler_params=_params("arbitrary"),
        name="out_proj_ffn",
    )(y, w_proj, x, mod, ln_g[0].reshape(1, d), ln_b[0].reshape(1, d), mod, mod, mod, w_in, w_out,
      ln_g[1].reshape(1, d), ln_b[1].reshape(1, d))


def _mlstm_kernel(q_ref, k_ref, v_ref, o_ref, g_ref, nw_ref, y_ref, c_ref, n_ref, m_ref, *, chunk, dk, dv):
    L = chunk
    nchunks = q_ref.shape[0] // L

    @pl.when(pl.program_id(1) == 0)
    def _():
        c_ref[...] = jnp.zeros_like(c_ref)
        n_ref[...] = jnp.zeros_like(n_ref)
        m_ref[...] = jnp.zeros_like(m_ref)

    row_i = lax.broadcasted_iota(jnp.int32, (L, L), 0)
    col_i = lax.broadcasted_iota(jnp.int32, (L, L), 1)
    causal = col_i <= row_i
    tril = jnp.where(causal, 1.0, 0.0).astype(BF16)
    lane = lax.broadcasted_iota(jnp.int32, (L, LANES), 1)

    def body(c, carry):
        rows = pl.ds(pl.multiple_of(c * L, L), L)
        g = g_ref[rows, :]
        logf = jnp.where(lane >= A_HEADS, _log_sigmoid(g), 0.0)
        bcum = _dot_exact_lhs01(tril, logf)
        z = jnp.where(lane < A_HEADS, g, bcum)
        zt = z.T
        for h in range(A_HEADS):
            i_col = z[:, h:h + 1]
            b_col = z[:, A_HEADS + h:A_HEADS + h + 1]
            i_row = zt[h:h + 1, :]
            b_row = zt[A_HEADS + h:A_HEADS + h + 1, :]
            m_prev = m_ref[h, 0:1, 0:1]
            dmat = jnp.where(causal, b_col - b_row + i_row, -jnp.inf)
            m_inter = b_col + m_prev
            m_t = jnp.maximum(m_inter, jnp.max(dmat, axis=-1, keepdims=True))
            qh = q_ref[rows, h * dk:(h + 1) * dk]
            kh = k_ref[rows, h * dk:(h + 1) * dk]
            vh = v_ref[rows, h * dv:(h + 1) * dv]
            a = _dot_nt(qh, kh) * jnp.exp(dmat - m_t)
            w_inter = jnp.exp(m_inter - m_t)
            c_prev = c_ref[h]
            n_prev = n_ref[h, 0:1, :]
            num = _dot(a.astype(BF16), vh) + w_inter * _dot(qh, c_prev.astype(BF16))
            qn = jnp.sum(qh.astype(F32) * n_prev, axis=-1, keepdims=True)
            den = jnp.sum(a, axis=-1, keepdims=True) + w_inter * qn
            hout = num / jnp.maximum(jnp.abs(den), jnp.exp(-m_t))
            y = _rms_norm(hout, nw_ref[:, h * dv:(h + 1) * dv]) * o_ref[rows, h * dv:(h + 1) * dv]
            y_ref[rows, h * dv:(h + 1) * dv] = y.astype(y_ref.dtype)
            b_last = b_col[L - 1:L, :]
            g_col = b_last - b_col + i_col
            m_new = jnp.maximum(b_last + m_prev, jnp.max(g_col, axis=0, keepdims=True))
            decay = jnp.exp(b_last + m_prev - m_new)
            kws = kh.astype(F32) * jnp.exp(g_col - m_new)
            c_ref[h] = decay * c_prev + _dot(kws.T.astype(BF16), vh)
            n_ref[h, 0:1, :] = decay * n_prev + jnp.sum(kws, axis=0, keepdims=True)
            m_ref[h] = jnp.broadcast_to(m_new, m_ref.shape[1:])
        return carry

    lax.fori_loop(0, nchunks, body, 0)


def _mlstm(q, k, v, o, g, norm_w, bsz, seq):
    t = q.shape[0]
    qk, vw = q.shape[1], v.shape[1]
    dk, dv = qk // A_HEADS, vw // A_HEADS
    ts = min(MLSTM_STEP, seq)
    spb = seq // ts
    row = lambda width: pl.BlockSpec((ts, width), lambda b, s: (b * spb + s, 0))
    return pl.pallas_call(
        functools.partial(_mlstm_kernel, chunk=min(MLSTM_CHUNK, ts), dk=dk, dv=dv),
        grid=(bsz, spb),
        in_specs=[row(qk), row(qk), row(vw), row(vw), row(LANES), _const_spec((1, vw))],
        out_specs=row(vw),
        out_shape=jax.ShapeDtypeStruct((t, vw), BF16),
        scratch_shapes=[pltpu.VMEM((A_HEADS, dk, dv), F32), pltpu.VMEM((A_HEADS, 8, dk), F32),
                        pltpu.VMEM((A_HEADS, 8, LANES), F32)],
        compiler_params=_params("arbitrary", "arbitrary"),
        name="mlstm_mixer",
    )(q, k, v, o, g, norm_w.reshape(1, vw))


def _swa_kernel(sink_ref, q_ref, kc_ref, kp_ref, vtc_ref, vtp_ref, o_ref, *, heads, hd, group):
    n = pl.program_id(1)
    blk = q_ref.shape[0]
    hpd = SWA_HEADS_PER_DOT
    key_i = lax.broadcasted_iota(jnp.int32, (2 * blk, blk), 0)
    qry_i = lax.broadcasted_iota(jnp.int32, (2 * blk, blk), 1)
    valid = (key_i > qry_i) & (key_i <= qry_i + blk) & ((key_i >= blk) | (n > 0))
    bias = jnp.concatenate([jnp.where(valid, 0.0, -jnp.inf)] * hpd, axis=1)
    upper = lax.broadcasted_iota(jnp.int32, (blk, LANES), 1) >= hd
    ones = jnp.ones((SWA_ONES_ROWS, 2 * blk), BF16)

    def scores(j0):
        gi = j0 // group
        ks = slice(gi * LANES, (gi + 1) * LANES)
        k2 = jnp.concatenate([kp_ref[:, ks], kc_ref[:, ks]], axis=0)
        qz = []
        for j in range(j0, j0 + hpd):
            slab = q_ref[:, (j // 2) * LANES:(j // 2 + 1) * LANES]
            own = upper if j % 2 else jnp.logical_not(upper)
            qz.append(jnp.where(own, slab, jnp.zeros_like(slab)))
        return _dot_nt(k2, jnp.concatenate(qz, axis=0))

    sts = [scores(j0) for j0 in range(0, heads, hpd)]
    for idx, j0 in enumerate(range(0, heads, hpd)):
        gi = j0 // group
        st = sts[idx] + bias
        off = (j0 % group) * blk
        sink = sink_ref[gi:gi + 1, off:off + hpd * blk] * LOG2_E
        m = jnp.maximum(jnp.max(st, axis=0, keepdims=True), sink)
        p = jnp.exp2(st - m).astype(BF16)
        vs = slice(gi * hd, (gi + 1) * hd)
        vte = jnp.concatenate([jnp.concatenate([vtp_ref[vs, :], vtc_ref[vs, :]], axis=1), ones], axis=0)
        acc = _dot(vte, p)
        ot = acc[:hd, :] * (1.0 / (acc[hd:hd + 1, :] + jnp.exp2(sink - m)))
        for pr in range(hpd // 2):
            pair = jnp.concatenate([ot[:, (2 * pr) * blk:(2 * pr + 1) * blk],
                                    ot[:, (2 * pr + 1) * blk:(2 * pr + 2) * blk]], axis=0)
            c0 = (j0 // 2 + pr) * LANES
            o_ref[:, c0:c0 + LANES] = pair.T.astype(o_ref.dtype)


def _swa(q, k2, vt, sinks, bsz, seq):
    t, qw = q.shape
    kw = vt.shape[0]
    heads = qw // B_HEAD_DIM
    blk = min(B_BLOCK, seq)
    nb = seq // blk
    sink_rows = jnp.repeat(sinks.astype(F32).reshape(heads // B_GROUP, B_GROUP), blk, axis=1)
    cur = lambda width: pl.BlockSpec((blk, width), lambda b, n: (b * nb + n, 0))
    prev = lambda width: pl.BlockSpec((blk, width), lambda b, n: (b * nb + jnp.maximum(n - 1, 0), 0))
    cur_t = pl.BlockSpec((kw, blk), lambda b, n: (0, b * nb + n))
    prev_t = pl.BlockSpec((kw, blk), lambda b, n: (0, b * nb + jnp.maximum(n - 1, 0)))
    return pl.pallas_call(
        functools.partial(_swa_kernel, heads=heads, hd=B_HEAD_DIM, group=B_GROUP),
        grid=(bsz, nb),
        in_specs=[_const_spec(sink_rows.shape), cur(qw), cur(2 * kw), prev(2 * kw), cur_t, prev_t],
        out_specs=cur(qw),
        out_shape=jax.ShapeDtypeStruct((t, qw), BF16),
        compiler_params=_params("arbitrary", "arbitrary"),
        name="swa_mixer",
    )(sink_rows, q, k2, k2, vt, vt)


def _hgrn_kernel(q_ref, f_ref, v_ref, g_ref, lbp_ref, nw_ref, y_ref, st_ref, cf_ref, ck_ref,
                 cfc_ref, ckc_ref, vsc_ref, *,
                 chunk, dk, dv, layer):
    L = chunk
    nchunks = q_ref.shape[0] // L
    heads = q_ref.shape[1] // dk

    @pl.when(pl.program_id(1) == 0)
    def _():
        st_ref[...] = jnp.zeros_like(st_ref)

    lbp = lbp_ref[...]
    e = jnp.exp(lbp - jnp.max(lbp, axis=0, keepdims=True))
    sm = e / jnp.sum(e, axis=0, keepdims=True)
    lb = jnp.zeros_like(sm[0:1])
    for r in range(1, layer + 1):
        lb = lb + sm[r:r + 1]
    log_lb = jnp.log(lb)
    log_1m_lb = jnp.log1p(-lb)

    ones = jnp.ones((dk, LANES), BF16)
    half = L // 2
    trow = lax.broadcasted_iota(jnp.int32, (half, dv), 0)

    rb = min(HGRN_CUMSUM_ROWS, q_ref.shape[0])
    row_i = lax.broadcasted_iota(jnp.int32, (rb, rb), 0)
    col_i = lax.broadcasted_iota(jnp.int32, (rb, rb), 1)
    tril = jnp.where((col_i <= row_i) & (col_i // L == row_i // L), 1.0, 0.0).astype(BF16)
    for r0 in range(0, q_ref.shape[0], rb):
        fp = f_ref[r0:r0 + rb, :]
        ls = jnp.minimum(fp, 0.0) - jnp.log(1.0 + jnp.exp(-jnp.abs(fp)))
        x2 = log_1m_lb + ls
        logf = jnp.maximum(log_lb, x2) + jnp.log(1.0 + jnp.exp(-jnp.abs(log_lb - x2)))
        cf = _dot_exact_lhs01(tril, logf) * LOG2_E
        cf_ref[r0:r0 + rb, :] = cf
        ck_ref[r0:r0 + rb, :] = cf - (x2 - fp) * LOG2_E

    unroll = cfc_ref.shape[0]

    def body(c, carry):
        rows = [pl.ds(pl.multiple_of((c * unroll + u) * L, L), L) for u in range(unroll)]
        for u in range(unroll):
            cfc_ref[u] = cf_ref[rows[u], :]
            ckc_ref[u] = ck_ref[rows[u], :]
            vsc_ref[u] = v_ref[rows[u], :].astype(F32)
        def issue(h, u):
            cs = slice(h * dk, (h + 1) * dk)
            vs = slice(h * dv, (h + 1) * dv)
            cfh = cfc_ref[u, :, cs]
            qh = q_ref[rows[u], cs]
            ws = []
            for s in range(L):
                t0 = 0 if s < half else half
                ws.append(qh[t0:] * jnp.exp2(cfh[t0:] - ckc_ref[u, s:s + 1, cs]))
            w_all = jnp.concatenate(ws, axis=0).astype(BF16)
            r_all = _dot(w_all, ones)
            st = st_ref[h]
            inter = _dot_nt((qh * jnp.exp2(cfh)).astype(BF16), st.astype(BF16))
            last = cfc_ref[u, L - 1:L, cs]
            kdec = jnp.exp2(last - ckc_ref[u, :, cs]).astype(BF16)
            st_ref[h] = jnp.exp2(last) * st + _dot(vsc_ref[u, :, vs].T.astype(BF16), kdec)
            return r_all, inter

        def finish(h, u, r_all, o):
            vs = slice(h * dv, (h + 1) * dv)
            o_lo, o_hi = o[:half], o[half:]
            for s in range(half):
                r = r_all[s * L:(s + 1) * L, :]
                v_s = vsc_ref[u, s:s + 1, vs]
                o_lo = o_lo + jnp.where(trow >= s, r[:half], 0.0) * v_s
                o_hi = o_hi + r[half:] * v_s
            for s in range(half, L):
                r = r_all[half * L + (s - half) * half:half * L + (s - half + 1) * half, :]
                o_hi = o_hi + jnp.where(trow >= s - half, r, 0.0) * vsc_ref[u, s:s + 1, vs]
            o = jnp.concatenate([o_lo, o_hi], axis=0)
            y = _rms_norm(o, nw_ref[:, vs]) * g_ref[rows[u], vs]
            y_ref[rows[u], vs] = y.astype(y_ref.dtype)

        work = [(h, u) for u in range(unroll) for h in range(heads)]
        pending = []
        for hu in work:
            pending.append((hu, issue(*hu)))
            if len(pending) > HGRN_SKEW:
                (h0, u0), (r0, i0) = pending.pop(0)
                finish(h0, u0, r0, i0)
        for (h0, u0), (r0, i0) in pending:
            finish(h0, u0, r0, i0)
        return carry

    lax.fori_loop(0, nchunks // unroll, body, 0)


def _hgrn(q, f, v, g, lower_bounds, layer, norm_w, bsz, seq):
    t, kw = q.shape
    vw = v.shape[1]
    dk, dv = kw // C_HEADS, vw // C_HEADS
    ts = min(HGRN_STEP, seq)
    spb = seq // ts
    chunk = min(HGRN_CHUNK, ts)
    unroll = math.gcd(HGRN_UNROLL, ts // chunk)
    row = lambda width: pl.BlockSpec((ts, width), lambda b, s: (b * spb + s, 0))
    return pl.pallas_call(
        functools.partial(_hgrn_kernel, chunk=chunk, dk=dk, dv=dv, layer=layer),
        grid=(bsz, spb),
        in_specs=[row(kw), row(kw), row(vw), row(vw), _const_spec(lower_bounds.shape), _const_spec((1, vw))],
        out_specs=row(vw),
        out_shape=jax.ShapeDtypeStruct((t, vw), BF16),
        scratch_shapes=[pltpu.VMEM((C_HEADS, dv, dk), F32), pltpu.VMEM((ts, kw), F32), pltpu.VMEM((ts, kw), F32),
                        pltpu.VMEM((unroll, chunk, kw), F32), pltpu.VMEM((unroll, chunk, kw), F32),
                        pltpu.VMEM((unroll, chunk, vw), F32)],
        compiler_params=_params("arbitrary", "arbitrary"),
        name="hgrn2_mixer",
    )(q, f, v, g, lower_bounds, norm_w.reshape(1, vw))


def _diff_kernel(q_ref, k_ref, vt_ref, lam_ref, nw_ref, o_ref, acc_ref, st_ref, *, blk, hd, lam_init):
    qi = pl.program_id(2)
    lv = lam_ref[...]
    lam = (jnp.exp(jnp.sum(lv[0:1] * lv[1:2], axis=-1, keepdims=True))
           - jnp.exp(jnp.sum(lv[2:3] * lv[3:4], axis=-1, keepdims=True)) + lam_init)
    q = q_ref[...]
    lane = lax.broadcasted_iota(jnp.int32, q.shape, 1)
    zero = jnp.zeros_like(q)
    qz = (jnp.where(lane < hd, q, zero), jnp.where(lane >= hd, q, zero))
    acc_ref[...] = jnp.zeros_like(acc_ref)

    hw = 2 * hd
    ones = jnp.ones((DIFF_ONES_ROWS, blk), BF16)

    def scores_into(slot, j):
        kb = k_ref[pl.ds(pl.multiple_of(j * blk, blk), blk), :]
        for c in range(2):
            st_ref[2 * slot + c] = _dot_nt(kb, qz[c])

    def step(j, j_next, m, masked, src):
        if j_next is not None:
            scores_into(1 - src, j_next)
        vte = jnp.concatenate([vt_ref[:, pl.ds(pl.multiple_of(j * blk, blk), blk)], ones], axis=0)
        if masked:
            key_i = lax.broadcasted_iota(jnp.int32, (blk, blk), 0)
            qry_i = lax.broadcasted_iota(jnp.int32, (blk, blk), 1)
            keep = key_i <= qry_i
        m_out = []
        for c in range(2):
            st = st_ref[2 * src + c]
            if masked:
                st = jnp.where(keep, st, -jnp.inf)
            m_new = jnp.maximum(m[c], jnp.max(st, axis=0, keepdims=True))
            alpha = jnp.exp2(m[c] - m_new)
            p = jnp.exp2(st - m_new).astype(BF16)
            acc_ref[c] = alpha * acc_ref[c] + _dot(vte, p)
            m_out.append(m_new)
        return tuple(m_out)

    scores_into(0, qi)
    neg = jnp.full((1, blk), -jnp.inf, F32)
    m = step(qi, 0, (neg, neg), True, 0)

    def pair(i, cr):
        cr = step(2 * i, 2 * i + 1, cr, False, 1)
        return step(2 * i + 1, 2 * i + 2, cr, False, 0)

    m = lax.fori_loop(0, qi // 2, pair, m)

    @pl.when(qi % 2 == 1)
    def _():
        step(qi - 1, None, m, False, 1)

    norm = [acc_ref[c, :hw, :] * (1.0 / acc_ref[c, hw:hw + 1, :]) for c in range(2)]
    out_t = norm[0] - lam * norm[1]
    o_ref[...] = (_rms_norm(out_t.T, nw_ref[...]) * (1.0 - lam_init)).astype(o_ref.dtype)


def _diff_attn(q, k, vt, lam_vec, norm_w, lam_init, bsz, seq):
    t, w = q.shape
    hw = 2 * D_HEAD_DIM
    heads = w // hw
    blk = min(DIFF_BLOCK, seq)
    nq = seq // blk
    return pl.pallas_call(
        functools.partial(_diff_kernel, blk=blk, hd=D_HEAD_DIM, lam_init=lam_init),
        grid=(bsz, heads, nq),
        in_specs=[pl.BlockSpec((blk, hw), lambda b, h, i: (b * nq + i, h)),
                  pl.BlockSpec((seq, hw), lambda b, h, i: (b, h)),
                  pl.BlockSpec((hw, seq), lambda b, h, i: (h, b)),
                  _const_spec(lam_vec.shape), _const_spec((1, hw))],
        out_specs=pl.BlockSpec((blk, hw), lambda b, h, i: (b * nq + i, h)),
        out_shape=jax.ShapeDtypeStruct((t, w), BF16),
        scratch_shapes=[pltpu.VMEM((2, hw + DIFF_ONES_ROWS, blk), F32), pltpu.VMEM((4, blk, blk), F32)],
        compiler_params=_params("arbitrary", "arbitrary", "arbitrary"),
        name="diff_attn_mixer",
    )(q, k, vt, lam_vec, norm_w.reshape(1, hw))


def _pad_cols(w, width):
    return jnp.pad(w, ((0, 0), (0, width - w.shape[1])))


def kernel(x, c, positions, ada_w, ada_b, ln_g, ln_b, mlstm_w_in, mlstm_b_gate, mlstm_norm, mlstm_w_out, swa_w_in, swa_sinks, swa_w_out, hgrn_w_in, hgrn_lower_bounds, hgrn_norm, hgrn_w_out, diff_w_in, diff_lambda, diff_norm, diff_w_out, ffn_w_in, ffn_w_out):
    bsz, seq, d = x.shape
    depth = ada_w.shape[0]
    alpha = (2 * depth) ** 0.25
    t = bsz * seq
    xf = x.reshape(t, d)
    mod = _modulation(c, ada_w, ada_b)
    cos_t, sin_t = _rope_tables(positions)
    for i in range(depth):
        mixer, j = i % 4, i // 4
        sub = 2 * i
        if mixer == 0:
            vw = mlstm_w_out.shape[1]
            qk = vw // 2
            n_gate = 2 * A_HEADS
            w = _pad_cols(mlstm_w_in[j], 2 * qk + 2 * vw + LANES).astype(BF16)
            bg = _pad_cols(mlstm_b_gate[j].reshape(1, n_gate), LANES)
            q, k, v, o, g = _in_proj(
                functools.partial(_mlstm_in_kernel, qk=qk, vw=vw), xf, mod, sub, seq, w, [bg],
                [_const_spec((1, LANES))],
                [(qk, BF16, False), (qk, BF16, False), (vw, BF16, False), (vw, F32, False), (LANES, F32, False)])
            y = _mlstm(q, k, v, o, g, mlstm_norm[j], bsz, seq)
            w_out = mlstm_w_out[j]
        elif mixer == 1:
            qw = swa_w_out.shape[1]
            kw = qw // B_GROUP
            w_qk = swa_w_in[j][:, :qw + kw].astype(BF16)
            w_vt = swa_w_in[j][:, qw + kw:].T.astype(BF16)
            q, k2, vt = _in_proj(
                functools.partial(_attn_in_vt_kernel, qw=qw, kw=kw, scale=B_HEAD_DIM ** -0.5 * LOG2_E, kdup=True),
                xf, mod, sub, seq, w_qk, [w_vt, cos_t, sin_t],
                [_const_spec(w_vt.shape)] + [pl.BlockSpec((min(TOKEN_TILE, seq), LANES), lambda i: (i, 0))] * 2,
                [(qw, BF16, False), (2 * kw, BF16, False), (kw, BF16, True)])
            y = _swa(q, k2, vt, swa_sinks[j], bsz, seq)
            w_out = swa_w_out[j]
        elif mixer == 2:
            vw = hgrn_w_out.shape[1]
            q, f, v, g = _in_proj(
                functools.partial(_hgrn_in_kernel, kw=vw, vw=vw), xf, mod, sub, seq,
                hgrn_w_in[j].astype(BF16), [], [],
                [(vw, F32, False), (vw, F32, False), (vw, BF16, False), (vw, F32, False)])
            y = _hgrn(q, f, v, g, hgrn_lower_bounds.astype(F32), i, hgrn_norm[j], bsz, seq)
            w_out = hgrn_w_out[j]
        else:
            w_ = diff_w_out.shape[1]
            lam_init = 0.8 - 0.6 * math.exp(-0.3 * i)
            w_qk = diff_w_in[j][:, :2 * w_].astype(BF16)
            w_vt = diff_w_in[j][:, 2 * w_:].T.astype(BF16)
            q, k, vt = _in_proj(
                functools.partial(_attn_in_vt_kernel, qw=w_, kw=w_, scale=D_HEAD_DIM ** -0.5 * math.log2(math.e)),
                xf, mod, sub, seq,
                w_qk, [w_vt, cos_t, sin_t],
                [_const_spec(w_vt.shape)] + [pl.BlockSpec((min(TOKEN_TILE, seq), LANES), lambda i: (i, 0))] * 2,
                [(w_, BF16, False), (w_, BF16, False), (w_, BF16, True)])
            y = _diff_attn(q, k, vt, diff_lambda[j].astype(F32), diff_norm[j], lam_init, bsz, seq)
            w_out = diff_w_out[j]
        xf = _layer_tail(y, w_out.astype(BF16), xf, mod, sub, seq, ffn_w_in[i].astype(BF16),
                         ffn_w_out[i].astype(BF16), ln_g[i], ln_b[i], alpha)
    return xf.reshape(bsz, seq, d)
```

```python
import functools
import math

import jax
import jax.numpy as jnp
from jax import lax
from jax.experimental import pallas as pl
from jax.experimental.pallas import tpu as pltpu

F32 = jnp.float32
BF16 = jnp.bfloat16

A_HEADS = 4
B_HEAD_DIM = 64
B_GROUP = 8
B_BLOCK = 128
C_HEADS = 8
D_HEAD_DIM = 64
ROPE_THETA = 500000.0
ROT_FRAC = 4
LN_EPS = 1e-5
RMS_EPS = 1e-6
LANES = 128
MXU_COLS = 256
LOG2_E = math.log2(math.e)
VMEM_LIMIT = 52 * 2 ** 20

TOKEN_TILE = 512
TAIL_TILE = 512
TAIL_PARTS = 2
FFN_MAX_CHUNK = 4096
MLSTM_CHUNK = 128
MLSTM_STEP = 512
MLSTM_SEQS = 1
HGRN_CHUNK = 16
HGRN_STEP = 512
HGRN_CUMSUM_ROWS = 128
HGRN_UNROLL = 8
HGRN_SKEW = 1
DIFF_BLOCK = 512
DIFF_ONES_ROWS = 16
SWA_ONES_ROWS = 16
SWA_HEADS_PER_DOT = 8


def _params(*sem):
    return pltpu.CompilerParams(dimension_semantics=sem, vmem_limit_bytes=VMEM_LIMIT)


def _dot(a, b):
    return jnp.dot(a, b, preferred_element_type=F32)


def _dot_nt(a, b):
    return lax.dot_general(a, b, (((1,), (1,)), ((), ())), preferred_element_type=F32)


def _split3(x):
    hi = x.astype(BF16)
    r = x - hi.astype(F32)
    mid = r.astype(BF16)
    lo = (r - mid.astype(F32)).astype(BF16)
    return hi, mid, lo


def _dot_exact_lhs01(mask_bf16, x):
    hi, mid, lo = _split3(x)
    return _dot(mask_bf16, hi) + (_dot(mask_bf16, mid) + _dot(mask_bf16, lo))


def _sigmoid(x):
    return 1.0 / (1.0 + jnp.exp(-x))


def _log_sigmoid(x):
    return jnp.minimum(x, 0.0) - jnp.log1p(jnp.exp(-jnp.abs(x)))


def _layer_norm(z, g, b):
    mu = jnp.mean(z, axis=-1, keepdims=True)
    zc = z - mu
    var = jnp.mean(zc * zc, axis=-1, keepdims=True)
    return zc * lax.rsqrt(var + LN_EPS) * g + b


def _rms_norm(h, w):
    return h * lax.rsqrt(jnp.mean(h * h, axis=-1, keepdims=True) + RMS_EPS) * w


def _const_spec(shape):
    nd = len(shape)
    return pl.BlockSpec(shape, lambda *_: (0,) * nd)


def _mod_spec(sub, part, tiles_per_batch, d):
    return pl.BlockSpec((None, None, None, 1, d), lambda i, *_: (sub, part, i // tiles_per_batch, 0, 0))


def _mod_kernel(c_ref, w_ref, b_ref, o_ref):
    c = c_ref[...]
    cond = c * _sigmoid(c)
    w = w_ref[...]
    a_hi = cond.astype(BF16)
    a_lo = (cond - a_hi.astype(F32)).astype(BF16)
    w_hi = w.astype(BF16)
    w_lo = (w - w_hi.astype(F32)).astype(BF16)
    acc = _dot(a_hi, w_hi) + (_dot(a_hi, w_lo) + _dot(a_lo, w_hi))
    o_ref[...] = acc + b_ref[...]


def _modulation(c, ada_w, ada_b):
    depth, _, d, d3 = ada_w.shape
    bsz = c.shape[0]
    nsub = depth * 2
    w = ada_w.reshape(nsub, d, d3)
    b = ada_b.reshape(nsub, 1, d3)
    out = pl.pallas_call(
        _mod_kernel,
        grid=(nsub, 3),
        in_specs=[
            pl.BlockSpec((bsz, d), lambda l, j: (0, 0)),
            pl.BlockSpec((None, d, d), lambda l, j: (l, 0, j)),
            pl.BlockSpec((None, 1, d), lambda l, j: (l, 0, j)),
        ],
        out_specs=pl.BlockSpec((None, None, bsz, d), lambda l, j: (l, j, 0, 0)),
        out_shape=jax.ShapeDtypeStruct((nsub, 3, bsz, d), F32),
        compiler_params=_params("arbitrary", "arbitrary"),
        name="adaln_mod",
    )(c, w, b)
    return out.reshape(nsub, 3, bsz, 1, d)


def _rope_kernel(pos_ref, inv_ref, ecos_ref, esin_ref, base_ref, cos_ref, sin_ref):
    ang = inv_ref[...] * pos_ref[...].astype(F32)
    cos_ref[...] = _spread(jnp.cos(ang), ecos_ref[...]) + base_ref[...]
    sin_ref[...] = _spread(jnp.sin(ang), esin_ref[...])


def _spread(vals_t, e):
    hi, mid, lo = _split3(vals_t.T)
    return _dot(hi, e) + (_dot(mid, e) + _dot(lo, e))


def _rope_tables(positions):
    t = positions.size
    rot = B_HEAD_DIM // ROT_FRAC
    half = rot // 2
    rows = 16
    inv = jnp.power(ROPE_THETA, -jnp.arange(half, dtype=F32) * 2.0 / rot)
    inv_c = jnp.zeros((rows, 1), F32).at[:half, 0].set(inv)
    lane = jnp.arange(LANES) % B_HEAD_DIM
    hit = (jnp.arange(rows)[:, None] == (lane % half)[None, :]) & (lane < rot)[None, :]
    e_cos = hit.astype(BF16)
    e_sin = (hit * jnp.where(lane < half, -1.0, 1.0)[None, :]).astype(BF16)
    base = (lane >= rot).astype(F32).reshape(1, LANES)
    tm = min(TOKEN_TILE, t)
    return pl.pallas_call(
        _rope_kernel,
        grid=(t // tm,),
        in_specs=[pl.BlockSpec((1, tm), lambda i: (0, i)), _const_spec((rows, 1)), _const_spec((rows, LANES)),
                  _const_spec((rows, LANES)), _const_spec((1, LANES))],
        out_specs=[pl.BlockSpec((tm, LANES), lambda i: (i, 0))] * 2,
        out_shape=[jax.ShapeDtypeStruct((t, LANES), F32)] * 2,
        compiler_params=_params("arbitrary"),
        name="rope_tables",
    )(positions.reshape(1, t), inv_c, e_cos, e_sin, base)


def _rope_slab(slab, cos, sin_signed, lo_mask):
    fwd = pltpu.roll(slab, LANES - 8, axis=1)
    bwd = pltpu.roll(slab, 8, axis=1)
    return slab * cos + jnp.where(lo_mask, fwd, bwd) * sin_signed


def _rope_lo_mask(tm):
    lane = lax.broadcasted_iota(jnp.int32, (tm, LANES), 1) % B_HEAD_DIM
    return lane < (B_HEAD_DIM // ROT_FRAC // 2)


def _modulated(x_ref, sh_ref, sc_ref):
    return (x_ref[...] * (1.0 + sc_ref[...]) + sh_ref[...]).astype(BF16)


def _mlstm_in_kernel(x_ref, sh_ref, sc_ref, w_ref, bg_ref, q_ref, k_ref, v_ref, o_ref, g_ref, *, qk, vw):
    h = _modulated(x_ref, sh_ref, sc_ref)
    dk = qk // A_HEADS
    q_ref[...] = _dot(h, w_ref[:, :qk]).astype(BF16)
    k_ref[...] = (_dot(h, w_ref[:, qk:2 * qk]) * (dk ** -0.5)).astype(BF16)
    v_ref[...] = _dot(h, w_ref[:, 2 * qk:2 * qk + vw]).astype(BF16)
    o_ref[...] = _sigmoid(_dot(h, w_ref[:, 2 * qk + vw:2 * qk + 2 * vw]))
    g_ref[...] = _dot(h, w_ref[:, 2 * qk + 2 * vw:]) + bg_ref[...]


def _attn_in_body(x_ref, sh_ref, sc_ref, w_ref, wvt_ref, cos_ref, sin_ref, q_ref, k_ref, v_ref, *,
                  qw, kw, scale, kdup=False):
    h = _modulated(x_ref, sh_ref, sc_ref)
    cos = cos_ref[...]
    sin = sin_ref[...]
    lo = _rope_lo_mask(h.shape[0])
    vw = 0 if wvt_ref is not None else w_ref.shape[1] - qw - kw
    first_head = lax.broadcasted_iota(jnp.int32, (h.shape[0], LANES), 1) < B_HEAD_DIM

    def emit(col, slab):
        if col < qw:
            q_ref[:, col:col + LANES] = (_rope_slab(slab, cos, sin, lo) * scale).astype(BF16)
        elif col < qw + kw and kdup:
            r = _rope_slab(slab, cos, sin, lo)
            sw = pltpu.roll(r, B_HEAD_DIM, axis=1)
            c = 2 * (col - qw)
            k_ref[:, c:c + LANES] = jnp.where(first_head, r, sw).astype(BF16)
            k_ref[:, c + LANES:c + 2 * LANES] = jnp.where(first_head, sw, r).astype(BF16)
        elif col < qw + kw:
            k_ref[:, col - qw:col - qw + LANES] = _rope_slab(slab, cos, sin, lo).astype(BF16)
        else:
            v_ref[:, col - qw - kw:col - qw - kw + LANES] = slab.astype(BF16)

    total = qw + kw + vw
    for c0 in range(0, total, MXU_COLS):
        width = min(MXU_COLS, total - c0)
        wide = _dot(h, w_ref[:, c0:c0 + width])
        for off in range(0, width, LANES):
            emit(c0 + off, wide[:, off:off + LANES])
    if wvt_ref is not None:
        v_ref[...] = _dot_nt(wvt_ref[...], h).astype(BF16)


def _attn_in_vt_kernel(x_ref, sh_ref, sc_ref, w_ref, wvt_ref, cos_ref, sin_ref, q_ref, k_ref, v_ref, **kw):
    _attn_in_body(x_ref, sh_ref, sc_ref, w_ref, wvt_ref, cos_ref, sin_ref, q_ref, k_ref, v_ref, **kw)


def _hgrn_in_kernel(x_ref, sh_ref, sc_ref, w_ref, q_ref, f_ref, v_ref, g_ref, *, kw, vw):
    h = _modulated(x_ref, sh_ref, sc_ref)
    q_ref[...] = _dot(h, w_ref[:, :kw])
    f_ref[...] = _dot(h, w_ref[:, kw:2 * kw])
    v_ref[...] = _dot(h, w_ref[:, 2 * kw:2 * kw + vw]).astype(BF16)
    g = _dot(h, w_ref[:, 2 * kw + vw:])
    g_ref[...] = g * _sigmoid(g)


def _in_proj(body, x, mod, sub, seq, w, extra, extra_specs, outs):
    t, d = x.shape
    tm = min(TOKEN_TILE, seq)
    tpb = seq // tm
    row = lambda width: pl.BlockSpec((tm, width), lambda i: (i, 0))
    col = lambda width: pl.BlockSpec((width, tm), lambda i: (0, i))
    return pl.pallas_call(
        body,
        grid=(t // tm,),
        in_specs=[row(d), _mod_spec(sub, 0, tpb, d), _mod_spec(sub, 1, tpb, d), _const_spec(w.shape)] + extra_specs,
        out_specs=[col(wd) if tr else row(wd) for wd, _, tr in outs],
        out_shape=[jax.ShapeDtypeStruct((wd, t) if tr else (t, wd), dt) for wd, dt, tr in outs],
        compiler_params=_params("arbitrary"),
        name=body.func.__name__.strip("_") if isinstance(body, functools.partial) else body.__name__.strip("_"),
    )(x, mod, mod, w, *extra)


def _tail_kernel(y_ref, wp_ref, x_ref, gate0_ref, g0_ref, b0_ref, sh_ref, sc_ref, gate1_ref, wi_ref, wo_ref,
                 g1_ref, b1_ref, o_ref, *, hidden, chunk, alpha):
    hm = x_ref.shape[0] // TAIL_PARTS
    rows = [slice(p * hm, (p + 1) * hm) for p in range(TAIL_PARTS)]
    nchunk = hidden // chunk

    def norm1(i, y):
        return _layer_norm(alpha * x_ref[rows[i], :] + (1.0 + gate0_ref[...]) * y, g0_ref[...], b0_ref[...])

    def ffn_part(h, j):
        a = _dot(h, wi_ref[:, j * chunk:(j + 1) * chunk])
        u = _dot(h, wi_ref[:, hidden + j * chunk:hidden + (j + 1) * chunk])
        act = (a * _sigmoid(a) * u).astype(BF16)
        return _dot(act, wo_ref[j * chunk:(j + 1) * chunk, :])

    def norm2(i, x, acc):
        o_ref[rows[i], :] = _layer_norm(alpha * x + (1.0 + gate1_ref[...]) * acc, g1_ref[...], b1_ref[...])

    ys, xs, hs, accs = {}, {}, {}, {}
    for i in range(TAIL_PARTS + 2):
        if i < TAIL_PARTS:
            ys[i] = _dot(y_ref[rows[i], :], wp_ref[...])
        if 0 <= i - 1 < TAIL_PARTS:
            accs[i - 1] = ffn_part(hs[i - 1], 0)
        if i < TAIL_PARTS:
            xs[i] = norm1(i, ys[i])
            hs[i] = (xs[i] * (1.0 + sc_ref[...]) + sh_ref[...]).astype(BF16)
        if 0 <= i - 1 < TAIL_PARTS:
            for j in range(1, nchunk):
                accs[i - 1] = accs[i - 1] + ffn_part(hs[i - 1], j)
        if 0 <= i - 2 < TAIL_PARTS:
            norm2(i - 2, xs[i - 2], accs[i - 2])


def _ffn_chunk(hidden):
    best = None
    for c in range(MXU_COLS, min(hidden, FFN_MAX_CHUNK) + 1, MXU_COLS):
        if hidden % c == 0:
            best = c
    assert best is not None, hidden
    return best


def _resident_spec(shape):
    nd = len(shape)
    return pl.BlockSpec(shape, lambda *_: (0,) * nd, pipeline_mode=pl.Buffered(1))


def _layer_tail(y, w_proj, x, mod, sub, seq, w_in, w_out, ln_g, ln_b, alpha):
    t, d = x.shape
    hidden = w_out.shape[0]
    tm = min(TAIL_TILE, seq)
    tpb = seq // tm
    row = lambda width: pl.BlockSpec((tm, width), lambda i: (i, 0))
    vec = _const_spec((1, d))
    return pl.pallas_call(
        functools.partial(_tail_kernel, hidden=hidden, chunk=_ffn_chunk(hidden), alpha=alpha),
        grid=(t // tm,),
        in_specs=[row(y.shape[1]), _resident_spec(w_proj.shape), row(d), _mod_spec(sub, 2, tpb, d), vec, vec,
                  _mod_spec(sub + 1, 0, tpb, d), _mod_spec(sub + 1, 1, tpb, d), _mod_spec(sub + 1, 2, tpb, d),
                  _resident_spec(w_in.shape), _resident_spec(w_out.shape), vec, vec],
        out_specs=row(d),
        out_shape=jax.ShapeDtypeStruct((t, d), F32),
        compiler_params=_params("arbitrary"),
        name="out_proj_ffn",
    )(y, w_proj, x, mod, ln_g[0].reshape(1, d), ln_b[0].reshape(1, d), mod, mod, mod, w_in, w_out,
      ln_g[1].reshape(1, d), ln_b[1].reshape(1, d))


def _mlstm_kernel(q_ref, k_ref, v_ref, o_ref, g_ref, nw_ref, y_ref, c_ref, n_ref, m_ref, *, chunk, dk, dv):
    L = chunk
    nseq = q_ref.shape[0]
    nchunks = q_ref.shape[1] // L

    @pl.when(pl.program_id(1) == 0)
    def _():
        c_ref[...] = jnp.zeros_like(c_ref)
        n_ref[...] = jnp.zeros_like(n_ref)
        m_ref[...] = jnp.zeros_like(m_ref)

    row_i = lax.broadcasted_iota(jnp.int32, (L, L), 0)
    col_i = lax.broadcasted_iota(jnp.int32, (L, L), 1)
    causal = col_i <= row_i
    tril = jnp.where(causal, 1.0, 0.0).astype(BF16)
    lane = lax.broadcasted_iota(jnp.int32, (L, LANES), 1)

    def body(c, carry):
        rows = pl.ds(pl.multiple_of(c * L, L), L)
        for u in range(nseq):
            g = g_ref[u, rows, :]
            logf = jnp.where(lane >= A_HEADS, _log_sigmoid(g), 0.0)
            bcum = _dot_exact_lhs01(tril, logf)
            z = jnp.where(lane < A_HEADS, g, bcum)
            zt = z.T
            for h in range(A_HEADS):
                s = u * A_HEADS + h
                i_col = z[:, h:h + 1]
                b_col = z[:, A_HEADS + h:A_HEADS + h + 1]
                i_row = zt[h:h + 1, :]
                b_row = zt[A_HEADS + h:A_HEADS + h + 1, :]
                m_prev = m_ref[s, 0:1, 0:1]
                dmat = jnp.where(causal, b_col - b_row + i_row, -jnp.inf)
                m_inter = b_col + m_prev
                m_t = jnp.maximum(m_inter, jnp.max(dmat, axis=-1, keepdims=True))
                qh = q_ref[u, rows, h * dk:(h + 1) * dk]
                kh = k_ref[u, rows, h * dk:(h + 1) * dk]
                vh = v_ref[u, rows, h * dv:(h + 1) * dv]
                a = _dot_nt(qh, kh) * jnp.exp(dmat - m_t)
                w_inter = jnp.exp(m_inter - m_t)
                c_prev = c_ref[s]
                n_prev = n_ref[s, 0:1, :]
                num = _dot(a.astype(BF16), vh) + w_inter * _dot(qh, c_prev.astype(BF16))
                qn = jnp.sum(qh.astype(F32) * n_prev, axis=-1, keepdims=True)
                den = jnp.sum(a, axis=-1, keepdims=True) + w_inter * qn
                hout = num / jnp.maximum(jnp.abs(den), jnp.exp(-m_t))
                y = _rms_norm(hout, nw_ref[:, h * dv:(h + 1) * dv]) * o_ref[u, rows, h * dv:(h + 1) * dv]
                y_ref[u, rows, h * dv:(h + 1) * dv] = y.astype(y_ref.dtype)
                b_last = b_col[L - 1:L, :]
                g_col = b_last - b_col + i_col
                m_new = jnp.maximum(b_last + m_prev, jnp.max(g_col, axis=0, keepdims=True))
                decay = jnp.exp(b_last + m_prev - m_new)
                kws = kh.astype(F32) * jnp.exp(g_col - m_new)
                c_ref[s] = decay * c_prev + _dot(kws.T.astype(BF16), vh)
                n_ref[s, 0:1, :] = decay * n_prev + jnp.sum(kws, axis=0, keepdims=True)
                m_ref[s] = jnp.broadcast_to(m_new, m_ref.shape[1:])
        return carry

    lax.fori_loop(0, nchunks, body, 0)


def _mlstm(q, k, v, o, g, norm_w, bsz, seq):
    t = q.shape[0]
    qk, vw = q.shape[1], v.shape[1]
    dk, dv = qk // A_HEADS, vw // A_HEADS
    ts = min(MLSTM_STEP, seq)
    nseq = math.gcd(MLSTM_SEQS, bsz)
    blk = lambda width: pl.BlockSpec((nseq, ts, width), lambda b, s: (b, s, 0))
    per_seq = lambda a: a.reshape(bsz, seq, a.shape[1])
    y = pl.pallas_call(
        functools.partial(_mlstm_kernel, chunk=min(MLSTM_CHUNK, ts), dk=dk, dv=dv),
        grid=(bsz // nseq, seq // ts),
        in_specs=[blk(qk), blk(qk), blk(vw), blk(vw), blk(LANES), _const_spec((1, vw))],
        out_specs=blk(vw),
        out_shape=jax.ShapeDtypeStruct((bsz, seq, vw), BF16),
        scratch_shapes=[pltpu.VMEM((nseq * A_HEADS, dk, dv), F32), pltpu.VMEM((nseq * A_HEADS, 8, dk), F32),
                        pltpu.VMEM((nseq * A_HEADS, 8, LANES), F32)],
        compiler_params=_params("arbitrary", "arbitrary"),
        name="mlstm_mixer",
    )(per_seq(q), per_seq(k), per_seq(v), per_seq(o), per_seq(g), norm_w.reshape(1, vw))
    return y.reshape(t, vw)


def _swa_kernel(sink_ref, q_ref, kc_ref, kp_ref, vtc_ref, vtp_ref, o_ref, *, heads, hd, group):
    n = pl.program_id(1)
    blk = q_ref.shape[0]
    hpd = SWA_HEADS_PER_DOT
    key_i = lax.broadcasted_iota(jnp.int32, (2 * blk, blk), 0)
    qry_i = lax.broadcasted_iota(jnp.int32, (2 * blk, blk), 1)
    valid = (key_i > qry_i) & (key_i <= qry_i + blk) & ((key_i >= blk) | (n > 0))
    bias = jnp.concatenate([jnp.where(valid, 0.0, -jnp.inf)] * hpd, axis=1)
    upper = lax.broadcasted_iota(jnp.int32, (blk, LANES), 1) >= hd
    ones = jnp.ones((SWA_ONES_ROWS, 2 * blk), BF16)

    def scores(j0):
        gi = j0 // group
        ks = slice(gi * LANES, (gi + 1) * LANES)
        k2 = jnp.concatenate([kp_ref[:, ks], kc_ref[:, ks]], axis=0)
        qz = []
        for j in range(j0, j0 + hpd):
            slab = q_ref[:, (j // 2) * LANES:(j // 2 + 1) * LANES]
            own = upper if j % 2 else jnp.logical_not(upper)
            qz.append(jnp.where(own, slab, jnp.zeros_like(slab)))
        return _dot_nt(k2, jnp.concatenate(qz, axis=0))

    sts = [scores(j0) for j0 in range(0, heads, hpd)]
    for idx, j0 in enumerate(range(0, heads, hpd)):
        gi = j0 // group
        st = sts[idx] + bias
        off = (j0 % group) * blk
        sink = sink_ref[gi:gi + 1, off:off + hpd * blk] * LOG2_E
        m = jnp.maximum(jnp.max(st, axis=0, keepdims=True), sink)
        p = jnp.exp2(st - m).astype(BF16)
        vs = slice(gi * hd, (gi + 1) * hd)
        vte = jnp.concatenate([jnp.concatenate([vtp_ref[vs, :], vtc_ref[vs, :]], axis=1), ones], axis=0)
        acc = _dot(vte, p)
        ot = acc[:hd, :] * (1.0 / (acc[hd:hd + 1, :] + jnp.exp2(sink - m)))
        for pr in range(hpd // 2):
            pair = jnp.concatenate([ot[:, (2 * pr) * blk:(2 * pr + 1) * blk],
                                    ot[:, (2 * pr + 1) * blk:(2 * pr + 2) * blk]], axis=0)
            c0 = (j0 // 2 + pr) * LANES
            o_ref[:, c0:c0 + LANES] = pair.T.astype(o_ref.dtype)


def _swa(q, k2, vt, sinks, bsz, seq):
    t, qw = q.shape
    kw = vt.shape[0]
    heads = qw // B_HEAD_DIM
    blk = min(B_BLOCK, seq)
    nb = seq // blk
    sink_rows = jnp.repeat(sinks.astype(F32).reshape(heads // B_GROUP, B_GROUP), blk, axis=1)
    cur = lambda width: pl.BlockSpec((blk, width), lambda b, n: (b * nb + n, 0))
    prev = lambda width: pl.BlockSpec((blk, width), lambda b, n: (b * nb + jnp.maximum(n - 1, 0), 0))
    cur_t = pl.BlockSpec((kw, blk), lambda b, n: (0, b * nb + n))
    prev_t = pl.BlockSpec((kw, blk), lambda b, n: (0, b * nb + jnp.maximum(n - 1, 0)))
    return pl.pallas_call(
        functools.partial(_swa_kernel, heads=heads, hd=B_HEAD_DIM, group=B_GROUP),
        grid=(bsz, nb),
        in_specs=[_const_spec(sink_rows.shape), cur(qw), cur(2 * kw), prev(2 * kw), cur_t, prev_t],
        out_specs=cur(qw),
        out_shape=jax.ShapeDtypeStruct((t, qw), BF16),
        compiler_params=_params("arbitrary", "arbitrary"),
        name="swa_mixer",
    )(sink_rows, q, k2, k2, vt, vt)


def _hgrn_kernel(q_ref, f_ref, v_ref, g_ref, lbp_ref, nw_ref, y_ref, st_ref, cf_ref, ck_ref,
                 cfc_ref, ckc_ref, vsc_ref, *,
                 chunk, dk, dv, layer):
    L = chunk
    nchunks = q_ref.shape[0] // L
    heads = q_ref.shape[1] // dk

    @pl.when(pl.program_id(1) == 0)
    def _():
        st_ref[...] = jnp.zeros_like(st_ref)

    lbp = lbp_ref[...]
    e = jnp.exp(lbp - jnp.max(lbp, axis=0, keepdims=True))
    sm = e / jnp.sum(e, axis=0, keepdims=True)
    lb = jnp.zeros_like(sm[0:1])
    for r in range(1, layer + 1):
        lb = lb + sm[r:r + 1]
    log_lb = jnp.log(lb)
    log_1m_lb = jnp.log1p(-lb)

    ones = jnp.ones((dk, LANES), BF16)
    half = L // 2
    trow = lax.broadcasted_iota(jnp.int32, (half, dv), 0)

    rb = min(HGRN_CUMSUM_ROWS, q_ref.shape[0])
    row_i = lax.broadcasted_iota(jnp.int32, (rb, rb), 0)
    col_i = lax.broadcasted_iota(jnp.int32, (rb, rb), 1)
    tril = jnp.where((col_i <= row_i) & (col_i // L == row_i // L), 1.0, 0.0).astype(BF16)
    for r0 in range(0, q_ref.shape[0], rb):
        fp = f_ref[r0:r0 + rb, :]
        ls = jnp.minimum(fp, 0.0) - jnp.log(1.0 + jnp.exp(-jnp.abs(fp)))
        x2 = log_1m_lb + ls
        logf = jnp.maximum(log_lb, x2) + jnp.log(1.0 + jnp.exp(-jnp.abs(log_lb - x2)))
        cf = _dot_exact_lhs01(tril, logf) * LOG2_E
        cf_ref[r0:r0 + rb, :] = cf
        ck_ref[r0:r0 + rb, :] = cf - (x2 - fp) * LOG2_E

    unroll = cfc_ref.shape[0]

    def body(c, carry):
        rows = [pl.ds(pl.multiple_of((c * unroll + u) * L, L), L) for u in range(unroll)]
        for u in range(unroll):
            cfc_ref[u] = cf_ref[rows[u], :]
            ckc_ref[u] = ck_ref[rows[u], :]
            vsc_ref[u] = v_ref[rows[u], :].astype(F32)
        def issue(h, u):
            cs = slice(h * dk, (h + 1) * dk)
            vs = slice(h * dv, (h + 1) * dv)
            cfh = cfc_ref[u, :, cs]
            qh = q_ref[rows[u], cs]
            ws = []
            for s in range(L):
                t0 = 0 if s < half else half
                ws.append(qh[t0:] * jnp.exp2(cfh[t0:] - ckc_ref[u, s:s + 1, cs]))
            w_all = jnp.concatenate(ws, axis=0).astype(BF16)
            r_all = _dot(w_all, ones)
            st = st_ref[h]
            inter = _dot_nt((qh * jnp.exp2(cfh)).astype(BF16), st.astype(BF16))
            last = cfc_ref[u, L - 1:L, cs]
            kdec = jnp.exp2(last - ckc_ref[u, :, cs]).astype(BF16)
            st_ref[h] = jnp.exp2(last) * st + _dot(vsc_ref[u, :, vs].T.astype(BF16), kdec)
            return r_all, inter

        def finish(h, u, r_all, o):
            vs = slice(h * dv, (h + 1) * dv)
            o_lo, o_hi = o[:half], o[half:]
            for s in range(half):
                r = r_all[s * L:(s + 1) * L, :]
                v_s = vsc_ref[u, s:s + 1, vs]
                o_lo = o_lo + jnp.where(trow >= s, r[:half], 0.0) * v_s
                o_hi = o_hi + r[half:] * v_s
            for s in range(half, L):
                r = r_all[half * L + (s - half) * half:half * L + (s - half + 1) * half, :]
                o_hi = o_hi + jnp.where(trow >= s - half, r, 0.0) * vsc_ref[u, s:s + 1, vs]
            o = jnp.concatenate([o_lo, o_hi], axis=0)
            y = _rms_norm(o, nw_ref[:, vs]) * g_ref[rows[u], vs]
            y_ref[rows[u], vs] = y.astype(y_ref.dtype)

        work = [(h, u) for u in range(unroll) for h in range(heads)]
        pending = []
        for hu in work:
            pending.append((hu, issue(*hu)))
            if len(pending) > HGRN_SKEW:
                (h0, u0), (r0, i0) = pending.pop(0)
                finish(h0, u0, r0, i0)
        for (h0, u0), (r0, i0) in pending:
            finish(h0, u0, r0, i0)
        return carry

    lax.fori_loop(0, nchunks // unroll, body, 0)


def _hgrn(q, f, v, g, lower_bounds, layer, norm_w, bsz, seq):
    t, kw = q.shape
    vw = v.shape[1]
    dk, dv = kw // C_HEADS, vw // C_HEADS
    ts = min(HGRN_STEP, seq)
    spb = seq // ts
    chunk = min(HGRN_CHUNK, ts)
    unroll = math.gcd(HGRN_UNROLL, ts // chunk)
    row = lambda width: pl.BlockSpec((ts, width), lambda b, s: (b * spb + s, 0))
    return pl.pallas_call(
        functools.partial(_hgrn_kernel, chunk=chunk, dk=dk, dv=dv, layer=layer),
        grid=(bsz, spb),
        in_specs=[row(kw), row(kw), row(vw), row(vw), _const_spec(lower_bounds.shape), _const_spec((1, vw))],
        out_specs=row(vw),
        out_shape=jax.ShapeDtypeStruct((t, vw), BF16),
        scratch_shapes=[pltpu.VMEM((C_HEADS, dv, dk), F32), pltpu.VMEM((ts, kw), F32), pltpu.VMEM((ts, kw), F32),
                        pltpu.VMEM((unroll, chunk, kw), F32), pltpu.VMEM((unroll, chunk, kw), F32),
                        pltpu.VMEM((unroll, chunk, vw), F32)],
        compiler_params=_params("arbitrary", "arbitrary"),
        name="hgrn2_mixer",
    )(q, f, v, g, lower_bounds, norm_w.reshape(1, vw))


def _diff_kernel(q_ref, k_ref, vt_ref, lam_ref, nw_ref, o_ref, acc_ref, st_ref, *, blk, hd, lam_init):
    qi = pl.program_id(2)
    lv = lam_ref[...]
    lam = (jnp.exp(jnp.sum(lv[0:1] * lv[1:2], axis=-1, keepdims=True))
           - jnp.exp(jnp.sum(lv[2:3] * lv[3:4], axis=-1, keepdims=True)) + lam_init)
    q = q_ref[...]
    lane = lax.broadcasted_iota(jnp.int32, q.shape, 1)
    zero = jnp.zeros_like(q)
    qz = (jnp.where(lane < hd, q, zero), jnp.where(lane >= hd, q, zero))
    acc_ref[...] = jnp.zeros_like(acc_ref)

    hw = 2 * hd
    ones = jnp.ones((DIFF_ONES_ROWS, blk), BF16)

    def scores_into(slot, j):
        kb = k_ref[pl.ds(pl.multiple_of(j * blk, blk), blk), :]
        for c in range(2):
            st_ref[2 * slot + c] = _dot_nt(kb, qz[c])

    def step(j, j_next, m, masked, src):
        if j_next is not None:
            scores_into(1 - src, j_next)
        vte = jnp.concatenate([vt_ref[:, pl.ds(pl.multiple_of(j * blk, blk), blk)], ones], axis=0)
        if masked:
            key_i = lax.broadcasted_iota(jnp.int32, (blk, blk), 0)
            qry_i = lax.broadcasted_iota(jnp.int32, (blk, blk), 1)
            keep = key_i <= qry_i
        m_out = []
        for c in range(2):
            st = st_ref[2 * src + c]
            if masked:
                st = jnp.where(keep, st, -jnp.inf)
            m_new = jnp.maximum(m[c], jnp.max(st, axis=0, keepdims=True))
            alpha = jnp.exp2(m[c] - m_new)
            p = jnp.exp2(st - m_new).astype(BF16)
            acc_ref[c] = alpha * acc_ref[c] + _dot(vte, p)
            m_out.append(m_new)
        return tuple(m_out)

    scores_into(0, qi)
    neg = jnp.full((1, blk), -jnp.inf, F32)
    m = step(qi, 0, (neg, neg), True, 0)

    def pair(i, cr):
        cr = step(2 * i, 2 * i + 1, cr, False, 1)
        return step(2 * i + 1, 2 * i + 2, cr, False, 0)

    m = lax.fori_loop(0, qi // 2, pair, m)

    @pl.when(qi % 2 == 1)
    def _():
        step(qi - 1, None, m, False, 1)

    norm = [acc_ref[c, :hw, :] * (1.0 / acc_ref[c, hw:hw + 1, :]) for c in range(2)]
    out_t = norm[0] - lam * norm[1]
    o_ref[...] = (_rms_norm(out_t.T, nw_ref[...]) * (1.0 - lam_init)).astype(o_ref.dtype)


def _diff_attn(q, k, vt, lam_vec, norm_w, lam_init, bsz, seq):
    t, w = q.shape
    hw = 2 * D_HEAD_DIM
    heads = w // hw
    blk = min(DIFF_BLOCK, seq)
    nq = seq // blk
    return pl.pallas_call(
        functools.partial(_diff_kernel, blk=blk, hd=D_HEAD_DIM, lam_init=lam_init),
        grid=(bsz, heads, nq),
        in_specs=[pl.BlockSpec((blk, hw), lambda b, h, i: (b * nq + i, h)),
                  pl.BlockSpec((seq, hw), lambda b, h, i: (b, h)),
                  pl.BlockSpec((hw, seq), lambda b, h, i: (h, b)),
                  _const_spec(lam_vec.shape), _const_spec((1, hw))],
        out_specs=pl.BlockSpec((blk, hw), lambda b, h, i: (b * nq + i, h)),
        out_shape=jax.ShapeDtypeStruct((t, w), BF16),
        scratch_shapes=[pltpu.VMEM((2, hw + DIFF_ONES_ROWS, blk), F32), pltpu.VMEM((4, blk, blk), F32)],
        compiler_params=_params("arbitrary", "arbitrary", "arbitrary"),
        name="diff_attn_mixer",
    )(q, k, vt, lam_vec, norm_w.reshape(1, hw))


def _pad_cols(w, width):
    return jnp.pad(w, ((0, 0), (0, width - w.shape[1])))


def kernel(x, c, positions, ada_w, ada_b, ln_g, ln_b, mlstm_w_in, mlstm_b_gate, mlstm_norm, mlstm_w_out, swa_w_in, swa_sinks, swa_w_out, hgrn_w_in, hgrn_lower_bounds, hgrn_norm, hgrn_w_out, diff_w_in, diff_lambda, diff_norm, diff_w_out, ffn_w_in, ffn_w_out):
    bsz, seq, d = x.shape
    depth = ada_w.shape[0]
    alpha = (2 * depth) ** 0.25
    t = bsz * seq
    xf = x.reshape(t, d)
    mod = _modulation(c, ada_w, ada_b)
    cos_t, sin_t = _rope_tables(positions)
    for i in range(depth):
        mixer, j = i % 4, i // 4
        sub = 2 * i
        if mixer == 0:
            vw = mlstm_w_out.shape[1]
            qk = vw // 2
            n_gate = 2 * A_HEADS
            w = _pad_cols(mlstm_w_in[j], 2 * qk + 2 * vw + LANES).astype(BF16)
            bg = _pad_cols(mlstm_b_gate[j].reshape(1, n_gate), LANES)
            q, k, v, o, g = _in_proj(
                functools.partial(_mlstm_in_kernel, qk=qk, vw=vw), xf, mod, sub, seq, w, [bg],
                [_const_spec((1, LANES))],
                [(qk, BF16, False), (qk, BF16, False), (vw, BF16, False), (vw, F32, False), (LANES, F32, False)])
            y = _mlstm(q, k, v, o, g, mlstm_norm[j], bsz, seq)
            w_out = mlstm_w_out[j]
        elif mixer == 1:
            qw = swa_w_out.shape[1]
            kw = qw // B_GROUP
            w_qk = swa_w_in[j][:, :qw + kw].astype(BF16)
            w_vt = swa_w_in[j][:, qw + kw:].T.astype(BF16)
            q, k2, vt = _in_proj(
                functools.partial(_attn_in_vt_kernel, qw=qw, kw=kw, scale=B_HEAD_DIM ** -0.5 * LOG2_E, kdup=True),
                xf, mod, sub, seq, w_qk, [w_vt, cos_t, sin_t],
                [_const_spec(w_vt.shape)] + [pl.BlockSpec((min(TOKEN_TILE, seq), LANES), lambda i: (i, 0))] * 2,
                [(qw, BF16, False), (2 * kw, BF16, False), (kw, BF16, True)])
            y = _swa(q, k2, vt, swa_sinks[j], bsz, seq)
            w_out = swa_w_out[j]
        elif mixer == 2:
            vw = hgrn_w_out.shape[1]
            q, f, v, g = _in_proj(
                functools.partial(_hgrn_in_kernel, kw=vw, vw=vw), xf, mod, sub, seq,
                hgrn_w_in[j].astype(BF16), [], [],
                [(vw, F32, False), (vw, F32, False), (vw, BF16, False), (vw, F32, False)])
            y = _hgrn(q, f, v, g, hgrn_lower_bounds.astype(F32), i, hgrn_norm[j], bsz, seq)
            w_out = hgrn_w_out[j]
        else:
            w_ = diff_w_out.shape[1]
            lam_init = 0.8 - 0.6 * math.exp(-0.3 * i)
            w_qk = diff_w_in[j][:, :2 * w_].astype(BF16)
            w_vt = diff_w_in[j][:, 2 * w_:].T.astype(BF16)
            q, k, vt = _in_proj(
                functools.partial(_attn_in_vt_kernel, qw=w_, kw=w_, scale=D_HEAD_DIM ** -0.5 * math.log2(math.e)),
                xf, mod, sub, seq,
                w_qk, [w_vt, cos_t, sin_t],
                [_const_spec(w_vt.shape)] + [pl.BlockSpec((min(TOKEN_TILE, seq), LANES), lambda i: (i, 0))] * 2,
                [(w_, BF16, False), (w_, BF16, False), (w_, BF16, True)])
            y = _diff_attn(q, k, vt, diff_lambda[j].astype(F32), diff_norm[j], lam_init, bsz, seq)
            w_out = diff_w_out[j]
        xf = _layer_tail(y, w_out.astype(BF16), xf, mod, sub, seq, ffn_w_in[i].astype(BF16),
                         ffn_w_out[i].astype(BF16), ln_g[i], ln_b[i], alpha)
    return xf.reshape(bsz, seq, d)
```

```python
import functools
import math

import jax
import jax.numpy as jnp
from jax import lax
from jax.experimental import pallas as pl
from jax.experimental.pallas import tpu as pltpu

F32 = jnp.float32
BF16 = jnp.bfloat16

A_HEADS = 4
B_HEAD_DIM = 64
B_GROUP = 8
B_BLOCK = 128
C_HEADS = 8
D_HEAD_DIM = 64
ROPE_THETA = 500000.0
ROT_FRAC = 4
LN_EPS = 1e-5
RMS_EPS = 1e-6
LANES = 128
MXU_COLS = 256
LOG2_E = math.log2(math.e)
VMEM_LIMIT = 52 * 2 ** 20

TOKEN_TILE = 512
TAIL_TILE = 512
TAIL_PARTS = 2
FFN_MAX_CHUNK = 4096
MLSTM_CHUNK = 128
MLSTM_STEP = 512
MLSTM_SEQS = 1
HGRN_CHUNK = 16
HGRN_STEP = 512
HGRN_CUMSUM_ROWS = 128
HGRN_UNROLL = 8
HGRN_SKEW = 1
DIFF_BLOCK = 512
DIFF_ONES_ROWS = 16
SWA_ONES_ROWS = 16
SWA_HEADS_PER_DOT = 8


def _params(*sem):
    return pltpu.CompilerParams(dimension_semantics=sem, vmem_limit_bytes=VMEM_LIMIT)


def _dot(a, b):
    return jnp.dot(a, b, preferred_element_type=F32)


def _dot_nt(a, b):
    return lax.dot_general(a, b, (((1,), (1,)), ((), ())), preferred_element_type=F32)


def _split3(x):
    hi = x.astype(BF16)
    r = x - hi.astype(F32)
    mid = r.astype(BF16)
    lo = (r - mid.astype(F32)).astype(BF16)
    return hi, mid, lo


def _dot_exact_lhs01(mask_bf16, x):
    hi, mid, lo = _split3(x)
    return _dot(mask_bf16, hi) + (_dot(mask_bf16, mid) + _dot(mask_bf16, lo))


def _sigmoid(x):
    return 1.0 / (1.0 + jnp.exp(-x))


def _log_sigmoid(x):
    return jnp.minimum(x, 0.0) - jnp.log1p(jnp.exp(-jnp.abs(x)))


def _layer_norm(z, g, b):
    mu = jnp.mean(z, axis=-1, keepdims=True)
    zc = z - mu
    var = jnp.mean(zc * zc, axis=-1, keepdims=True)
    return zc * lax.rsqrt(var + LN_EPS) * g + b


def _rms_norm(h, w):
    return h * lax.rsqrt(jnp.mean(h * h, axis=-1, keepdims=True) + RMS_EPS) * w


def _const_spec(shape):
    nd = len(shape)
    return pl.BlockSpec(shape, lambda *_: (0,) * nd)


def _mod_spec(sub, part, tiles_per_batch, d):
    return pl.BlockSpec((None, None, None, 1, d), lambda i, *_: (sub, part, i // tiles_per_batch, 0, 0))


def _mod_kernel(c_ref, w_ref, b_ref, o_ref):
    c = c_ref[...]
    cond = c * _sigmoid(c)
    w = w_ref[...]
    a_hi = cond.astype(BF16)
    a_lo = (cond - a_hi.astype(F32)).astype(BF16)
    w_hi = w.astype(BF16)
    w_lo = (w - w_hi.astype(F32)).astype(BF16)
    acc = _dot(a_hi, w_hi) + (_dot(a_hi, w_lo) + _dot(a_lo, w_hi))
    o_ref[...] = acc + b_ref[...]


def _modulation(c, ada_w, ada_b):
    depth, _, d, d3 = ada_w.shape
    bsz = c.shape[0]
    nsub = depth * 2
    w = ada_w.reshape(nsub, d, d3)
    b = ada_b.reshape(nsub, 1, d3)
    out = pl.pallas_call(
        _mod_kernel,
        grid=(nsub, 3),
        in_specs=[
            pl.BlockSpec((bsz, d), lambda l, j: (0, 0)),
            pl.BlockSpec((None, d, d), lambda l, j: (l, 0, j)),
            pl.BlockSpec((None, 1, d), lambda l, j: (l, 0, j)),
        ],
        out_specs=pl.BlockSpec((None, None, bsz, d), lambda l, j: (l, j, 0, 0)),
        out_shape=jax.ShapeDtypeStruct((nsub, 3, bsz, d), F32),
        compiler_params=_params("arbitrary", "arbitrary"),
        name="adaln_mod",
    )(c, w, b)
    return out.reshape(nsub, 3, bsz, 1, d)


def _rope_kernel(pos_ref, inv_ref, ecos_ref, esin_ref, base_ref, cos_ref, sin_ref):
    ang = inv_ref[...] * pos_ref[...].astype(F32)
    cos_ref[...] = _spread(jnp.cos(ang), ecos_ref[...]) + base_ref[...]
    sin_ref[...] = _spread(jnp.sin(ang), esin_ref[...])


def _spread(vals_t, e):
    hi, mid, lo = _split3(vals_t.T)
    return _dot(hi, e) + (_dot(mid, e) + _dot(lo, e))


def _rope_tables(positions):
    t = positions.size
    rot = B_HEAD_DIM // ROT_FRAC
    half = rot // 2
    rows = 16
    inv = jnp.power(ROPE_THETA, -jnp.arange(half, dtype=F32) * 2.0 / rot)
    inv_c = jnp.zeros((rows, 1), F32).at[:half, 0].set(inv)
    lane = jnp.arange(LANES) % B_HEAD_DIM
    hit = (jnp.arange(rows)[:, None] == (lane % half)[None, :]) & (lane < rot)[None, :]
    e_cos = hit.astype(BF16)
    e_sin = (hit * jnp.where(lane < half, -1.0, 1.0)[None, :]).astype(BF16)
    base = (lane >= rot).astype(F32).reshape(1, LANES)
    tm = min(TOKEN_TILE, t)
    return pl.pallas_call(
        _rope_kernel,
        grid=(t // tm,),
        in_specs=[pl.BlockSpec((1, tm), lambda i: (0, i)), _const_spec((rows, 1)), _const_spec((rows, LANES)),
                  _const_spec((rows, LANES)), _const_spec((1, LANES))],
        out_specs=[pl.BlockSpec((tm, LANES), lambda i: (i, 0))] * 2,
        out_shape=[jax.ShapeDtypeStruct((t, LANES), F32)] * 2,
        compiler_params=_params("arbitrary"),
        name="rope_tables",
    )(positions.reshape(1, t), inv_c, e_cos, e_sin, base)


def _rope_slab(slab, cos, sin_signed, lo_mask):
    fwd = pltpu.roll(slab, LANES - 8, axis=1)
    bwd = pltpu.roll(slab, 8, axis=1)
    return slab * cos + jnp.where(lo_mask, fwd, bwd) * sin_signed


def _rope_lo_mask(tm):
    lane = lax.broadcasted_iota(jnp.int32, (tm, LANES), 1) % B_HEAD_DIM
    return lane < (B_HEAD_DIM // ROT_FRAC // 2)


def _modulated(x_ref, sh_ref, sc_ref):
    return (x_ref[...] * (1.0 + sc_ref[...]) + sh_ref[...]).astype(BF16)


def _mlstm_in_kernel(x_ref, sh_ref, sc_ref, w_ref, wvt_ref, wot_ref, bg_ref, q_ref, k_ref, vt_ref, ot_ref, g_ref,
                     *, qk):
    h = _modulated(x_ref, sh_ref, sc_ref)
    dk = qk // A_HEADS
    q_ref[...] = _dot(h, w_ref[:, :qk]).astype(BF16)
    k_ref[...] = (_dot(h, w_ref[:, qk:2 * qk]) * (dk ** -0.5)).astype(BF16)
    vt_ref[...] = _dot_nt(wvt_ref[...], h).astype(BF16)
    ot_ref[...] = _sigmoid(_dot_nt(wot_ref[...], h))
    g_ref[...] = _dot(h, w_ref[:, 2 * qk:]) + bg_ref[...]


def _attn_in_body(x_ref, sh_ref, sc_ref, w_ref, wvt_ref, cos_ref, sin_ref, q_ref, k_ref, v_ref, *,
                  qw, kw, scale, kdup=False):
    h = _modulated(x_ref, sh_ref, sc_ref)
    cos = cos_ref[...]
    sin = sin_ref[...]
    lo = _rope_lo_mask(h.shape[0])
    vw = 0 if wvt_ref is not None else w_ref.shape[1] - qw - kw
    first_head = lax.broadcasted_iota(jnp.int32, (h.shape[0], LANES), 1) < B_HEAD_DIM

    def emit(col, slab):
        if col < qw:
            q_ref[:, col:col + LANES] = (_rope_slab(slab, cos, sin, lo) * scale).astype(BF16)
        elif col < qw + kw and kdup:
            r = _rope_slab(slab, cos, sin, lo)
            sw = pltpu.roll(r, B_HEAD_DIM, axis=1)
            c = 2 * (col - qw)
            k_ref[:, c:c + LANES] = jnp.where(first_head, r, sw).astype(BF16)
            k_ref[:, c + LANES:c + 2 * LANES] = jnp.where(first_head, sw, r).astype(BF16)
        elif col < qw + kw:
            k_ref[:, col - qw:col - qw + LANES] = _rope_slab(slab, cos, sin, lo).astype(BF16)
        else:
            v_ref[:, col - qw - kw:col - qw - kw + LANES] = slab.astype(BF16)

    total = qw + kw + vw
    for c0 in range(0, total, MXU_COLS):
        width = min(MXU_COLS, total - c0)
        wide = _dot(h, w_ref[:, c0:c0 + width])
        for off in range(0, width, LANES):
            emit(c0 + off, wide[:, off:off + LANES])
    if wvt_ref is not None:
        v_ref[...] = _dot_nt(wvt_ref[...], h).astype(BF16)


def _attn_in_vt_kernel(x_ref, sh_ref, sc_ref, w_ref, wvt_ref, cos_ref, sin_ref, q_ref, k_ref, v_ref, **kw):
    _attn_in_body(x_ref, sh_ref, sc_ref, w_ref, wvt_ref, cos_ref, sin_ref, q_ref, k_ref, v_ref, **kw)


def _hgrn_in_kernel(x_ref, sh_ref, sc_ref, w_ref, q_ref, f_ref, v_ref, g_ref, *, kw, vw):
    h = _modulated(x_ref, sh_ref, sc_ref)
    q_ref[...] = _dot(h, w_ref[:, :kw])
    f_ref[...] = _dot(h, w_ref[:, kw:2 * kw])
    v_ref[...] = _dot(h, w_ref[:, 2 * kw:2 * kw + vw]).astype(BF16)
    g = _dot(h, w_ref[:, 2 * kw + vw:])
    g_ref[...] = g * _sigmoid(g)


def _in_proj(body, x, mod, sub, seq, w, extra, extra_specs, outs):
    t, d = x.shape
    tm = min(TOKEN_TILE, seq)
    tpb = seq // tm
    row = lambda width: pl.BlockSpec((tm, width), lambda i: (i, 0))
    col = lambda width: pl.BlockSpec((width, tm), lambda i: (0, i))
    return pl.pallas_call(
        body,
        grid=(t // tm,),
        in_specs=[row(d), _mod_spec(sub, 0, tpb, d), _mod_spec(sub, 1, tpb, d), _const_spec(w.shape)] + extra_specs,
        out_specs=[col(wd) if tr else row(wd) for wd, _, tr in outs],
        out_shape=[jax.ShapeDtypeStruct((wd, t) if tr else (t, wd), dt) for wd, dt, tr in outs],
        compiler_params=_params("arbitrary"),
        name=body.func.__name__.strip("_") if isinstance(body, functools.partial) else body.__name__.strip("_"),
    )(x, mod, mod, w, *extra)


def _tail_kernel(y_ref, wp_ref, x_ref, gate0_ref, g0_ref, b0_ref, sh_ref, sc_ref, gate1_ref, wi_ref, wo_ref,
                 g1_ref, b1_ref, o_ref, *, hidden, chunk, alpha):
    hm = x_ref.shape[0] // TAIL_PARTS
    rows = [slice(p * hm, (p + 1) * hm) for p in range(TAIL_PARTS)]
    nchunk = hidden // chunk

    def norm1(i, y):
        return _layer_norm(alpha * x_ref[rows[i], :] + (1.0 + gate0_ref[...]) * y, g0_ref[...], b0_ref[...])

    def ffn_part(h, j):
        a = _dot(h, wi_ref[:, j * chunk:(j + 1) * chunk])
        u = _dot(h, wi_ref[:, hidden + j * chunk:hidden + (j + 1) * chunk])
        act = (a * _sigmoid(a) * u).astype(BF16)
        return _dot(act, wo_ref[j * chunk:(j + 1) * chunk, :])

    def norm2(i, x, acc):
        o_ref[rows[i], :] = _layer_norm(alpha * x + (1.0 + gate1_ref[...]) * acc, g1_ref[...], b1_ref[...])

    ys, xs, hs, accs = {}, {}, {}, {}
    for i in range(TAIL_PARTS + 2):
        if i < TAIL_PARTS:
            ys[i] = _dot(y_ref[rows[i], :], wp_ref[...])
        if 0 <= i - 1 < TAIL_PARTS:
            accs[i - 1] = ffn_part(hs[i - 1], 0)
        if i < TAIL_PARTS:
            xs[i] = norm1(i, ys[i])
            hs[i] = (xs[i] * (1.0 + sc_ref[...]) + sh_ref[...]).astype(BF16)
        if 0 <= i - 1 < TAIL_PARTS:
            for j in range(1, nchunk):
                accs[i - 1] = accs[i - 1] + ffn_part(hs[i - 1], j)
        if 0 <= i - 2 < TAIL_PARTS:
            norm2(i - 2, xs[i - 2], accs[i - 2])


def _ffn_chunk(hidden):
    best = None
    for c in range(MXU_COLS, min(hidden, FFN_MAX_CHUNK) + 1, MXU_COLS):
        if hidden % c == 0:
            best = c
    assert best is not None, hidden
    return best


def _resident_spec(shape):
    nd = len(shape)
    return pl.BlockSpec(shape, lambda *_: (0,) * nd, pipeline_mode=pl.Buffered(1))


def _layer_tail(y, w_proj, x, mod, sub, seq, w_in, w_out, ln_g, ln_b, alpha):
    t, d = x.shape
    hidden = w_out.shape[0]
    tm = min(TAIL_TILE, seq)
    tpb = seq // tm
    row = lambda width: pl.BlockSpec((tm, width), lambda i: (i, 0))
    vec = _const_spec((1, d))
    return pl.pallas_call(
        functools.partial(_tail_kernel, hidden=hidden, chunk=_ffn_chunk(hidden), alpha=alpha),
        grid=(t // tm,),
        in_specs=[row(y.shape[1]), _resident_spec(w_proj.shape), row(d), _mod_spec(sub, 2, tpb, d), vec, vec,
                  _mod_spec(sub + 1, 0, tpb, d), _mod_spec(sub + 1, 1, tpb, d), _mod_spec(sub + 1, 2, tpb, d),
                  _resident_spec(w_in.shape), _resident_spec(w_out.shape), vec, vec],
        out_specs=row(d),
        out_shape=jax.ShapeDtypeStruct((t, d), F32),
        compiler_params=_params("arbitrary"),
        name="out_proj_ffn",
    )(y, w_proj, x, mod, ln_g[0].reshape(1, d), ln_b[0].reshape(1, d), mod, mod, mod, w_in, w_out,
      ln_g[1].reshape(1, d), ln_b[1].reshape(1, d))


def _mlstm_kernel(q_ref, k_ref, vt_ref, ot_ref, g_ref, nw_ref, y_ref, ct_ref, m_ref, z_ref, zt_ref, *,
                  chunk, dk, dv):
    L = chunk
    nchunks = q_ref.shape[0] // L

    @pl.when(pl.program_id(1) == 0)
    def _():
        ct_ref[...] = jnp.zeros_like(ct_ref)
        m_ref[...] = jnp.zeros_like(m_ref)

    key_i = lax.broadcasted_iota(jnp.int32, (L, L), 0)
    qry_i = lax.broadcasted_iota(jnp.int32, (L, L), 1)
    causal_t = key_i <= qry_i

    tril = jnp.where(qry_i <= key_i, 1.0, 0.0).astype(BF16)
    lane = lax.broadcasted_iota(jnp.int32, (L, LANES), 1)
    for r0 in range(0, g_ref.shape[0], L):
        g = g_ref[r0:r0 + L, :]
        logf = jnp.where(lane >= A_HEADS, _log_sigmoid(g), 0.0)
        z = jnp.where(lane < A_HEADS, g, _dot_exact_lhs01(tril, logf))
        z_ref[r0:r0 + L, :] = z
        zt_ref[:, r0:r0 + L] = z.T

    def body(c, carry):
        start = pl.multiple_of(c * L, L)
        rows = pl.ds(start, L)
        z = z_ref[rows, :]
        for h in range(A_HEADS):
            i_col = z[:, h:h + 1]
            b_col = z[:, A_HEADS + h:A_HEADS + h + 1]
            b_row = zt_ref[A_HEADS + h:A_HEADS + h + 1, pl.ds(start, L)]
            m_prev = m_ref[h, 0:1, 0:1]
            d_t = jnp.where(causal_t, b_row + (i_col - b_col), -jnp.inf)
            m_inter = b_row + m_prev
            m_t = jnp.maximum(m_inter, jnp.max(d_t, axis=0, keepdims=True))
            qh = q_ref[rows, h * dk:(h + 1) * dk]
            kh = k_ref[rows, h * dk:(h + 1) * dk]
            vt = vt_ref[h * dv:(h + 1) * dv, pl.ds(start, L)]
            a_t = _dot_nt(kh, qh) * jnp.exp(d_t - m_t)
            w_inter = jnp.exp(m_inter - m_t)
            ct = ct_ref[h]
            inter = _dot_nt(ct.astype(BF16), qh)
            num_t = _dot(vt, a_t.astype(BF16)) + w_inter * inter[:dv]
            den = jnp.sum(a_t, axis=0, keepdims=True) + w_inter * inter[dv:dv + 1]
            hout_t = num_t * (1.0 / jnp.maximum(jnp.abs(den), jnp.exp(-m_t)))
            ms = jnp.mean(hout_t * hout_t, axis=0, keepdims=True)
            y_t = (hout_t * lax.rsqrt(ms + RMS_EPS) * nw_ref[h * dv:(h + 1) * dv, :]
                   * ot_ref[h * dv:(h + 1) * dv, pl.ds(start, L)])
            y_ref[rows, h * dv:(h + 1) * dv] = y_t.T.astype(y_ref.dtype)
            b_last = b_col[L - 1:L, :]
            g_col = b_last - b_col + i_col
            m_new = jnp.maximum(b_last + m_prev, jnp.max(g_col, axis=0, keepdims=True))
            decay = jnp.exp(b_last + m_prev - m_new)
            kws = kh.astype(F32) * jnp.exp(g_col - m_new)
            ct_ref[h, :dv, :] = decay * ct[:dv] + _dot(vt, kws.astype(BF16))
            ct_ref[h, dv:dv + 1, :] = decay * ct[dv:dv + 1] + jnp.sum(kws, axis=0, keepdims=True)
            m_ref[h] = jnp.broadcast_to(m_new, m_ref.shape[1:])
        return carry

    lax.fori_loop(0, nchunks, body, 0)


def _mlstm(q, k, vt, ot, g, norm_w, bsz, seq):
    t, qk = q.shape
    vw = vt.shape[0]
    dk, dv = qk // A_HEADS, vw // A_HEADS
    ts = min(MLSTM_STEP, seq)
    spb = seq // ts
    chunk = min(MLSTM_CHUNK, ts)
    row = lambda width: pl.BlockSpec((ts, width), lambda b, s: (b * spb + s, 0))
    col = pl.BlockSpec((vw, ts), lambda b, s: (0, b * spb + s))
    nw_cols = jnp.broadcast_to(norm_w.astype(F32).reshape(vw, 1), (vw, chunk))
    return pl.pallas_call(
        functools.partial(_mlstm_kernel, chunk=chunk, dk=dk, dv=dv),
        grid=(bsz, spb),
        in_specs=[row(qk), row(qk), col, col, row(LANES), _const_spec((vw, chunk))],
        out_specs=row(vw),
        out_shape=jax.ShapeDtypeStruct((t, vw), BF16),
        scratch_shapes=[pltpu.VMEM((A_HEADS, dv + 8, dk), F32), pltpu.VMEM((A_HEADS, 8, LANES), F32),
                        pltpu.VMEM((ts, LANES), F32), pltpu.VMEM((LANES, ts), F32)],
        compiler_params=_params("arbitrary", "arbitrary"),
        name="mlstm_mixer",
    )(q, k, vt, ot, g, nw_cols)


def _swa_kernel(sink_ref, q_ref, kc_ref, kp_ref, vtc_ref, vtp_ref, o_ref, *, heads, hd, group):
    n = pl.program_id(1)
    blk = q_ref.shape[0]
    hpd = SWA_HEADS_PER_DOT
    key_i = lax.broadcasted_iota(jnp.int32, (2 * blk, blk), 0)
    qry_i = lax.broadcasted_iota(jnp.int32, (2 * blk, blk), 1)
    valid = (key_i > qry_i) & (key_i <= qry_i + blk) & ((key_i >= blk) | (n > 0))
    bias = jnp.concatenate([jnp.where(valid, 0.0, -jnp.inf)] * hpd, axis=1)
    upper = lax.broadcasted_iota(jnp.int32, (blk, LANES), 1) >= hd
    ones = jnp.ones((SWA_ONES_ROWS, 2 * blk), BF16)

    def scores(j0):
        gi = j0 // group
        ks = slice(gi * LANES, (gi + 1) * LANES)
        k2 = jnp.concatenate([kp_ref[:, ks], kc_ref[:, ks]], axis=0)
        qz = []
        for j in range(j0, j0 + hpd):
            slab = q_ref[:, (j // 2) * LANES:(j // 2 + 1) * LANES]
            own = upper if j % 2 else jnp.logical_not(upper)
            qz.append(jnp.where(own, slab, jnp.zeros_like(slab)))
        return _dot_nt(k2, jnp.concatenate(qz, axis=0))

    sts = [scores(j0) for j0 in range(0, heads, hpd)]
    for idx, j0 in enumerate(range(0, heads, hpd)):
        gi = j0 // group
        st = sts[idx] + bias
        off = (j0 % group) * blk
        sink = sink_ref[gi:gi + 1, off:off + hpd * blk] * LOG2_E
        m = jnp.maximum(jnp.max(st, axis=0, keepdims=True), sink)
        p = jnp.exp2(st - m).astype(BF16)
        vs = slice(gi * hd, (gi + 1) * hd)
        vte = jnp.concatenate([jnp.concatenate([vtp_ref[vs, :], vtc_ref[vs, :]], axis=1), ones], axis=0)
        acc = _dot(vte, p)
        ot = acc[:hd, :] * (1.0 / (acc[hd:hd + 1, :] + jnp.exp2(sink - m)))
        for pr in range(hpd // 2):
            pair = jnp.concatenate([ot[:, (2 * pr) * blk:(2 * pr + 1) * blk],
                                    ot[:, (2 * pr + 1) * blk:(2 * pr + 2) * blk]], axis=0)
            c0 = (j0 // 2 + pr) * LANES
            o_ref[:, c0:c0 + LANES] = pair.T.astype(o_ref.dtype)


def _swa(q, k2, vt, sinks, bsz, seq):
    t, qw = q.shape
    kw = vt.shape[0]
    heads = qw // B_HEAD_DIM
    blk = min(B_BLOCK, seq)
    nb = seq // blk
    sink_rows = jnp.repeat(sinks.astype(F32).reshape(heads // B_GROUP, B_GROUP), blk, axis=1)
    cur = lambda width: pl.BlockSpec((blk, width), lambda b, n: (b * nb + n, 0))
    prev = lambda width: pl.BlockSpec((blk, width), lambda b, n: (b * nb + jnp.maximum(n - 1, 0), 0))
    cur_t = pl.BlockSpec((kw, blk), lambda b, n: (0, b * nb + n))
    prev_t = pl.BlockSpec((kw, blk), lambda b, n: (0, b * nb + jnp.maximum(n - 1, 0)))
    return pl.pallas_call(
        functools.partial(_swa_kernel, heads=heads, hd=B_HEAD_DIM, group=B_GROUP),
        grid=(bsz, nb),
        in_specs=[_const_spec(sink_rows.shape), cur(qw), cur(2 * kw), prev(2 * kw), cur_t, prev_t],
        out_specs=cur(qw),
        out_shape=jax.ShapeDtypeStruct((t, qw), BF16),
        compiler_params=_params("arbitrary", "arbitrary"),
        name="swa_mixer",
    )(sink_rows, q, k2, k2, vt, vt)


def _hgrn_kernel(q_ref, f_ref, v_ref, g_ref, lbp_ref, nw_ref, y_ref, st_ref, cf_ref, ck_ref,
                 cfc_ref, ckc_ref, vsc_ref, *,
                 chunk, dk, dv, layer):
    L = chunk
    nchunks = q_ref.shape[0] // L
    heads = q_ref.shape[1] // dk

    @pl.when(pl.program_id(1) == 0)
    def _():
        st_ref[...] = jnp.zeros_like(st_ref)

    lbp = lbp_ref[...]
    e = jnp.exp(lbp - jnp.max(lbp, axis=0, keepdims=True))
    sm = e / jnp.sum(e, axis=0, keepdims=True)
    lb = jnp.zeros_like(sm[0:1])
    for r in range(1, layer + 1):
        lb = lb + sm[r:r + 1]
    log_lb = jnp.log(lb)
    log_1m_lb = jnp.log1p(-lb)

    ones = jnp.ones((dk, LANES), BF16)
    half = L // 2
    trow = lax.broadcasted_iota(jnp.int32, (half, dv), 0)

    rb = min(HGRN_CUMSUM_ROWS, q_ref.shape[0])
    row_i = lax.broadcasted_iota(jnp.int32, (rb, rb), 0)
    col_i = lax.broadcasted_iota(jnp.int32, (rb, rb), 1)
    tril = jnp.where((col_i <= row_i) & (col_i // L == row_i // L), 1.0, 0.0).astype(BF16)
    for r0 in range(0, q_ref.shape[0], rb):
        fp = f_ref[r0:r0 + rb, :]
        ls = jnp.minimum(fp, 0.0) - jnp.log(1.0 + jnp.exp(-jnp.abs(fp)))
        x2 = log_1m_lb + ls
        logf = jnp.maximum(log_lb, x2) + jnp.log(1.0 + jnp.exp(-jnp.abs(log_lb - x2)))
        cf = _dot_exact_lhs01(tril, logf) * LOG2_E
        cf_ref[r0:r0 + rb, :] = cf
        ck_ref[r0:r0 + rb, :] = cf - (x2 - fp) * LOG2_E

    unroll = cfc_ref.shape[0]

    def body(c, carry):
        rows = [pl.ds(pl.multiple_of((c * unroll + u) * L, L), L) for u in range(unroll)]
        for u in range(unroll):
            cfc_ref[u] = cf_ref[rows[u], :]
            ckc_ref[u] = ck_ref[rows[u], :]
            vsc_ref[u] = v_ref[rows[u], :].astype(F32)
        def issue(h, u):
            cs = slice(h * dk, (h + 1) * dk)
            vs = slice(h * dv, (h + 1) * dv)
            cfh = cfc_ref[u, :, cs]
            qh = q_ref[rows[u], cs]
            ws = []
            for s in range(L):
                t0 = 0 if s < half else half
                ws.append(qh[t0:] * jnp.exp2(cfh[t0:] - ckc_ref[u, s:s + 1, cs]))
            w_all = jnp.concatenate(ws, axis=0).astype(BF16)
            r_all = _dot(w_all, ones)
            st = st_ref[h]
            inter = _dot_nt((qh * jnp.exp2(cfh)).astype(BF16), st.astype(BF16))
            last = cfc_ref[u, L - 1:L, cs]
            kdec = jnp.exp2(last - ckc_ref[u, :, cs]).astype(BF16)
            st_ref[h] = jnp.exp2(last) * st + _dot(vsc_ref[u, :, vs].T.astype(BF16), kdec)
            return r_all, inter

        def finish(h, u, r_all, o):
            vs = slice(h * dv, (h + 1) * dv)
            o_lo, o_hi = o[:half], o[half:]
            for s in range(half):
                r = r_all[s * L:(s + 1) * L, :]
                v_s = vsc_ref[u, s:s + 1, vs]
                o_lo = o_lo + jnp.where(trow >= s, r[:half], 0.0) * v_s
                o_hi = o_hi + r[half:] * v_s
            for s in range(half, L):
                r = r_all[half * L + (s - half) * half:half * L + (s - half + 1) * half, :]
                o_hi = o_hi + jnp.where(trow >= s - half, r, 0.0) * vsc_ref[u, s:s + 1, vs]
            o = jnp.concatenate([o_lo, o_hi], axis=0)
            y = _rms_norm(o, nw_ref[:, vs]) * g_ref[rows[u], vs]
            y_ref[rows[u], vs] = y.astype(y_ref.dtype)

        work = [(h, u) for u in range(unroll) for h in range(heads)]
        pending = []
        for hu in work:
            pending.append((hu, issue(*hu)))
            if len(pending) > HGRN_SKEW:
                (h0, u0), (r0, i0) = pending.pop(0)
                finish(h0, u0, r0, i0)
        for (h0, u0), (r0, i0) in pending:
            finish(h0, u0, r0, i0)
        return carry

    lax.fori_loop(0, nchunks // unroll, body, 0)


def _hgrn(q, f, v, g, lower_bounds, layer, norm_w, bsz, seq):
    t, kw = q.shape
    vw = v.shape[1]
    dk, dv = kw // C_HEADS, vw // C_HEADS
    ts = min(HGRN_STEP, seq)
    spb = seq // ts
    chunk = min(HGRN_CHUNK, ts)
    unroll = math.gcd(HGRN_UNROLL, ts // chunk)
    row = lambda width: pl.BlockSpec((ts, width), lambda b, s: (b * spb + s, 0))
    return pl.pallas_call(
        functools.partial(_hgrn_kernel, chunk=chunk, dk=dk, dv=dv, layer=layer),
        grid=(bsz, spb),
        in_specs=[row(kw), row(kw), row(vw), row(vw), _const_spec(lower_bounds.shape), _const_spec((1, vw))],
        out_specs=row(vw),
        out_shape=jax.ShapeDtypeStruct((t, vw), BF16),
        scratch_shapes=[pltpu.VMEM((C_HEADS, dv, dk), F32), pltpu.VMEM((ts, kw), F32), pltpu.VMEM((ts, kw), F32),
                        pltpu.VMEM((unroll, chunk, kw), F32), pltpu.VMEM((unroll, chunk, kw), F32),
                        pltpu.VMEM((unroll, chunk, vw), F32)],
        compiler_params=_params("arbitrary", "arbitrary"),
        name="hgrn2_mixer",
    )(q, f, v, g, lower_bounds, norm_w.reshape(1, vw))


def _diff_kernel(q_ref, k_ref, vt_ref, lam_ref, nw_ref, o_ref, acc_ref, st_ref, *, blk, hd, lam_init):
    qi = pl.program_id(2)
    lv = lam_ref[...]
    lam = (jnp.exp(jnp.sum(lv[0:1] * lv[1:2], axis=-1, keepdims=True))
           - jnp.exp(jnp.sum(lv[2:3] * lv[3:4], axis=-1, keepdims=True)) + lam_init)
    q = q_ref[...]
    lane = lax.broadcasted_iota(jnp.int32, q.shape, 1)
    zero = jnp.zeros_like(q)
    qz = (jnp.where(lane < hd, q, zero), jnp.where(lane >= hd, q, zero))
    acc_ref[...] = jnp.zeros_like(acc_ref)

    hw = 2 * hd
    ones = jnp.ones((DIFF_ONES_ROWS, blk), BF16)

    def scores_into(slot, j):
        kb = k_ref[pl.ds(pl.multiple_of(j * blk, blk), blk), :]
        for c in range(2):
            st_ref[2 * slot + c] = _dot_nt(kb, qz[c])

    def step(j, j_next, m, masked, src):
        if j_next is not None:
            scores_into(1 - src, j_next)
        vte = jnp.concatenate([vt_ref[:, pl.ds(pl.multiple_of(j * blk, blk), blk)], ones], axis=0)
        if masked:
            key_i = lax.broadcasted_iota(jnp.int32, (blk, blk), 0)
            qry_i = lax.broadcasted_iota(jnp.int32, (blk, blk), 1)
            keep = key_i <= qry_i
        m_out = []
        for c in range(2):
            st = st_ref[2 * src + c]
            if masked:
                st = jnp.where(keep, st, -jnp.inf)
            m_new = jnp.maximum(m[c], jnp.max(st, axis=0, keepdims=True))
            alpha = jnp.exp2(m[c] - m_new)
            p = jnp.exp2(st - m_new).astype(BF16)
            acc_ref[c] = alpha * acc_ref[c] + _dot(vte, p)
            m_out.append(m_new)
        return tuple(m_out)

    scores_into(0, qi)
    neg = jnp.full((1, blk), -jnp.inf, F32)
    m = step(qi, 0, (neg, neg), True, 0)

    def pair(i, cr):
        cr = step(2 * i, 2 * i + 1, cr, False, 1)
        return step(2 * i + 1, 2 * i + 2, cr, False, 0)

    m = lax.fori_loop(0, qi // 2, pair, m)

    @pl.when(qi % 2 == 1)
    def _():
        step(qi - 1, None, m, False, 1)

    norm = [acc_ref[c, :hw, :] * (1.0 / acc_ref[c, hw:hw + 1, :]) for c in range(2)]
    out_t = norm[0] - lam * norm[1]
    o_ref[...] = (_rms_norm(out_t.T, nw_ref[...]) * (1.0 - lam_init)).astype(o_ref.dtype)


def _diff_attn(q, k, vt, lam_vec, norm_w, lam_init, bsz, seq):
    t, w = q.shape
    hw = 2 * D_HEAD_DIM
    heads = w // hw
    blk = min(DIFF_BLOCK, seq)
    nq = seq // blk
    return pl.pallas_call(
        functools.partial(_diff_kernel, blk=blk, hd=D_HEAD_DIM, lam_init=lam_init),
        grid=(bsz, heads, nq),
        in_specs=[pl.BlockSpec((blk, hw), lambda b, h, i: (b * nq + i, h)),
                  pl.BlockSpec((seq, hw), lambda b, h, i: (b, h)),
                  pl.BlockSpec((hw, seq), lambda b, h, i: (h, b)),
                  _const_spec(lam_vec.shape), _const_spec((1, hw))],
        out_specs=pl.BlockSpec((blk, hw), lambda b, h, i: (b * nq + i, h)),
        out_shape=jax.ShapeDtypeStruct((t, w), BF16),
        scratch_shapes=[pltpu.VMEM((2, hw + DIFF_ONES_ROWS, blk), F32), pltpu.VMEM((4, blk, blk), F32)],
        compiler_params=_params("arbitrary", "arbitrary", "arbitrary"),
        name="diff_attn_mixer",
    )(q, k, vt, lam_vec, norm_w.reshape(1, hw))


def _pad_cols(w, width):
    return jnp.pad(w, ((0, 0), (0, width - w.shape[1])))


def kernel(x, c, positions, ada_w, ada_b, ln_g, ln_b, mlstm_w_in, mlstm_b_gate, mlstm_norm, mlstm_w_out, swa_w_in, swa_sinks, swa_w_out, hgrn_w_in, hgrn_lower_bounds, hgrn_norm, hgrn_w_out, diff_w_in, diff_lambda, diff_norm, diff_w_out, ffn_w_in, ffn_w_out):
    bsz, seq, d = x.shape
    depth = ada_w.shape[0]
    alpha = (2 * depth) ** 0.25
    t = bsz * seq
    xf = x.reshape(t, d)
    mod = _modulation(c, ada_w, ada_b)
    cos_t, sin_t = _rope_tables(positions)
    for i in range(depth):
        mixer, j = i % 4, i // 4
        sub = 2 * i
        if mixer == 0:
            vw = mlstm_w_out.shape[1]
            qk = vw // 2
            n_gate = 2 * A_HEADS
            w_in = mlstm_w_in[j]
            w = _pad_cols(jnp.concatenate([w_in[:, :2 * qk], w_in[:, 2 * qk + 2 * vw:]], axis=1),
                          2 * qk + LANES).astype(BF16)
            w_vt = w_in[:, 2 * qk:2 * qk + vw].T.astype(BF16)
            w_ot = w_in[:, 2 * qk + vw:2 * qk + 2 * vw].T.astype(BF16)
            bg = _pad_cols(mlstm_b_gate[j].reshape(1, n_gate), LANES)
            q, k, vt, ot, g = _in_proj(
                functools.partial(_mlstm_in_kernel, qk=qk), xf, mod, sub, seq, w, [w_vt, w_ot, bg],
                [_const_spec(w_vt.shape), _const_spec(w_ot.shape), _const_spec((1, LANES))],
                [(qk, BF16, False), (qk, BF16, False), (vw, BF16, True), (vw, F32, True), (LANES, F32, False)])
            y = _mlstm(q, k, vt, ot, g, mlstm_norm[j], bsz, seq)
            w_out = mlstm_w_out[j]
        elif mixer == 1:
            qw = swa_w_out.shape[1]
            kw = qw // B_GROUP
            w_qk = swa_w_in[j][:, :qw + kw].astype(BF16)
            w_vt = swa_w_in[j][:, qw + kw:].T.astype(BF16)
            q, k2, vt = _in_proj(
                functools.partial(_attn_in_vt_kernel, qw=qw, kw=kw, scale=B_HEAD_DIM ** -0.5 * LOG2_E, kdup=True),
                xf, mod, sub, seq, w_qk, [w_vt, cos_t, sin_t],
                [_const_spec(w_vt.shape)] + [pl.BlockSpec((min(TOKEN_TILE, seq), LANES), lambda i: (i, 0))] * 2,
                [(qw, BF16, False), (2 * kw, BF16, False), (kw, BF16, True)])
            y = _swa(q, k2, vt, swa_sinks[j], bsz, seq)
            w_out = swa_w_out[j]
        elif mixer == 2:
            vw = hgrn_w_out.shape[1]
            q, f, v, g = _in_proj(
                functools.partial(_hgrn_in_kernel, kw=vw, vw=vw), xf, mod, sub, seq,
                hgrn_w_in[j].astype(BF16), [], [],
                [(vw, F32, False), (vw, F32, False), (vw, BF16, False), (vw, F32, False)])
            y = _hgrn(q, f, v, g, hgrn_lower_bounds.astype(F32), i, hgrn_norm[j], bsz, seq)
            w_out = hgrn_w_out[j]
        else:
            w_ = diff_w_out.shape[1]
            lam_init = 0.8 - 0.6 * math.exp(-0.3 * i)
            w_qk = diff_w_in[j][:, :2 * w_].astype(BF16)
            w_vt = diff_w_in[j][:, 2 * w_:].T.astype(BF16)
            q, k, vt = _in_proj(
                functools.partial(_attn_in_vt_kernel, qw=w_, kw=w_, scale=D_HEAD_DIM ** -0.5 * math.log2(math.e)),
                xf, mod, sub, seq,
                w_qk, [w_vt, cos_t, sin_t],
                [_const_spec(w_vt.shape)] + [pl.BlockSpec((min(TOKEN_TILE, seq), LANES), lambda i: (i, 0))] * 2,
                [(w_, BF16, False), (w_, BF16, False), (w_, BF16, True)])
            y = _diff_attn(q, k, vt, diff_lambda[j].astype(F32), diff_norm[j], lam_init, bsz, seq)
            w_out = diff_w_out[j]
        xf = _layer_tail(y, w_out.astype(BF16), xf, mod, sub, seq, ffn_w_in[i].astype(BF16),
                         ffn_w_out[i].astype(BF16), ln_g[i], ln_b[i], alpha)
    return xf.reshape(bsz, seq, d)
```

```python
import functools
import math

import jax
import jax.numpy as jnp
from jax import lax
from jax.experimental import pallas as pl
from jax.experimental.pallas import tpu as pltpu

F32 = jnp.float32
BF16 = jnp.bfloat16

A_HEADS = 4
B_HEAD_DIM = 64
B_GROUP = 8
B_BLOCK = 128
C_HEADS = 8
D_HEAD_DIM = 64
ROPE_THETA = 500000.0
ROT_FRAC = 4
LN_EPS = 1e-5
RMS_EPS = 1e-6
LANES = 128
MXU_COLS = 256
LOG2_E = math.log2(math.e)
VMEM_LIMIT = 52 * 2 ** 20

TOKEN_TILE = 512
TAIL_TILE = 512
TAIL_PARTS = 2
FFN_MAX_CHUNK = 4096
MLSTM_CHUNK = 128
MLSTM_STEP = 512
MLSTM_SEQS = 1
HGRN_CHUNK = 16
HGRN_STEP = 512
HGRN_CUMSUM_ROWS = 128
HGRN_UNROLL = 8
HGRN_SKEW = 1
DIFF_BLOCK = 512
DIFF_ONES_ROWS = 16
SWA_ONES_ROWS = 16
SWA_HEADS_PER_DOT = 8


def _params(*sem):
    return pltpu.CompilerParams(dimension_semantics=sem, vmem_limit_bytes=VMEM_LIMIT)


def _dot(a, b):
    return jnp.dot(a, b, preferred_element_type=F32)


def _dot_nt(a, b):
    return lax.dot_general(a, b, (((1,), (1,)), ((), ())), preferred_element_type=F32)


def _split3(x):
    hi = x.astype(BF16)
    r = x - hi.astype(F32)
    mid = r.astype(BF16)
    lo = (r - mid.astype(F32)).astype(BF16)
    return hi, mid, lo


def _dot_exact_lhs01(mask_bf16, x):
    hi, mid, lo = _split3(x)
    return _dot(mask_bf16, hi) + (_dot(mask_bf16, mid) + _dot(mask_bf16, lo))


def _sigmoid(x):
    return 1.0 / (1.0 + jnp.exp(-x))


def _log_sigmoid(x):
    return jnp.minimum(x, 0.0) - jnp.log1p(jnp.exp(-jnp.abs(x)))


def _layer_norm(z, g, b):
    mu = jnp.mean(z, axis=-1, keepdims=True)
    zc = z - mu
    var = jnp.mean(zc * zc, axis=-1, keepdims=True)
    return zc * lax.rsqrt(var + LN_EPS) * g + b


def _rms_norm(h, w):
    return h * lax.rsqrt(jnp.mean(h * h, axis=-1, keepdims=True) + RMS_EPS) * w


def _const_spec(shape):
    nd = len(shape)
    return pl.BlockSpec(shape, lambda *_: (0,) * nd)


def _mod_spec(sub, part, tiles_per_batch, d):
    return pl.BlockSpec((None, None, None, 1, d), lambda i, *_: (sub, part, i // tiles_per_batch, 0, 0))


def _mod_kernel(c_ref, w_ref, b_ref, o_ref):
    c = c_ref[...]
    cond = c * _sigmoid(c)
    w = w_ref[...]
    a_hi = cond.astype(BF16)
    a_lo = (cond - a_hi.astype(F32)).astype(BF16)
    w_hi = w.astype(BF16)
    w_lo = (w - w_hi.astype(F32)).astype(BF16)
    acc = _dot(a_hi, w_hi) + (_dot(a_hi, w_lo) + _dot(a_lo, w_hi))
    o_ref[...] = acc + b_ref[...]


def _modulation(c, ada_w, ada_b):
    depth, _, d, d3 = ada_w.shape
    bsz = c.shape[0]
    nsub = depth * 2
    w = ada_w.reshape(nsub, d, d3)
    b = ada_b.reshape(nsub, 1, d3)
    out = pl.pallas_call(
        _mod_kernel,
        grid=(nsub, 3),
        in_specs=[
            pl.BlockSpec((bsz, d), lambda l, j: (0, 0)),
            pl.BlockSpec((None, d, d), lambda l, j: (l, 0, j)),
            pl.BlockSpec((None, 1, d), lambda l, j: (l, 0, j)),
        ],
        out_specs=pl.BlockSpec((None, None, bsz, d), lambda l, j: (l, j, 0, 0)),
        out_shape=jax.ShapeDtypeStruct((nsub, 3, bsz, d), F32),
        compiler_params=_params("arbitrary", "arbitrary"),
        name="adaln_mod",
    )(c, w, b)
    return out.reshape(nsub, 3, bsz, 1, d)


def _rope_kernel(pos_ref, inv_ref, ecos_ref, esin_ref, base_ref, cos_ref, sin_ref):
    ang = inv_ref[...] * pos_ref[...].astype(F32)
    cos_ref[...] = _spread(jnp.cos(ang), ecos_ref[...]) + base_ref[...]
    sin_ref[...] = _spread(jnp.sin(ang), esin_ref[...])


def _spread(vals_t, e):
    hi, mid, lo = _split3(vals_t.T)
    return _dot(hi, e) + (_dot(mid, e) + _dot(lo, e))


def _rope_tables(positions):
    t = positions.size
    rot = B_HEAD_DIM // ROT_FRAC
    half = rot // 2
    rows = 16
    inv = jnp.power(ROPE_THETA, -jnp.arange(half, dtype=F32) * 2.0 / rot)
    inv_c = jnp.zeros((rows, 1), F32).at[:half, 0].set(inv)
    lane = jnp.arange(LANES) % B_HEAD_DIM
    hit = (jnp.arange(rows)[:, None] == (lane % half)[None, :]) & (lane < rot)[None, :]
    e_cos = hit.astype(BF16)
    e_sin = (hit * jnp.where(lane < half, -1.0, 1.0)[None, :]).astype(BF16)
    base = (lane >= rot).astype(F32).reshape(1, LANES)
    tm = min(TOKEN_TILE, t)
    return pl.pallas_call(
        _rope_kernel,
        grid=(t // tm,),
        in_specs=[pl.BlockSpec((1, tm), lambda i: (0, i)), _const_spec((rows, 1)), _const_spec((rows, LANES)),
                  _const_spec((rows, LANES)), _const_spec((1, LANES))],
        out_specs=[pl.BlockSpec((tm, LANES), lambda i: (i, 0))] * 2,
        out_shape=[jax.ShapeDtypeStruct((t, LANES), F32)] * 2,
        compiler_params=_params("arbitrary"),
        name="rope_tables",
    )(positions.reshape(1, t), inv_c, e_cos, e_sin, base)


def _rope_slab(slab, cos, sin_signed, lo_mask):
    fwd = pltpu.roll(slab, LANES - 8, axis=1)
    bwd = pltpu.roll(slab, 8, axis=1)
    return slab * cos + jnp.where(lo_mask, fwd, bwd) * sin_signed


def _rope_lo_mask(tm):
    lane = lax.broadcasted_iota(jnp.int32, (tm, LANES), 1) % B_HEAD_DIM
    return lane < (B_HEAD_DIM // ROT_FRAC // 2)


def _modulated(x_ref, sh_ref, sc_ref):
    return (x_ref[...] * (1.0 + sc_ref[...]) + sh_ref[...]).astype(BF16)


def _mlstm_in_kernel(x_ref, sh_ref, sc_ref, w_ref, wvt_ref, wot_ref, bg_ref, q_ref, k_ref, vt_ref, ot_ref, g_ref,
                     *, qk):
    h = _modulated(x_ref, sh_ref, sc_ref)
    dk = qk // A_HEADS
    q_ref[...] = _dot(h, w_ref[:, :qk]).astype(BF16)
    k_ref[...] = (_dot(h, w_ref[:, qk:2 * qk]) * (dk ** -0.5)).astype(BF16)
    vt_ref[...] = _dot_nt(wvt_ref[...], h).astype(BF16)
    ot_ref[...] = _sigmoid(_dot_nt(wot_ref[...], h))
    g_ref[...] = _dot(h, w_ref[:, 2 * qk:]) + bg_ref[...]


def _attn_in_body(x_ref, sh_ref, sc_ref, w_ref, wvt_ref, cos_ref, sin_ref, q_ref, k_ref, v_ref, *,
                  qw, kw, scale, kdup=False):
    h = _modulated(x_ref, sh_ref, sc_ref)
    cos = cos_ref[...]
    sin = sin_ref[...]
    lo = _rope_lo_mask(h.shape[0])
    vw = 0 if wvt_ref is not None else w_ref.shape[1] - qw - kw
    first_head = lax.broadcasted_iota(jnp.int32, (h.shape[0], LANES), 1) < B_HEAD_DIM

    def emit(col, slab):
        if col < qw:
            q_ref[:, col:col + LANES] = (_rope_slab(slab, cos, sin, lo) * scale).astype(BF16)
        elif col < qw + kw and kdup:
            r = _rope_slab(slab, cos, sin, lo)
            sw = pltpu.roll(r, B_HEAD_DIM, axis=1)
            c = 2 * (col - qw)
            k_ref[:, c:c + LANES] = jnp.where(first_head, r, sw).astype(BF16)
            k_ref[:, c + LANES:c + 2 * LANES] = jnp.where(first_head, sw, r).astype(BF16)
        elif col < qw + kw:
            k_ref[:, col - qw:col - qw + LANES] = _rope_slab(slab, cos, sin, lo).astype(BF16)
        else:
            v_ref[:, col - qw - kw:col - qw - kw + LANES] = slab.astype(BF16)

    total = qw + kw + vw
    for c0 in range(0, total, MXU_COLS):
        width = min(MXU_COLS, total - c0)
        wide = _dot(h, w_ref[:, c0:c0 + width])
        for off in range(0, width, LANES):
            emit(c0 + off, wide[:, off:off + LANES])
    if wvt_ref is not None:
        v_ref[...] = _dot_nt(wvt_ref[...], h).astype(BF16)


def _attn_in_vt_kernel(x_ref, sh_ref, sc_ref, w_ref, wvt_ref, cos_ref, sin_ref, q_ref, k_ref, v_ref, **kw):
    _attn_in_body(x_ref, sh_ref, sc_ref, w_ref, wvt_ref, cos_ref, sin_ref, q_ref, k_ref, v_ref, **kw)


def _hgrn_in_kernel(x_ref, sh_ref, sc_ref, w_ref, q_ref, f_ref, v_ref, g_ref, *, kw, vw):
    h = _modulated(x_ref, sh_ref, sc_ref)
    q_ref[...] = _dot(h, w_ref[:, :kw])
    f_ref[...] = _dot(h, w_ref[:, kw:2 * kw])
    v_ref[...] = _dot(h, w_ref[:, 2 * kw:2 * kw + vw]).astype(BF16)
    g = _dot(h, w_ref[:, 2 * kw + vw:])
    g_ref[...] = g * _sigmoid(g)


def _in_proj(body, x, mod, sub, seq, w, extra, extra_specs, outs):
    t, d = x.shape
    tm = min(TOKEN_TILE, seq)
    tpb = seq // tm
    row = lambda width: pl.BlockSpec((tm, width), lambda i: (i, 0))
    col = lambda width: pl.BlockSpec((width, tm), lambda i: (0, i))
    return pl.pallas_call(
        body,
        grid=(t // tm,),
        in_specs=[row(d), _mod_spec(sub, 0, tpb, d), _mod_spec(sub, 1, tpb, d), _const_spec(w.shape)] + extra_specs,
        out_specs=[col(wd) if tr else row(wd) for wd, _, tr in outs],
        out_shape=[jax.ShapeDtypeStruct((wd, t) if tr else (t, wd), dt) for wd, dt, tr in outs],
        compiler_params=_params("arbitrary"),
        name=body.func.__name__.strip("_") if isinstance(body, functools.partial) else body.__name__.strip("_"),
    )(x, mod, mod, w, *extra)


def _tail_kernel(y_ref, wp_ref, x_ref, gate0_ref, g0_ref, b0_ref, sh_ref, sc_ref, gate1_ref, wi_ref, wo_ref,
                 g1_ref, b1_ref, o_ref, *, hidden, chunk, alpha):
    hm = x_ref.shape[0] // TAIL_PARTS
    rows = [slice(p * hm, (p + 1) * hm) for p in range(TAIL_PARTS)]
    nchunk = hidden // chunk

    def norm1(i, y):
        return _layer_norm(alpha * x_ref[rows[i], :] + (1.0 + gate0_ref[...]) * y, g0_ref[...], b0_ref[...])

    def ffn_part(h, j):
        a = _dot(h, wi_ref[:, j * chunk:(j + 1) * chunk])
        u = _dot(h, wi_ref[:, hidden + j * chunk:hidden + (j + 1) * chunk])
        act = (a * _sigmoid(a) * u).astype(BF16)
        return _dot(act, wo_ref[j * chunk:(j + 1) * chunk, :])

    def norm2(i, x, acc):
        o_ref[rows[i], :] = _layer_norm(alpha * x + (1.0 + gate1_ref[...]) * acc, g1_ref[...], b1_ref[...])

    ys, xs, hs, accs = {}, {}, {}, {}
    for i in range(TAIL_PARTS + 2):
        if i < TAIL_PARTS:
            ys[i] = _dot(y_ref[rows[i], :], wp_ref[...])
        if 0 <= i - 1 < TAIL_PARTS:
            accs[i - 1] = ffn_part(hs[i - 1], 0)
        if i < TAIL_PARTS:
            xs[i] = norm1(i, ys[i])
            hs[i] = (xs[i] * (1.0 + sc_ref[...]) + sh_ref[...]).astype(BF16)
        if 0 <= i - 1 < TAIL_PARTS:
            for j in range(1, nchunk):
                accs[i - 1] = accs[i - 1] + ffn_part(hs[i - 1], j)
        if 0 <= i - 2 < TAIL_PARTS:
            norm2(i - 2, xs[i - 2], accs[i - 2])


def _ffn_chunk(hidden):
    best = None
    for c in range(MXU_COLS, min(hidden, FFN_MAX_CHUNK) + 1, MXU_COLS):
        if hidden % c == 0:
            best = c
    assert best is not None, hidden
    return best


def _resident_spec(shape):
    nd = len(shape)
    return pl.BlockSpec(shape, lambda *_: (0,) * nd, pipeline_mode=pl.Buffered(1))


def _layer_tail(y, w_proj, x, mod, sub, seq, w_in, w_out, ln_g, ln_b, alpha):
    t, d = x.shape
    hidden = w_out.shape[0]
    tm = min(TAIL_TILE, seq)
    tpb = seq // tm
    row = lambda width: pl.BlockSpec((tm, width), lambda i: (i, 0))
    vec = _const_spec((1, d))
    return pl.pallas_call(
        functools.partial(_tail_kernel, hidden=hidden, chunk=_ffn_chunk(hidden), alpha=alpha),
        grid=(t // tm,),
        in_specs=[row(y.shape[1]), _resident_spec(w_proj.shape), row(d), _mod_spec(sub, 2, tpb, d), vec, vec,
                  _mod_spec(sub + 1, 0, tpb, d), _mod_spec(sub + 1, 1, tpb, d), _mod_spec(sub + 1, 2, tpb, d),
                  _resident_spec(w_in.shape), _resident_spec(w_out.shape), vec, vec],
        out_specs=row(d),
        out_shape=jax.ShapeDtypeStruct((t, d), F32),
        compiler_params=_params("arbitrary"),
        name="out_proj_ffn",
    )(y, w_proj, x, mod, ln_g[0].reshape(1, d), ln_b[0].reshape(1, d), mod, mod, mod, w_in, w_out,
      ln_g[1].reshape(1, d), ln_b[1].reshape(1, d))


def _mlstm_kernel(q_ref, k_ref, vt_ref, ot_ref, g_ref, nw_ref, y_ref, ct_ref, m_ref, z_ref, zt_ref, *,
                  chunk, dk, dv):
    L = chunk
    nchunks = q_ref.shape[0] // L

    @pl.when(pl.program_id(1) == 0)
    def _():
        ct_ref[...] = jnp.zeros_like(ct_ref)
        m_ref[...] = jnp.zeros_like(m_ref)

    key_i = lax.broadcasted_iota(jnp.int32, (L, L), 0)
    qry_i = lax.broadcasted_iota(jnp.int32, (L, L), 1)
    causal_t = key_i <= qry_i

    tril = jnp.where(qry_i <= key_i, 1.0, 0.0).astype(BF16)
    lane = lax.broadcasted_iota(jnp.int32, (L, LANES), 1)
    for r0 in range(0, g_ref.shape[0], L):
        g = g_ref[r0:r0 + L, :]
        logf = jnp.where(lane >= A_HEADS, _log_sigmoid(g), 0.0)
        z = jnp.where(lane < A_HEADS, g, _dot_exact_lhs01(tril, logf))
        z_ref[r0:r0 + L, :] = z
        zt_ref[:, r0:r0 + L] = z.T

    def body(c, carry):
        start = pl.multiple_of(c * L, L)
        rows = pl.ds(start, L)
        z = z_ref[rows, :]
        early = []
        for h in range(A_HEADS):
            i_col = z[:, h:h + 1]
            b_col = z[:, A_HEADS + h:A_HEADS + h + 1]
            m_prev = m_ref[h, 0:1, 0:1]
            qh = q_ref[rows, h * dk:(h + 1) * dk]
            kh = k_ref[rows, h * dk:(h + 1) * dk]
            vt = vt_ref[h * dv:(h + 1) * dv, pl.ds(start, L)]
            s_t = _dot_nt(kh, qh)
            ct = ct_ref[h]
            inter = _dot_nt(ct.astype(BF16), qh)
            b_last = b_col[L - 1:L, :]
            g_col = b_last - b_col + i_col
            m_new = jnp.maximum(b_last + m_prev, jnp.max(g_col, axis=0, keepdims=True))
            decay = jnp.exp(b_last + m_prev - m_new)
            kws = kh.astype(F32) * jnp.exp(g_col - m_new)
            ct_ref[h, :dv, :] = decay * ct[:dv] + _dot(vt, kws.astype(BF16))
            ct_ref[h, dv:dv + 1, :] = decay * ct[dv:dv + 1] + jnp.sum(kws, axis=0, keepdims=True)
            m_ref[h] = jnp.broadcast_to(m_new, m_ref.shape[1:])
            early.append((i_col - b_col, m_prev, vt, s_t, inter))
        for h in range(A_HEADS):
            ib_col, m_prev, vt, s_t, inter = early[h]
            b_row = zt_ref[A_HEADS + h:A_HEADS + h + 1, pl.ds(start, L)]
            d_t = jnp.where(causal_t, b_row + ib_col, -jnp.inf)
            m_inter = b_row + m_prev
            m_t = jnp.maximum(m_inter, jnp.max(d_t, axis=0, keepdims=True))
            a_t = s_t * jnp.exp(d_t - m_t)
            w_inter = jnp.exp(m_inter - m_t)
            num_t = _dot(vt, a_t.astype(BF16)) + w_inter * inter[:dv]
            den = jnp.sum(a_t, axis=0, keepdims=True) + w_inter * inter[dv:dv + 1]
            hout_t = num_t * (1.0 / jnp.maximum(jnp.abs(den), jnp.exp(-m_t)))
            ms = jnp.mean(hout_t * hout_t, axis=0, keepdims=True)
            y_t = (hout_t * lax.rsqrt(ms + RMS_EPS) * nw_ref[h * dv:(h + 1) * dv, :]
                   * ot_ref[h * dv:(h + 1) * dv, pl.ds(start, L)])
            y_ref[rows, h * dv:(h + 1) * dv] = y_t.T.astype(y_ref.dtype)
        return carry

    lax.fori_loop(0, nchunks, body, 0, unroll=True)


def _mlstm(q, k, vt, ot, g, norm_w, bsz, seq):
    t, qk = q.shape
    vw = vt.shape[0]
    dk, dv = qk // A_HEADS, vw // A_HEADS
    ts = min(MLSTM_STEP, seq)
    spb = seq // ts
    chunk = min(MLSTM_CHUNK, ts)
    row = lambda width: pl.BlockSpec((ts, width), lambda b, s: (b * spb + s, 0))
    col = pl.BlockSpec((vw, ts), lambda b, s: (0, b * spb + s))
    nw_cols = jnp.broadcast_to(norm_w.astype(F32).reshape(vw, 1), (vw, chunk))
    return pl.pallas_call(
        functools.partial(_mlstm_kernel, chunk=chunk, dk=dk, dv=dv),
        grid=(bsz, spb),
        in_specs=[row(qk), row(qk), col, col, row(LANES), _const_spec((vw, chunk))],
        out_specs=row(vw),
        out_shape=jax.ShapeDtypeStruct((t, vw), BF16),
        scratch_shapes=[pltpu.VMEM((A_HEADS, dv + 8, dk), F32), pltpu.VMEM((A_HEADS, 8, LANES), F32),
                        pltpu.VMEM((ts, LANES), F32), pltpu.VMEM((LANES, ts), F32)],
        compiler_params=_params("arbitrary", "arbitrary"),
        name="mlstm_mixer",
    )(q, k, vt, ot, g, nw_cols)


def _swa_kernel(sink_ref, q_ref, kc_ref, kp_ref, vtc_ref, vtp_ref, o_ref, st_ref, *, heads, hd, group):
    n = pl.program_id(1)
    blk = q_ref.shape[0]
    hpd = SWA_HEADS_PER_DOT
    key_i = lax.broadcasted_iota(jnp.int32, (2 * blk, blk), 0)
    qry_i = lax.broadcasted_iota(jnp.int32, (2 * blk, blk), 1)
    valid = (key_i > qry_i) & (key_i <= qry_i + blk) & ((key_i >= blk) | (n > 0))
    bias = jnp.concatenate([jnp.where(valid, 0.0, -jnp.inf)] * hpd, axis=1)
    upper = lax.broadcasted_iota(jnp.int32, (blk, LANES), 1) >= hd
    ones = jnp.ones((SWA_ONES_ROWS, 2 * blk), BF16)

    def scores(j0):
        gi = j0 // group
        ks = slice(gi * LANES, (gi + 1) * LANES)
        k2 = jnp.concatenate([kp_ref[:, ks], kc_ref[:, ks]], axis=0)
        qz = []
        for j in range(j0, j0 + hpd):
            slab = q_ref[:, (j // 2) * LANES:(j // 2 + 1) * LANES]
            own = upper if j % 2 else jnp.logical_not(upper)
            qz.append(jnp.where(own, slab, jnp.zeros_like(slab)))
        return _dot_nt(k2, jnp.concatenate(qz, axis=0))

    for idx, j0 in enumerate(range(0, heads, hpd)):
        st_ref[idx] = scores(j0)
    for idx, j0 in enumerate(range(0, heads, hpd)):
        gi = j0 // group
        st = st_ref[idx] + bias
        off = (j0 % group) * blk
        sink = sink_ref[gi:gi + 1, off:off + hpd * blk] * LOG2_E
        m = jnp.maximum(jnp.max(st, axis=0, keepdims=True), sink)
        p = jnp.exp2(st - m).astype(BF16)
        vs = slice(gi * hd, (gi + 1) * hd)
        vte = jnp.concatenate([jnp.concatenate([vtp_ref[vs, :], vtc_ref[vs, :]], axis=1), ones], axis=0)
        acc = _dot(vte, p)
        ot = acc[:hd, :] * (1.0 / (acc[hd:hd + 1, :] + jnp.exp2(sink - m)))
        for pr in range(hpd // 2):
            pair = jnp.concatenate([ot[:, (2 * pr) * blk:(2 * pr + 1) * blk],
                                    ot[:, (2 * pr + 1) * blk:(2 * pr + 2) * blk]], axis=0)
            c0 = (j0 // 2 + pr) * LANES
            o_ref[:, c0:c0 + LANES] = pair.T.astype(o_ref.dtype)


def _swa(q, k2, vt, sinks, bsz, seq):
    t, qw = q.shape
    kw = vt.shape[0]
    heads = qw // B_HEAD_DIM
    blk = min(B_BLOCK, seq)
    nb = seq // blk
    sink_rows = jnp.repeat(sinks.astype(F32).reshape(heads // B_GROUP, B_GROUP), blk, axis=1)
    cur = lambda width: pl.BlockSpec((blk, width), lambda b, n: (b * nb + n, 0))
    prev = lambda width: pl.BlockSpec((blk, width), lambda b, n: (b * nb + jnp.maximum(n - 1, 0), 0))
    cur_t = pl.BlockSpec((kw, blk), lambda b, n: (0, b * nb + n))
    prev_t = pl.BlockSpec((kw, blk), lambda b, n: (0, b * nb + jnp.maximum(n - 1, 0)))
    return pl.pallas_call(
        functools.partial(_swa_kernel, heads=heads, hd=B_HEAD_DIM, group=B_GROUP),
        grid=(bsz, nb),
        in_specs=[_const_spec(sink_rows.shape), cur(qw), cur(2 * kw), prev(2 * kw), cur_t, prev_t],
        out_specs=cur(qw),
        out_shape=jax.ShapeDtypeStruct((t, qw), BF16),
        scratch_shapes=[pltpu.VMEM((heads // SWA_HEADS_PER_DOT, 2 * blk, SWA_HEADS_PER_DOT * blk), F32)],
        compiler_params=_params("arbitrary", "arbitrary"),
        name="swa_mixer",
    )(sink_rows, q, k2, k2, vt, vt)


def _hgrn_kernel(q_ref, f_ref, v_ref, g_ref, lbp_ref, nw_ref, y_ref, st_ref, cf_ref, ck_ref,
                 cfc_ref, ckc_ref, vsc_ref, *,
                 chunk, dk, dv, layer):
    L = chunk
    nchunks = q_ref.shape[0] // L
    heads = q_ref.shape[1] // dk

    @pl.when(pl.program_id(1) == 0)
    def _():
        st_ref[...] = jnp.zeros_like(st_ref)

    lbp = lbp_ref[...]
    e = jnp.exp(lbp - jnp.max(lbp, axis=0, keepdims=True))
    sm = e / jnp.sum(e, axis=0, keepdims=True)
    lb = jnp.zeros_like(sm[0:1])
    for r in range(1, layer + 1):
        lb = lb + sm[r:r + 1]
    log2_lb = jnp.log(lb) * LOG2_E
    log2_1m_lb = jnp.log1p(-lb) * LOG2_E

    ones = jnp.ones((dk, LANES), BF16)
    half = L // 2
    trow = lax.broadcasted_iota(jnp.int32, (half, dv), 0)

    rb = min(HGRN_CUMSUM_ROWS, q_ref.shape[0])
    row_i = lax.broadcasted_iota(jnp.int32, (rb, rb), 0)
    col_i = lax.broadcasted_iota(jnp.int32, (rb, rb), 1)
    tril = jnp.where((col_i <= row_i) & (col_i // L == row_i // L), 1.0, 0.0).astype(BF16)
    for r0 in range(0, q_ref.shape[0], rb):
        f2 = f_ref[r0:r0 + rb, :] * LOG2_E
        ls = jnp.minimum(f2, 0.0) - jnp.log2(1.0 + jnp.exp2(-jnp.abs(f2)))
        x2 = log2_1m_lb + ls
        logf = jnp.maximum(log2_lb, x2) + jnp.log2(1.0 + jnp.exp2(-jnp.abs(log2_lb - x2)))
        cf = _dot_exact_lhs01(tril, logf)
        cf_ref[r0:r0 + rb, :] = cf
        ck_ref[r0:r0 + rb, :] = cf - (x2 - f2)

    unroll = cfc_ref.shape[0]

    def body(c, carry):
        rows = [pl.ds(pl.multiple_of((c * unroll + u) * L, L), L) for u in range(unroll)]
        for u in range(unroll):
            cfc_ref[u] = cf_ref[rows[u], :]
            ckc_ref[u] = ck_ref[rows[u], :]
            vsc_ref[u] = v_ref[rows[u], :].astype(F32)
        def issue(h, u):
            cs = slice(h * dk, (h + 1) * dk)
            vs = slice(h * dv, (h + 1) * dv)
            cfh = cfc_ref[u, :, cs]
            qh = q_ref[rows[u], cs]
            ws = []
            for s in range(L):
                t0 = 0 if s < half else half
                ws.append(qh[t0:] * jnp.exp2(cfh[t0:] - ckc_ref[u, s:s + 1, cs]))
            w_all = jnp.concatenate(ws, axis=0).astype(BF16)
            r_all = _dot(w_all, ones)
            st = st_ref[h]
            inter = _dot_nt((qh * jnp.exp2(cfh)).astype(BF16), st.astype(BF16))
            last = cfc_ref[u, L - 1:L, cs]
            kdec = jnp.exp2(last - ckc_ref[u, :, cs]).astype(BF16)
            st_ref[h] = jnp.exp2(last) * st + _dot(vsc_ref[u, :, vs].T.astype(BF16), kdec)
            return r_all, inter

        def finish(h, u, r_all, o):
            vs = slice(h * dv, (h + 1) * dv)
            o_lo, o_hi = o[:half], o[half:]
            for s in range(half):
                r = r_all[s * L:(s + 1) * L, :]
                v_s = vsc_ref[u, s:s + 1, vs]
                o_lo = o_lo + jnp.where(trow >= s, r[:half], 0.0) * v_s
                o_hi = o_hi + r[half:] * v_s
            for s in range(half, L):
                r = r_all[half * L + (s - half) * half:half * L + (s - half + 1) * half, :]
                o_hi = o_hi + jnp.where(trow >= s - half, r, 0.0) * vsc_ref[u, s:s + 1, vs]
            o = jnp.concatenate([o_lo, o_hi], axis=0)
            y = _rms_norm(o, nw_ref[:, vs]) * g_ref[rows[u], vs]
            y_ref[rows[u], vs] = y.astype(y_ref.dtype)

        work = [(h, u) for u in range(unroll) for h in range(heads)]
        pending = []
        for hu in work:
            pending.append((hu, issue(*hu)))
            if len(pending) > HGRN_SKEW:
                (h0, u0), (r0, i0) = pending.pop(0)
                finish(h0, u0, r0, i0)
        for (h0, u0), (r0, i0) in pending:
            finish(h0, u0, r0, i0)
        return carry

    lax.fori_loop(0, nchunks // unroll, body, 0)


def _hgrn(q, f, v, g, lower_bounds, layer, norm_w, bsz, seq):
    t, kw = q.shape
    vw = v.shape[1]
    dk, dv = kw // C_HEADS, vw // C_HEADS
    ts = min(HGRN_STEP, seq)
    spb = seq // ts
    chunk = min(HGRN_CHUNK, ts)
    unroll = math.gcd(HGRN_UNROLL, ts // chunk)
    row = lambda width: pl.BlockSpec((ts, width), lambda b, s: (b * spb + s, 0))
    return pl.pallas_call(
        functools.partial(_hgrn_kernel, chunk=chunk, dk=dk, dv=dv, layer=layer),
        grid=(bsz, spb),
        in_specs=[row(kw), row(kw), row(vw), row(vw), _const_spec(lower_bounds.shape), _const_spec((1, vw))],
        out_specs=row(vw),
        out_shape=jax.ShapeDtypeStruct((t, vw), BF16),
        scratch_shapes=[pltpu.VMEM((C_HEADS, dv, dk), F32), pltpu.VMEM((ts, kw), F32), pltpu.VMEM((ts, kw), F32),
                        pltpu.VMEM((unroll, chunk, kw), F32), pltpu.VMEM((unroll, chunk, kw), F32),
                        pltpu.VMEM((unroll, chunk, vw), F32)],
        compiler_params=_params("arbitrary", "arbitrary"),
        name="hgrn2_mixer",
    )(q, f, v, g, lower_bounds, norm_w.reshape(1, vw))


def _diff_kernel(q_ref, k_ref, vt_ref, lam_ref, nw_ref, o_ref, acc_ref, st_ref, *, blk, hd, lam_init):
    qi = pl.program_id(2)
    lv = lam_ref[...]
    lam = (jnp.exp(jnp.sum(lv[0:1] * lv[1:2], axis=-1, keepdims=True))
           - jnp.exp(jnp.sum(lv[2:3] * lv[3:4], axis=-1, keepdims=True)) + lam_init)
    q = q_ref[...]
    lane = lax.broadcasted_iota(jnp.int32, q.shape, 1)
    zero = jnp.zeros_like(q)
    qz = (jnp.where(lane < hd, q, zero), jnp.where(lane >= hd, q, zero))
    acc_ref[...] = jnp.zeros_like(acc_ref)

    hw = 2 * hd
    ones = jnp.ones((DIFF_ONES_ROWS, blk), BF16)

    def scores_into(slot, j):
        kb = k_ref[pl.ds(pl.multiple_of(j * blk, blk), blk), :]
        for c in range(2):
            st_ref[2 * slot + c] = _dot_nt(kb, qz[c])

    def step(j, j_next, m, masked, src):
        if j_next is not None:
            scores_into(1 - src, j_next)
        vte = jnp.concatenate([vt_ref[:, pl.ds(pl.multiple_of(j * blk, blk), blk)], ones], axis=0)
        if masked:
            key_i = lax.broadcasted_iota(jnp.int32, (blk, blk), 0)
            qry_i = lax.broadcasted_iota(jnp.int32, (blk, blk), 1)
            keep = key_i <= qry_i
        m_out = []
        for c in range(2):
            st = st_ref[2 * src + c]
            if masked:
                st = jnp.where(keep, st, -jnp.inf)
            m_new = jnp.maximum(m[c], jnp.max(st, axis=0, keepdims=True))
            alpha = jnp.exp2(m[c] - m_new)
            p = jnp.exp2(st - m_new).astype(BF16)
            acc_ref[c] = alpha * acc_ref[c] + _dot(vte, p)
            m_out.append(m_new)
        return tuple(m_out)

    scores_into(0, qi)
    neg = jnp.full((1, blk), -jnp.inf, F32)
    m = step(qi, 0, (neg, neg), True, 0)

    def pair(i, cr):
        cr = step(2 * i, 2 * i + 1, cr, False, 1)
        return step(2 * i + 1, 2 * i + 2, cr, False, 0)

    m = lax.fori_loop(0, qi // 2, pair, m)

    @pl.when(qi % 2 == 1)
    def _():
        step(qi - 1, None, m, False, 1)

    norm = [acc_ref[c, :hw, :] * (1.0 / acc_ref[c, hw:hw + 1, :]) for c in range(2)]
    out_t = norm[0] - lam * norm[1]
    o_ref[...] = (_rms_norm(out_t.T, nw_ref[...]) * (1.0 - lam_init)).astype(o_ref.dtype)


def _diff_attn(q, k, vt, lam_vec, norm_w, lam_init, bsz, seq):
    t, w = q.shape
    hw = 2 * D_HEAD_DIM
    heads = w // hw
    blk = min(DIFF_BLOCK, seq)
    nq = seq // blk
    return pl.pallas_call(
        functools.partial(_diff_kernel, blk=blk, hd=D_HEAD_DIM, lam_init=lam_init),
        grid=(bsz, heads, nq),
        in_specs=[pl.BlockSpec((blk, hw), lambda b, h, i: (b * nq + i, h)),
                  pl.BlockSpec((seq, hw), lambda b, h, i: (b, h)),
                  pl.BlockSpec((hw, seq), lambda b, h, i: (h, b)),
                  _const_spec(lam_vec.shape), _const_spec((1, hw))],
        out_specs=pl.BlockSpec((blk, hw), lambda b, h, i: (b * nq + i, h)),
        out_shape=jax.ShapeDtypeStruct((t, w), BF16),
        scratch_shapes=[pltpu.VMEM((2, hw + DIFF_ONES_ROWS, blk), F32), pltpu.VMEM((4, blk, blk), F32)],
        compiler_params=_params("arbitrary", "arbitrary", "arbitrary"),
        name="diff_attn_mixer",
    )(q, k, vt, lam_vec, norm_w.reshape(1, hw))


def _pad_cols(w, width):
    return jnp.pad(w, ((0, 0), (0, width - w.shape[1])))


def kernel(x, c, positions, ada_w, ada_b, ln_g, ln_b, mlstm_w_in, mlstm_b_gate, mlstm_norm, mlstm_w_out, swa_w_in, swa_sinks, swa_w_out, hgrn_w_in, hgrn_lower_bounds, hgrn_norm, hgrn_w_out, diff_w_in, diff_lambda, diff_norm, diff_w_out, ffn_w_in, ffn_w_out):
    bsz, seq, d = x.shape
    depth = ada_w.shape[0]
    alpha = (2 * depth) ** 0.25
    t = bsz * seq
    xf = x.reshape(t, d)
    mod = _modulation(c, ada_w, ada_b)
    cos_t, sin_t = _rope_tables(positions)
    for i in range(depth):
        mixer, j = i % 4, i // 4
        sub = 2 * i
        if mixer == 0:
            vw = mlstm_w_out.shape[1]
            qk = vw // 2
            n_gate = 2 * A_HEADS
            w_in = mlstm_w_in[j]
            w = _pad_cols(jnp.concatenate([w_in[:, :2 * qk], w_in[:, 2 * qk + 2 * vw:]], axis=1),
                          2 * qk + LANES).astype(BF16)
            w_vt = w_in[:, 2 * qk:2 * qk + vw].T.astype(BF16)
            w_ot = w_in[:, 2 * qk + vw:2 * qk + 2 * vw].T.astype(BF16)
            bg = _pad_cols(mlstm_b_gate[j].reshape(1, n_gate), LANES)
            q, k, vt, ot, g = _in_proj(
                functools.partial(_mlstm_in_kernel, qk=qk), xf, mod, sub, seq, w, [w_vt, w_ot, bg],
                [_const_spec(w_vt.shape), _const_spec(w_ot.shape), _const_spec((1, LANES))],
                [(qk, BF16, False), (qk, BF16, False), (vw, BF16, True), (vw, F32, True), (LANES, F32, False)])
            y = _mlstm(q, k, vt, ot, g, mlstm_norm[j], bsz, seq)
            w_out = mlstm_w_out[j]
        elif mixer == 1:
            qw = swa_w_out.shape[1]
            kw = qw // B_GROUP
            w_qk = swa_w_in[j][:, :qw + kw].astype(BF16)
            w_vt = swa_w_in[j][:, qw + kw:].T.astype(BF16)
            q, k2, vt = _in_proj(
                functools.partial(_attn_in_vt_kernel, qw=qw, kw=kw, scale=B_HEAD_DIM ** -0.5 * LOG2_E, kdup=True),
                xf, mod, sub, seq, w_qk, [w_vt, cos_t, sin_t],
                [_const_spec(w_vt.shape)] + [pl.BlockSpec((min(TOKEN_TILE, seq), LANES), lambda i: (i, 0))] * 2,
                [(qw, BF16, False), (2 * kw, BF16, False), (kw, BF16, True)])
            y = _swa(q, k2, vt, swa_sinks[j], bsz, seq)
            w_out = swa_w_out[j]
        elif mixer == 2:
            vw = hgrn_w_out.shape[1]
            q, f, v, g = _in_proj(
                functools.partial(_hgrn_in_kernel, kw=vw, vw=vw), xf, mod, sub, seq,
                hgrn_w_in[j].astype(BF16), [], [],
                [(vw, F32, False), (vw, F32, False), (vw, BF16, False), (vw, F32, False)])
            y = _hgrn(q, f, v, g, hgrn_lower_bounds.astype(F32), i, hgrn_norm[j], bsz, seq)
            w_out = hgrn_w_out[j]
        else:
            w_ = diff_w_out.shape[1]
            lam_init = 0.8 - 0.6 * math.exp(-0.3 * i)
            w_qk = diff_w_in[j][:, :2 * w_].astype(BF16)
            w_vt = diff_w_in[j][:, 2 * w_:].T.astype(BF16)
            q, k, vt = _in_proj(
                functools.partial(_attn_in_vt_kernel, qw=w_, kw=w_, scale=D_HEAD_DIM ** -0.5 * math.log2(math.e)),
                xf, mod, sub, seq,
                w_qk, [w_vt, cos_t, sin_t],
                [_const_spec(w_vt.shape)] + [pl.BlockSpec((min(TOKEN_TILE, seq), LANES), lambda i: (i, 0))] * 2,
                [(w_, BF16, False), (w_, BF16, False), (w_, BF16, True)])
            y = _diff_attn(q, k, vt, diff_lambda[j].astype(F32), diff_norm[j], lam_init, bsz, seq)
            w_out = diff_w_out[j]
        xf = _layer_tail(y, w_out.astype(BF16), xf, mod, sub, seq, ffn_w_in[i].astype(BF16),
                         ffn_w_out[i].astype(BF16), ln_g[i], ln_b[i], alpha)
    return xf.reshape(bsz, seq, d)
```

```python
import functools
import math

import jax
import jax.numpy as jnp
from jax import lax
from jax.experimental import pallas as pl
from jax.experimental.pallas import tpu as pltpu

F32 = jnp.float32
BF16 = jnp.bfloat16

A_HEADS = 4
B_HEAD_DIM = 64
B_GROUP = 8
B_BLOCK = 128
C_HEADS = 8
D_HEAD_DIM = 64
ROPE_THETA = 500000.0
ROT_FRAC = 4
LN_EPS = 1e-5
RMS_EPS = 1e-6
LANES = 128
MXU_COLS = 256
LOG2_E = math.log2(math.e)
VMEM_LIMIT = 52 * 2 ** 20

TOKEN_TILE = 512
TAIL_TILE = 512
TAIL_PARTS = 2
FFN_MAX_CHUNK = 4096
MLSTM_CHUNK = 128
MLSTM_STEP = 512
HGRN_CHUNK = 16
HGRN_STEP = 512
HGRN_CUMSUM_ROWS = 128
HGRN_UNROLL = 8
HGRN_SKEW = 1
DIFF_BLOCK = 512
ONES_ROWS = 16
SWA_HEADS_PER_DOT = 8


def _params(*sem):
    return pltpu.CompilerParams(dimension_semantics=sem, vmem_limit_bytes=VMEM_LIMIT)


def _dot(a, b):
    return jnp.dot(a, b, preferred_element_type=F32)


def _dot_nt(a, b):
    return lax.dot_general(a, b, (((1,), (1,)), ((), ())), preferred_element_type=F32)


def _split3(x):
    hi = x.astype(BF16)
    r = x - hi.astype(F32)
    mid = r.astype(BF16)
    lo = (r - mid.astype(F32)).astype(BF16)
    return hi, mid, lo


def _dot_exact_lhs01(mask_bf16, x):
    hi, mid, lo = _split3(x)
    return _dot(mask_bf16, hi) + (_dot(mask_bf16, mid) + _dot(mask_bf16, lo))


def _sigmoid(x):
    return 1.0 / (1.0 + jnp.exp(-x))


def _log_sigmoid(x):
    return jnp.minimum(x, 0.0) - jnp.log1p(jnp.exp(-jnp.abs(x)))


def _layer_norm(z, g, b):
    mu = jnp.mean(z, axis=-1, keepdims=True)
    zc = z - mu
    var = jnp.mean(zc * zc, axis=-1, keepdims=True)
    return zc * lax.rsqrt(var + LN_EPS) * g + b


def _rms_norm(h, w):
    return h * lax.rsqrt(jnp.mean(h * h, axis=-1, keepdims=True) + RMS_EPS) * w


def _const_spec(shape):
    nd = len(shape)
    return pl.BlockSpec(shape, lambda *_: (0,) * nd)


def _mod_spec(sub, part, tiles_per_batch, d):
    return pl.BlockSpec((None, None, None, 1, d), lambda i, *_: (sub, part, i // tiles_per_batch, 0, 0))


def _mod_kernel(c_ref, w_ref, b_ref, o_ref):
    c = c_ref[...]
    cond = c * _sigmoid(c)
    w = w_ref[...]
    a_hi = cond.astype(BF16)
    a_lo = (cond - a_hi.astype(F32)).astype(BF16)
    w_hi = w.astype(BF16)
    w_lo = (w - w_hi.astype(F32)).astype(BF16)
    acc = _dot(a_hi, w_hi) + (_dot(a_hi, w_lo) + _dot(a_lo, w_hi))
    o_ref[...] = acc + b_ref[...]


def _modulation(c, ada_w, ada_b):
    depth, _, d, d3 = ada_w.shape
    bsz = c.shape[0]
    nsub = depth * 2
    w = ada_w.reshape(nsub, d, d3)
    b = ada_b.reshape(nsub, 1, d3)
    out = pl.pallas_call(
        _mod_kernel,
        grid=(nsub, 3),
        in_specs=[
            pl.BlockSpec((bsz, d), lambda l, j: (0, 0)),
            pl.BlockSpec((None, d, d), lambda l, j: (l, 0, j)),
            pl.BlockSpec((None, 1, d), lambda l, j: (l, 0, j)),
        ],
        out_specs=pl.BlockSpec((None, None, bsz, d), lambda l, j: (l, j, 0, 0)),
        out_shape=jax.ShapeDtypeStruct((nsub, 3, bsz, d), F32),
        compiler_params=_params("arbitrary", "arbitrary"),
        name="adaln_mod",
    )(c, w, b)
    return out.reshape(nsub, 3, bsz, 1, d)


def _rope_kernel(pos_ref, inv_ref, ecos_ref, esin_ref, base_ref, cos_ref, sin_ref):
    ang = inv_ref[...] * pos_ref[...].astype(F32)
    cos_ref[...] = _spread(jnp.cos(ang), ecos_ref[...]) + base_ref[...]
    sin_ref[...] = _spread(jnp.sin(ang), esin_ref[...])


def _spread(vals_t, e):
    hi, mid, lo = _split3(vals_t.T)
    return _dot(hi, e) + (_dot(mid, e) + _dot(lo, e))


def _rope_tables(positions):
    t = positions.size
    rot = B_HEAD_DIM // ROT_FRAC
    half = rot // 2
    rows = 16
    inv = jnp.power(ROPE_THETA, -jnp.arange(half, dtype=F32) * 2.0 / rot)
    inv_c = jnp.zeros((rows, 1), F32).at[:half, 0].set(inv)
    lane = jnp.arange(LANES) % B_HEAD_DIM
    hit = (jnp.arange(rows)[:, None] == (lane % half)[None, :]) & (lane < rot)[None, :]
    e_cos = hit.astype(BF16)
    e_sin = (hit * jnp.where(lane < half, -1.0, 1.0)[None, :]).astype(BF16)
    base = (lane >= rot).astype(F32).reshape(1, LANES)
    tm = min(TOKEN_TILE, t)
    return pl.pallas_call(
        _rope_kernel,
        grid=(t // tm,),
        in_specs=[pl.BlockSpec((1, tm), lambda i: (0, i)), _const_spec((rows, 1)), _const_spec((rows, LANES)),
                  _const_spec((rows, LANES)), _const_spec((1, LANES))],
        out_specs=[pl.BlockSpec((tm, LANES), lambda i: (i, 0))] * 2,
        out_shape=[jax.ShapeDtypeStruct((t, LANES), F32)] * 2,
        compiler_params=_params("arbitrary"),
        name="rope_tables",
    )(positions.reshape(1, t), inv_c, e_cos, e_sin, base)


def _rope_slab(slab, cos, sin_signed, lo_mask):
    fwd = pltpu.roll(slab, LANES - 8, axis=1)
    bwd = pltpu.roll(slab, 8, axis=1)
    return slab * cos + jnp.where(lo_mask, fwd, bwd) * sin_signed


def _rope_lo_mask(tm):
    lane = lax.broadcasted_iota(jnp.int32, (tm, LANES), 1) % B_HEAD_DIM
    return lane < (B_HEAD_DIM // ROT_FRAC // 2)


def _modulated(x_ref, sh_ref, sc_ref):
    return (x_ref[...] * (1.0 + sc_ref[...]) + sh_ref[...]).astype(BF16)


def _mlstm_in_kernel(x_ref, sh_ref, sc_ref, w_ref, wvt_ref, wot_ref, bg_ref, q_ref, k_ref, vt_ref, ot_ref, g_ref,
                     *, qk):
    h = _modulated(x_ref, sh_ref, sc_ref)
    dk = qk // A_HEADS
    q_ref[...] = _dot(h, w_ref[:, :qk]).astype(BF16)
    k_ref[...] = (_dot(h, w_ref[:, qk:2 * qk]) * (dk ** -0.5)).astype(BF16)
    vt_ref[...] = _dot_nt(wvt_ref[...], h).astype(BF16)
    ot_ref[...] = _sigmoid(_dot_nt(wot_ref[...], h))
    g_ref[...] = _dot(h, w_ref[:, 2 * qk:]) + bg_ref[...]


def _attn_in_body(x_ref, sh_ref, sc_ref, w_ref, wvt_ref, cos_ref, sin_ref, q_ref, k_ref, v_ref, *,
                  qw, kw, scale, kdup=False):
    h = _modulated(x_ref, sh_ref, sc_ref)
    cos = cos_ref[...]
    sin = sin_ref[...]
    lo = _rope_lo_mask(h.shape[0])
    vw = 0 if wvt_ref is not None else w_ref.shape[1] - qw - kw
    first_head = lax.broadcasted_iota(jnp.int32, (h.shape[0], LANES), 1) < B_HEAD_DIM

    def emit(col, slab):
        if col < qw:
            q_ref[:, col:col + LANES] = (_rope_slab(slab, cos, sin, lo) * scale).astype(BF16)
        elif col < qw + kw and kdup:
            r = _rope_slab(slab, cos, sin, lo)
            sw = pltpu.roll(r, B_HEAD_DIM, axis=1)
            c = 2 * (col - qw)
            k_ref[:, c:c + LANES] = jnp.where(first_head, r, sw).astype(BF16)
            k_ref[:, c + LANES:c + 2 * LANES] = jnp.where(first_head, sw, r).astype(BF16)
        elif col < qw + kw:
            k_ref[:, col - qw:col - qw + LANES] = _rope_slab(slab, cos, sin, lo).astype(BF16)
        else:
            v_ref[:, col - qw - kw:col - qw - kw + LANES] = slab.astype(BF16)

    total = qw + kw + vw
    for c0 in range(0, total, MXU_COLS):
        width = min(MXU_COLS, total - c0)
        wide = _dot(h, w_ref[:, c0:c0 + width])
        for off in range(0, width, LANES):
            emit(c0 + off, wide[:, off:off + LANES])
    if wvt_ref is not None:
        v_ref[...] = _dot_nt(wvt_ref[...], h).astype(BF16)


def _attn_in_vt_kernel(x_ref, sh_ref, sc_ref, w_ref, wvt_ref, cos_ref, sin_ref, q_ref, k_ref, v_ref, **kw):
    _attn_in_body(x_ref, sh_ref, sc_ref, w_ref, wvt_ref, cos_ref, sin_ref, q_ref, k_ref, v_ref, **kw)


def _hgrn_in_kernel(x_ref, sh_ref, sc_ref, w_ref, q_ref, f_ref, v_ref, g_ref, *, kw, vw):
    h = _modulated(x_ref, sh_ref, sc_ref)
    q_ref[...] = _dot(h, w_ref[:, :kw])
    f_ref[...] = _dot(h, w_ref[:, kw:2 * kw])
    v_ref[...] = _dot(h, w_ref[:, 2 * kw:2 * kw + vw]).astype(BF16)
    g = _dot(h, w_ref[:, 2 * kw + vw:])
    g_ref[...] = g * _sigmoid(g)


def _in_proj(body, x, mod, sub, seq, w, extra, extra_specs, outs):
    t, d = x.shape
    tm = min(TOKEN_TILE, seq)
    tpb = seq // tm
    row = lambda width: pl.BlockSpec((tm, width), lambda i: (i, 0))
    col = lambda width: pl.BlockSpec((width, tm), lambda i: (0, i))
    return pl.pallas_call(
        body,
        grid=(t // tm,),
        in_specs=[row(d), _mod_spec(sub, 0, tpb, d), _mod_spec(sub, 1, tpb, d), _const_spec(w.shape)] + extra_specs,
        out_specs=[col(wd) if tr else row(wd) for wd, _, tr in outs],
        out_shape=[jax.ShapeDtypeStruct((wd, t) if tr else (t, wd), dt) for wd, dt, tr in outs],
        compiler_params=_params("arbitrary"),
        name=body.func.__name__.strip("_") if isinstance(body, functools.partial) else body.__name__.strip("_"),
    )(x, mod, mod, w, *extra)


def _tail_kernel(y_ref, wp_ref, x_ref, gate0_ref, g0_ref, b0_ref, sh_ref, sc_ref, gate1_ref, wi_ref, wo_ref,
                 g1_ref, b1_ref, o_ref, *, hidden, chunk, alpha):
    hm = x_ref.shape[0] // TAIL_PARTS
    rows = [slice(p * hm, (p + 1) * hm) for p in range(TAIL_PARTS)]
    nchunk = hidden // chunk

    def norm1(i, y):
        return _layer_norm(alpha * x_ref[rows[i], :] + (1.0 + gate0_ref[...]) * y, g0_ref[...], b0_ref[...])

    def ffn_part(h, j):
        a = _dot(h, wi_ref[:, j * chunk:(j + 1) * chunk])
        u = _dot(h, wi_ref[:, hidden + j * chunk:hidden + (j + 1) * chunk])
        act = (a * _sigmoid(a) * u).astype(BF16)
        return _dot(act, wo_ref[j * chunk:(j + 1) * chunk, :])

    def norm2(i, x, acc):
        o_ref[rows[i], :] = _layer_norm(alpha * x + (1.0 + gate1_ref[...]) * acc, g1_ref[...], b1_ref[...])

    ys, xs, hs, accs = {}, {}, {}, {}
    for i in range(TAIL_PARTS + 2):
        if i < TAIL_PARTS:
            ys[i] = _dot(y_ref[rows[i], :], wp_ref[...])
        if 0 <= i - 1 < TAIL_PARTS:
            accs[i - 1] = ffn_part(hs[i - 1], 0)
        if i < TAIL_PARTS:
            xs[i] = norm1(i, ys[i])
            hs[i] = (xs[i] * (1.0 + sc_ref[...]) + sh_ref[...]).astype(BF16)
        if 0 <= i - 1 < TAIL_PARTS:
            for j in range(1, nchunk):
                accs[i - 1] = accs[i - 1] + ffn_part(hs[i - 1], j)
        if 0 <= i - 2 < TAIL_PARTS:
            norm2(i - 2, xs[i - 2], accs[i - 2])


def _ffn_chunk(hidden):
    best = None
    for c in range(MXU_COLS, min(hidden, FFN_MAX_CHUNK) + 1, MXU_COLS):
        if hidden % c == 0:
            best = c
    assert best is not None, hidden
    return best


def _resident_spec(shape):
    nd = len(shape)
    return pl.BlockSpec(shape, lambda *_: (0,) * nd, pipeline_mode=pl.Buffered(1))


def _layer_tail(y, w_proj, x, mod, sub, seq, w_in, w_out, ln_g, ln_b, alpha):
    t, d = x.shape
    hidden = w_out.shape[0]
    tm = min(TAIL_TILE, seq)
    tpb = seq // tm
    row = lambda width: pl.BlockSpec((tm, width), lambda i: (i, 0))
    vec = _const_spec((1, d))
    return pl.pallas_call(
        functools.partial(_tail_kernel, hidden=hidden, chunk=_ffn_chunk(hidden), alpha=alpha),
        grid=(t // tm,),
        in_specs=[row(y.shape[1]), _resident_spec(w_proj.shape), row(d), _mod_spec(sub, 2, tpb, d), vec, vec,
                  _mod_spec(sub + 1, 0, tpb, d), _mod_spec(sub + 1, 1, tpb, d), _mod_spec(sub + 1, 2, tpb, d),
                  _resident_spec(w_in.shape), _resident_spec(w_out.shape), vec, vec],
        out_specs=row(d),
        out_shape=jax.ShapeDtypeStruct((t, d), F32),
        compiler_params=_params("arbitrary"),
        name="out_proj_ffn",
    )(y, w_proj, x, mod, ln_g[0].reshape(1, d), ln_b[0].reshape(1, d), mod, mod, mod, w_in, w_out,
      ln_g[1].reshape(1, d), ln_b[1].reshape(1, d))


def _mlstm_kernel(q_ref, k_ref, vt_ref, ot_ref, g_ref, nw_ref, y_ref, ct_ref, m_ref, z_ref, zt_ref, *,
                  chunk, dk, dv):
    L = chunk
    nchunks = q_ref.shape[0] // L

    @pl.when(pl.program_id(1) == 0)
    def _():
        ct_ref[...] = jnp.zeros_like(ct_ref)
        m_ref[...] = jnp.zeros_like(m_ref)

    key_i = lax.broadcasted_iota(jnp.int32, (L, L), 0)
    qry_i = lax.broadcasted_iota(jnp.int32, (L, L), 1)
    causal_t = key_i <= qry_i

    tril = jnp.where(qry_i <= key_i, 1.0, 0.0).astype(BF16)
    lane = lax.broadcasted_iota(jnp.int32, (L, LANES), 1)
    for r0 in range(0, g_ref.shape[0], L):
        g = g_ref[r0:r0 + L, :]
        logf = jnp.where(lane >= A_HEADS, _log_sigmoid(g), 0.0)
        z = jnp.where(lane < A_HEADS, g, _dot_exact_lhs01(tril, logf))
        z_ref[r0:r0 + L, :] = z
        zt_ref[:, r0:r0 + L] = z.T

    def body(c, carry):
        start = pl.multiple_of(c * L, L)
        rows = pl.ds(start, L)
        z = z_ref[rows, :]
        early = []
        for h in range(A_HEADS):
            i_col = z[:, h:h + 1]
            b_col = z[:, A_HEADS + h:A_HEADS + h + 1]
            m_prev = m_ref[h, 0:1, 0:1]
            qh = q_ref[rows, h * dk:(h + 1) * dk]
            kh = k_ref[rows, h * dk:(h + 1) * dk]
            vt = vt_ref[h * dv:(h + 1) * dv, pl.ds(start, L)]
            s_t = _dot_nt(kh, qh)
            ct = ct_ref[h]
            inter = _dot_nt(ct.astype(BF16), qh)
            b_last = b_col[L - 1:L, :]
            g_col = b_last - b_col + i_col
            m_new = jnp.maximum(b_last + m_prev, jnp.max(g_col, axis=0, keepdims=True))
            decay = jnp.exp(b_last + m_prev - m_new)
            kws = kh.astype(F32) * jnp.exp(g_col - m_new)
            ct_ref[h, :dv, :] = decay * ct[:dv] + _dot(vt, kws.astype(BF16))
            ct_ref[h, dv:dv + 1, :] = decay * ct[dv:dv + 1] + jnp.sum(kws, axis=0, keepdims=True)
            m_ref[h] = jnp.broadcast_to(m_new, m_ref.shape[1:])
            early.append((i_col - b_col, m_prev, vt, s_t, inter))
        for h in range(A_HEADS):
            ib_col, m_prev, vt, s_t, inter = early[h]
            b_row = zt_ref[A_HEADS + h:A_HEADS + h + 1, pl.ds(start, L)]
            d_t = jnp.where(causal_t, b_row + ib_col, -jnp.inf)
            m_inter = b_row + m_prev
            m_t = jnp.maximum(m_inter, jnp.max(d_t, axis=0, keepdims=True))
            a_t = s_t * jnp.exp(d_t - m_t)
            w_inter = jnp.exp(m_inter - m_t)
            num_t = _dot(vt, a_t.astype(BF16)) + w_inter * inter[:dv]
            den = jnp.sum(a_t, axis=0, keepdims=True) + w_inter * inter[dv:dv + 1]
            hout_t = num_t * (1.0 / jnp.maximum(jnp.abs(den), jnp.exp(-m_t)))
            ms = jnp.mean(hout_t * hout_t, axis=0, keepdims=True)
            y_t = (hout_t * lax.rsqrt(ms + RMS_EPS) * nw_ref[h * dv:(h + 1) * dv, :]
                   * ot_ref[h * dv:(h + 1) * dv, pl.ds(start, L)])
            y_ref[rows, h * dv:(h + 1) * dv] = y_t.T.astype(y_ref.dtype)
        return carry

    lax.fori_loop(0, nchunks, body, 0, unroll=True)


def _mlstm(q, k, vt, ot, g, norm_w, bsz, seq):
    t, qk = q.shape
    vw = vt.shape[0]
    dk, dv = qk // A_HEADS, vw // A_HEADS
    ts = min(MLSTM_STEP, seq)
    spb = seq // ts
    chunk = min(MLSTM_CHUNK, ts)
    row = lambda width: pl.BlockSpec((ts, width), lambda b, s: (b * spb + s, 0))
    col = pl.BlockSpec((vw, ts), lambda b, s: (0, b * spb + s))
    nw_cols = jnp.broadcast_to(norm_w.astype(F32).reshape(vw, 1), (vw, chunk))
    return pl.pallas_call(
        functools.partial(_mlstm_kernel, chunk=chunk, dk=dk, dv=dv),
        grid=(bsz, spb),
        in_specs=[row(qk), row(qk), col, col, row(LANES), _const_spec((vw, chunk))],
        out_specs=row(vw),
        out_shape=jax.ShapeDtypeStruct((t, vw), BF16),
        scratch_shapes=[pltpu.VMEM((A_HEADS, dv + 8, dk), F32), pltpu.VMEM((A_HEADS, 8, LANES), F32),
                        pltpu.VMEM((ts, LANES), F32), pltpu.VMEM((LANES, ts), F32)],
        compiler_params=_params("arbitrary", "arbitrary"),
        name="mlstm_mixer",
    )(q, k, vt, ot, g, nw_cols)


def _swa_kernel(sink_ref, q_ref, kc_ref, kp_ref, vtc_ref, vtp_ref, o_ref, st_ref, *, heads, hd, group):
    n = pl.program_id(1)
    blk = q_ref.shape[0]
    hpd = SWA_HEADS_PER_DOT
    key_i = lax.broadcasted_iota(jnp.int32, (2 * blk, blk), 0)
    qry_i = lax.broadcasted_iota(jnp.int32, (2 * blk, blk), 1)
    valid = (key_i > qry_i) & (key_i <= qry_i + blk) & ((key_i >= blk) | (n > 0))
    bias = jnp.concatenate([jnp.where(valid, 0.0, -jnp.inf)] * hpd, axis=1)
    upper = lax.broadcasted_iota(jnp.int32, (blk, LANES), 1) >= hd
    ones = jnp.ones((ONES_ROWS, 2 * blk), BF16)

    def scores(j0):
        gi = j0 // group
        ks = slice(gi * LANES, (gi + 1) * LANES)
        k2 = jnp.concatenate([kp_ref[:, ks], kc_ref[:, ks]], axis=0)
        qz = []
        for j in range(j0, j0 + hpd):
            slab = q_ref[:, (j // 2) * LANES:(j // 2 + 1) * LANES]
            own = upper if j % 2 else jnp.logical_not(upper)
            qz.append(jnp.where(own, slab, jnp.zeros_like(slab)))
        return _dot_nt(k2, jnp.concatenate(qz, axis=0))

    for idx, j0 in enumerate(range(0, heads, hpd)):
        st_ref[idx] = scores(j0)
    for idx, j0 in enumerate(range(0, heads, hpd)):
        gi = j0 // group
        st = st_ref[idx] + bias
        off = (j0 % group) * blk
        sink = sink_ref[gi:gi + 1, off:off + hpd * blk] * LOG2_E
        m = jnp.maximum(jnp.max(st, axis=0, keepdims=True), sink)
        p = jnp.exp2(st - m).astype(BF16)
        vs = slice(gi * hd, (gi + 1) * hd)
        vte = jnp.concatenate([jnp.concatenate([vtp_ref[vs, :], vtc_ref[vs, :]], axis=1), ones], axis=0)
        acc = _dot(vte, p)
        ot = acc[:hd, :] * (1.0 / (acc[hd:hd + 1, :] + jnp.exp2(sink - m)))
        for pr in range(hpd // 2):
            pair = jnp.concatenate([ot[:, (2 * pr) * blk:(2 * pr + 1) * blk],
                                    ot[:, (2 * pr + 1) * blk:(2 * pr + 2) * blk]], axis=0)
            c0 = (j0 // 2 + pr) * LANES
            o_ref[:, c0:c0 + LANES] = pair.T.astype(o_ref.dtype)


def _swa(q, k2, vt, sinks, bsz, seq):
    t, qw = q.shape
    kw = vt.shape[0]
    heads = qw // B_HEAD_DIM
    blk = min(B_BLOCK, seq)
    nb = seq // blk
    sink_rows = jnp.repeat(sinks.astype(F32).reshape(heads // B_GROUP, B_GROUP), blk, axis=1)
    cur = lambda width: pl.BlockSpec((blk, width), lambda b, n: (b * nb + n, 0))
    prev = lambda width: pl.BlockSpec((blk, width), lambda b, n: (b * nb + jnp.maximum(n - 1, 0), 0))
    cur_t = pl.BlockSpec((kw, blk), lambda b, n: (0, b * nb + n))
    prev_t = pl.BlockSpec((kw, blk), lambda b, n: (0, b * nb + jnp.maximum(n - 1, 0)))
    return pl.pallas_call(
        functools.partial(_swa_kernel, heads=heads, hd=B_HEAD_DIM, group=B_GROUP),
        grid=(bsz, nb),
        in_specs=[_const_spec(sink_rows.shape), cur(qw), cur(2 * kw), prev(2 * kw), cur_t, prev_t],
        out_specs=cur(qw),
        out_shape=jax.ShapeDtypeStruct((t, qw), BF16),
        scratch_shapes=[pltpu.VMEM((heads // SWA_HEADS_PER_DOT, 2 * blk, SWA_HEADS_PER_DOT * blk), F32)],
        compiler_params=_params("arbitrary", "arbitrary"),
        name="swa_mixer",
    )(sink_rows, q, k2, k2, vt, vt)


def _hgrn_kernel(q_ref, f_ref, v_ref, g_ref, lbp_ref, nw_ref, y_ref, st_ref, cf_ref, ck_ref,
                 cfc_ref, ckc_ref, vsc_ref, *,
                 chunk, dk, dv, layer):
    L = chunk
    nchunks = q_ref.shape[0] // L
    heads = q_ref.shape[1] // dk

    @pl.when(pl.program_id(1) == 0)
    def _():
        st_ref[...] = jnp.zeros_like(st_ref)

    lbp = lbp_ref[...]
    e = jnp.exp(lbp - jnp.max(lbp, axis=0, keepdims=True))
    sm = e / jnp.sum(e, axis=0, keepdims=True)
    lb = jnp.zeros_like(sm[0:1])
    for r in range(1, layer + 1):
        lb = lb + sm[r:r + 1]
    log2_lb = jnp.log(lb) * LOG2_E
    log2_1m_lb = jnp.log1p(-lb) * LOG2_E

    ones = jnp.ones((dk, LANES), BF16)
    half = L // 2
    trow = lax.broadcasted_iota(jnp.int32, (half, dv), 0)

    rb = min(HGRN_CUMSUM_ROWS, q_ref.shape[0])
    row_i = lax.broadcasted_iota(jnp.int32, (rb, rb), 0)
    col_i = lax.broadcasted_iota(jnp.int32, (rb, rb), 1)
    tril = jnp.where((col_i <= row_i) & (col_i // L == row_i // L), 1.0, 0.0).astype(BF16)
    for r0 in range(0, q_ref.shape[0], rb):
        f2 = f_ref[r0:r0 + rb, :] * LOG2_E
        ls = jnp.minimum(f2, 0.0) - jnp.log2(1.0 + jnp.exp2(-jnp.abs(f2)))
        x2 = log2_1m_lb + ls
        logf = jnp.maximum(log2_lb, x2) + jnp.log2(1.0 + jnp.exp2(-jnp.abs(log2_lb - x2)))
        cf = _dot_exact_lhs01(tril, logf)
        cf_ref[r0:r0 + rb, :] = cf
        ck_ref[r0:r0 + rb, :] = cf - (x2 - f2)

    unroll = cfc_ref.shape[0]

    def body(c, carry):
        rows = [pl.ds(pl.multiple_of((c * unroll + u) * L, L), L) for u in range(unroll)]
        for u in range(unroll):
            cfc_ref[u] = cf_ref[rows[u], :]
            ckc_ref[u] = ck_ref[rows[u], :]
            vsc_ref[u] = v_ref[rows[u], :].astype(F32)
        def issue(h, u):
            cs = slice(h * dk, (h + 1) * dk)
            vs = slice(h * dv, (h + 1) * dv)
            cfh = cfc_ref[u, :, cs]
            qh = q_ref[rows[u], cs]
            ws = []
            for s in range(L):
                t0 = 0 if s < half else half
                ws.append(qh[t0:] * jnp.exp2(cfh[t0:] - ckc_ref[u, s:s + 1, cs]))
            w_all = jnp.concatenate(ws, axis=0).astype(BF16)
            r_all = _dot(w_all, ones)
            st = st_ref[h]
            inter = _dot_nt((qh * jnp.exp2(cfh)).astype(BF16), st.astype(BF16))
            last = cfc_ref[u, L - 1:L, cs]
            kdec = jnp.exp2(last - ckc_ref[u, :, cs]).astype(BF16)
            st_ref[h] = jnp.exp2(last) * st + _dot(vsc_ref[u, :, vs].T.astype(BF16), kdec)
            return r_all, inter

        def finish(h, u, r_all, o):
            vs = slice(h * dv, (h + 1) * dv)
            o_lo, o_hi = o[:half], o[half:]
            for s in range(half):
                r = r_all[s * L:(s + 1) * L, :]
                v_s = vsc_ref[u, s:s + 1, vs]
                o_lo = o_lo + jnp.where(trow >= s, r[:half], 0.0) * v_s
                o_hi = o_hi + r[half:] * v_s
            for s in range(half, L):
                r = r_all[half * L + (s - half) * half:half * L + (s - half + 1) * half, :]
                o_hi = o_hi + jnp.where(trow >= s - half, r, 0.0) * vsc_ref[u, s:s + 1, vs]
            o = jnp.concatenate([o_lo, o_hi], axis=0)
            y = _rms_norm(o, nw_ref[:, vs]) * g_ref[rows[u], vs]
            y_ref[rows[u], vs] = y.astype(y_ref.dtype)

        work = [(h, u) for u in range(unroll) for h in range(heads)]
        pending = []
        for hu in work:
            pending.append((hu, issue(*hu)))
            if len(pending) > HGRN_SKEW:
                (h0, u0), (r0, i0) = pending.pop(0)
                finish(h0, u0, r0, i0)
        for (h0, u0), (r0, i0) in pending:
            finish(h0, u0, r0, i0)
        return carry

    lax.fori_loop(0, nchunks // unroll, body, 0)


def _hgrn(q, f, v, g, lower_bounds, layer, norm_w, bsz, seq):
    t, kw = q.shape
    vw = v.shape[1]
    dk, dv = kw // C_HEADS, vw // C_HEADS
    ts = min(HGRN_STEP, seq)
    spb = seq // ts
    chunk = min(HGRN_CHUNK, ts)
    unroll = math.gcd(HGRN_UNROLL, ts // chunk)
    row = lambda width: pl.BlockSpec((ts, width), lambda b, s: (b * spb + s, 0))
    return pl.pallas_call(
        functools.partial(_hgrn_kernel, chunk=chunk, dk=dk, dv=dv, layer=layer),
        grid=(bsz, spb),
        in_specs=[row(kw), row(kw), row(vw), row(vw), _const_spec(lower_bounds.shape), _const_spec((1, vw))],
        out_specs=row(vw),
        out_shape=jax.ShapeDtypeStruct((t, vw), BF16),
        scratch_shapes=[pltpu.VMEM((C_HEADS, dv, dk), F32), pltpu.VMEM((ts, kw), F32), pltpu.VMEM((ts, kw), F32),
                        pltpu.VMEM((unroll, chunk, kw), F32), pltpu.VMEM((unroll, chunk, kw), F32),
                        pltpu.VMEM((unroll, chunk, vw), F32)],
        compiler_params=_params("arbitrary", "arbitrary"),
        name="hgrn2_mixer",
    )(q, f, v, g, lower_bounds, norm_w.reshape(1, vw))


def _diff_kernel(q_ref, k_ref, vt_ref, lam_ref, nw_ref, o_ref, acc_ref, st_ref, *, blk, hd, lam_init):
    qi = pl.program_id(2)
    lv = lam_ref[...]
    lam = (jnp.exp(jnp.sum(lv[0:1] * lv[1:2], axis=-1, keepdims=True))
           - jnp.exp(jnp.sum(lv[2:3] * lv[3:4], axis=-1, keepdims=True)) + lam_init)
    q = q_ref[...]
    lane = lax.broadcasted_iota(jnp.int32, q.shape, 1)
    zero = jnp.zeros_like(q)
    qz = (jnp.where(lane < hd, q, zero), jnp.where(lane >= hd, q, zero))
    acc_ref[...] = jnp.zeros_like(acc_ref)

    hw = 2 * hd
    ones = jnp.ones((ONES_ROWS, blk), BF16)

    def scores_into(slot, j):
        kb = k_ref[pl.ds(pl.multiple_of(j * blk, blk), blk), :]
        for c in range(2):
            st_ref[2 * slot + c] = _dot_nt(kb, qz[c])

    def step(j, j_next, m, masked, src):
        if j_next is not None:
            scores_into(1 - src, j_next)
        vte = jnp.concatenate([vt_ref[:, pl.ds(pl.multiple_of(j * blk, blk), blk)], ones], axis=0)
        if masked:
            key_i = lax.broadcasted_iota(jnp.int32, (blk, blk), 0)
            qry_i = lax.broadcasted_iota(jnp.int32, (blk, blk), 1)
            keep = key_i <= qry_i
        m_out = []
        for c in range(2):
            st = st_ref[2 * src + c]
            if masked:
                st = jnp.where(keep, st, -jnp.inf)
            m_new = jnp.maximum(m[c], jnp.max(st, axis=0, keepdims=True))
            alpha = jnp.exp2(m[c] - m_new)
            p = jnp.exp2(st - m_new).astype(BF16)
            acc_ref[c] = alpha * acc_ref[c] + _dot(vte, p)
            m_out.append(m_new)
        return tuple(m_out)

    scores_into(0, qi)
    neg = jnp.full((1, blk), -jnp.inf, F32)
    m = step(qi, 0, (neg, neg), True, 0)

    def pair(i, cr):
        cr = step(2 * i, 2 * i + 1, cr, False, 1)
        return step(2 * i + 1, 2 * i + 2, cr, False, 0)

    m = lax.fori_loop(0, qi // 2, pair, m)

    @pl.when(qi % 2 == 1)
    def _():
        step(qi - 1, None, m, False, 1)

    norm = [acc_ref[c, :hw, :] * (1.0 / acc_ref[c, hw:hw + 1, :]) for c in range(2)]
    out_t = norm[0] - lam * norm[1]
    o_ref[...] = (_rms_norm(out_t.T, nw_ref[...]) * (1.0 - lam_init)).astype(o_ref.dtype)


def _diff_attn(q, k, vt, lam_vec, norm_w, lam_init, bsz, seq):
    t, w = q.shape
    hw = 2 * D_HEAD_DIM
    heads = w // hw
    blk = min(DIFF_BLOCK, seq)
    nq = seq // blk
    return pl.pallas_call(
        functools.partial(_diff_kernel, blk=blk, hd=D_HEAD_DIM, lam_init=lam_init),
        grid=(bsz, heads, nq),
        in_specs=[pl.BlockSpec((blk, hw), lambda b, h, i: (b * nq + i, h)),
                  pl.BlockSpec((seq, hw), lambda b, h, i: (b, h)),
                  pl.BlockSpec((hw, seq), lambda b, h, i: (h, b)),
                  _const_spec(lam_vec.shape), _const_spec((1, hw))],
        out_specs=pl.BlockSpec((blk, hw), lambda b, h, i: (b * nq + i, h)),
        out_shape=jax.ShapeDtypeStruct((t, w), BF16),
        scratch_shapes=[pltpu.VMEM((2, hw + ONES_ROWS, blk), F32), pltpu.VMEM((4, blk, blk), F32)],
        compiler_params=_params("arbitrary", "arbitrary", "arbitrary"),
        name="diff_attn_mixer",
    )(q, k, vt, lam_vec, norm_w.reshape(1, hw))


def _pad_cols(w, width):
    return jnp.pad(w, ((0, 0), (0, width - w.shape[1])))


def kernel(x, c, positions, ada_w, ada_b, ln_g, ln_b, mlstm_w_in, mlstm_b_gate, mlstm_norm, mlstm_w_out, swa_w_in, swa_sinks, swa_w_out, hgrn_w_in, hgrn_lower_bounds, hgrn_norm, hgrn_w_out, diff_w_in, diff_lambda, diff_norm, diff_w_out, ffn_w_in, ffn_w_out):
    bsz, seq, d = x.shape
    depth = ada_w.shape[0]
    alpha = (2 * depth) ** 0.25
    t = bsz * seq
    xf = x.reshape(t, d)
    mod = _modulation(c, ada_w, ada_b)
    cos_t, sin_t = _rope_tables(positions)
    for i in range(depth):
        mixer, j = i % 4, i // 4
        sub = 2 * i
        if mixer == 0:
            vw = mlstm_w_out.shape[1]
            qk = vw // 2
            n_gate = 2 * A_HEADS
            w_in = mlstm_w_in[j]
            w = _pad_cols(jnp.concatenate([w_in[:, :2 * qk], w_in[:, 2 * qk + 2 * vw:]], axis=1),
                          2 * qk + LANES).astype(BF16)
            w_vt = w_in[:, 2 * qk:2 * qk + vw].T.astype(BF16)
            w_ot = w_in[:, 2 * qk + vw:2 * qk + 2 * vw].T.astype(BF16)
            bg = _pad_cols(mlstm_b_gate[j].reshape(1, n_gate), LANES)
            q, k, vt, ot, g = _in_proj(
                functools.partial(_mlstm_in_kernel, qk=qk), xf, mod, sub, seq, w, [w_vt, w_ot, bg],
                [_const_spec(w_vt.shape), _const_spec(w_ot.shape), _const_spec((1, LANES))],
                [(qk, BF16, False), (qk, BF16, False), (vw, BF16, True), (vw, F32, True), (LANES, F32, False)])
            y = _mlstm(q, k, vt, ot, g, mlstm_norm[j], bsz, seq)
            w_out = mlstm_w_out[j]
        elif mixer == 1:
            qw = swa_w_out.shape[1]
            kw = qw // B_GROUP
            w_qk = swa_w_in[j][:, :qw + kw].astype(BF16)
            w_vt = swa_w_in[j][:, qw + kw:].T.astype(BF16)
            q, k2, vt = _in_proj(
                functools.partial(_attn_in_vt_kernel, qw=qw, kw=kw, scale=B_HEAD_DIM ** -0.5 * LOG2_E, kdup=True),
                xf, mod, sub, seq, w_qk, [w_vt, cos_t, sin_t],
                [_const_spec(w_vt.shape)] + [pl.BlockSpec((min(TOKEN_TILE, seq), LANES), lambda i: (i, 0))] * 2,
                [(qw, BF16, False), (2 * kw, BF16, False), (kw, BF16, True)])
            y = _swa(q, k2, vt, swa_sinks[j], bsz, seq)
            w_out = swa_w_out[j]
        elif mixer == 2:
            vw = hgrn_w_out.shape[1]
            q, f, v, g = _in_proj(
                functools.partial(_hgrn_in_kernel, kw=vw, vw=vw), xf, mod, sub, seq,
                hgrn_w_in[j].astype(BF16), [], [],
                [(vw, F32, False), (vw, F32, False), (vw, BF16, False), (vw, F32, False)])
            y = _hgrn(q, f, v, g, hgrn_lower_bounds.astype(F32), i, hgrn_norm[j], bsz, seq)
            w_out = hgrn_w_out[j]
        else:
            w_ = diff_w_out.shape[1]
            lam_init = 0.8 - 0.6 * math.exp(-0.3 * i)
            w_qk = diff_w_in[j][:, :2 * w_].astype(BF16)
            w_vt = diff_w_in[j][:, 2 * w_:].T.astype(BF16)
            q, k, vt = _in_proj(
                functools.partial(_attn_in_vt_kernel, qw=w_, kw=w_, scale=D_HEAD_DIM ** -0.5 * math.log2(math.e)),
                xf, mod, sub, seq,
                w_qk, [w_vt, cos_t, sin_t],
                [_const_spec(w_vt.shape)] + [pl.BlockSpec((min(TOKEN_TILE, seq), LANES), lambda i: (i, 0))] * 2,
                [(w_, BF16, False), (w_, BF16, False), (w_, BF16, True)])
            y = _diff_attn(q, k, vt, diff_lambda[j].astype(F32), diff_norm[j], lam_init, bsz, seq)
            w_out = diff_w_out[j]
        xf = _layer_tail(y, w_out.astype(BF16), xf, mod, sub, seq, ffn_w_in[i].astype(BF16),
                         ffn_w_out[i].astype(BF16), ln_g[i], ln_b[i], alpha)
    return xf.reshape(bsz, seq, d)
```

```python
import functools
import math

import jax
import jax.numpy as jnp
from jax import lax
from jax.experimental import pallas as pl
from jax.experimental.pallas import tpu as pltpu

F32 = jnp.float32
BF16 = jnp.bfloat16

A_HEADS = 4
B_HEAD_DIM = 64
B_GROUP = 8
B_BLOCK = 128
C_HEADS = 8
D_HEAD_DIM = 64
ROPE_THETA = 500000.0
ROT_FRAC = 4
LN_EPS = 1e-5
RMS_EPS = 1e-6
LANES = 128
MXU_COLS = 256
LOG2_E = math.log2(math.e)
VMEM_LIMIT = 52 * 2 ** 20

TOKEN_TILE = 512
TAIL_TILE = 512
TAIL_PARTS = 2
FFN_MAX_CHUNK = 4096
MLSTM_CHUNK = 128
MLSTM_STEP = 512
HGRN_CHUNK = 16
HGRN_STEP = 512
HGRN_CUMSUM_ROWS = 128
HGRN_UNROLL = 8
HGRN_SKEW = 1
DIFF_BLOCK = 512
ONES_ROWS = 16
SWA_HEADS_PER_DOT = 8


def _params(*sem):
    return pltpu.CompilerParams(dimension_semantics=sem, vmem_limit_bytes=VMEM_LIMIT)


def _dot(a, b):
    return jnp.dot(a, b, preferred_element_type=F32)


def _dot_nt(a, b):
    return lax.dot_general(a, b, (((1,), (1,)), ((), ())), preferred_element_type=F32)


def _split3(x):
    hi = x.astype(BF16)
    r = x - hi.astype(F32)
    mid = r.astype(BF16)
    lo = (r - mid.astype(F32)).astype(BF16)
    return hi, mid, lo


def _dot_exact_lhs01(mask_bf16, x):
    hi, mid, lo = _split3(x)
    return _dot(mask_bf16, hi) + (_dot(mask_bf16, mid) + _dot(mask_bf16, lo))


def _sigmoid(x):
    return 1.0 / (1.0 + jnp.exp(-x))


def _log_sigmoid(x):
    return jnp.minimum(x, 0.0) - jnp.log1p(jnp.exp(-jnp.abs(x)))


def _layer_norm(z, g, b):
    mu = jnp.mean(z, axis=-1, keepdims=True)
    zc = z - mu
    var = jnp.mean(zc * zc, axis=-1, keepdims=True)
    return zc * lax.rsqrt(var + LN_EPS) * g + b


def _rms_norm(h, w):
    return h * lax.rsqrt(jnp.mean(h * h, axis=-1, keepdims=True) + RMS_EPS) * w


def _const_spec(shape):
    nd = len(shape)
    return pl.BlockSpec(shape, lambda *_: (0,) * nd)


def _mod_spec(sub, part, tiles_per_batch, d):
    return pl.BlockSpec((None, None, None, 1, d), lambda i, *_: (sub, part, i // tiles_per_batch, 0, 0))


def _mod_kernel(c_ref, w_ref, b_ref, o_ref):
    c = c_ref[...]
    cond = c * _sigmoid(c)
    w = w_ref[...]
    a_hi = cond.astype(BF16)
    a_lo = (cond - a_hi.astype(F32)).astype(BF16)
    w_hi = w.astype(BF16)
    w_lo = (w - w_hi.astype(F32)).astype(BF16)
    acc = _dot(a_hi, w_hi) + (_dot(a_hi, w_lo) + _dot(a_lo, w_hi))
    o_ref[...] = acc + b_ref[...]


def _modulation(c, ada_w, ada_b):
    depth, _, d, d3 = ada_w.shape
    bsz = c.shape[0]
    nsub = depth * 2
    w = ada_w.reshape(nsub, d, d3)
    b = ada_b.reshape(nsub, 1, d3)
    out = pl.pallas_call(
        _mod_kernel,
        grid=(nsub, 3),
        in_specs=[
            pl.BlockSpec((bsz, d), lambda l, j: (0, 0)),
            pl.BlockSpec((None, d, d), lambda l, j: (l, 0, j)),
            pl.BlockSpec((None, 1, d), lambda l, j: (l, 0, j)),
        ],
        out_specs=pl.BlockSpec((None, None, bsz, d), lambda l, j: (l, j, 0, 0)),
        out_shape=jax.ShapeDtypeStruct((nsub, 3, bsz, d), F32),
        compiler_params=_params("arbitrary", "arbitrary"),
        name="adaln_mod",
    )(c, w, b)
    return out.reshape(nsub, 3, bsz, 1, d)


def _rope_kernel(pos_ref, inv_ref, ecos_ref, esin_ref, base_ref, cos_ref, sin_ref):
    ang = inv_ref[...] * pos_ref[...].astype(F32)
    cos_ref[...] = _spread(jnp.cos(ang), ecos_ref[...]) + base_ref[...]
    sin_ref[...] = _spread(jnp.sin(ang), esin_ref[...])


def _spread(vals_t, e):
    hi, mid, lo = _split3(vals_t.T)
    return _dot(hi, e) + (_dot(mid, e) + _dot(lo, e))


def _rope_tables(positions):
    t = positions.size
    rot = B_HEAD_DIM // ROT_FRAC
    half = rot // 2
    rows = 16
    inv = jnp.power(ROPE_THETA, -jnp.arange(half, dtype=F32) * 2.0 / rot)
    inv_c = jnp.zeros((rows, 1), F32).at[:half, 0].set(inv)
    lane = jnp.arange(LANES) % B_HEAD_DIM
    hit = (jnp.arange(rows)[:, None] == (lane % half)[None, :]) & (lane < rot)[None, :]
    e_cos = hit.astype(BF16)
    e_sin = (hit * jnp.where(lane < half, -1.0, 1.0)[None, :]).astype(BF16)
    base = (lane >= rot).astype(F32).reshape(1, LANES)
    tm = min(TOKEN_TILE, t)
    return pl.pallas_call(
        _rope_kernel,
        grid=(t // tm,),
        in_specs=[pl.BlockSpec((1, tm), lambda i: (0, i)), _const_spec((rows, 1)), _const_spec((rows, LANES)),
                  _const_spec((rows, LANES)), _const_spec((1, LANES))],
        out_specs=[pl.BlockSpec((tm, LANES), lambda i: (i, 0))] * 2,
        out_shape=[jax.ShapeDtypeStruct((t, LANES), F32)] * 2,
        compiler_params=_params("arbitrary"),
        name="rope_tables",
    )(positions.reshape(1, t), inv_c, e_cos, e_sin, base)


def _rope_slab(slab, cos, sin_signed, lo_mask):
    fwd = pltpu.roll(slab, LANES - 8, axis=1)
    bwd = pltpu.roll(slab, 8, axis=1)
    return slab * cos + jnp.where(lo_mask, fwd, bwd) * sin_signed


def _rope_lo_mask(tm):
    lane = lax.broadcasted_iota(jnp.int32, (tm, LANES), 1) % B_HEAD_DIM
    return lane < (B_HEAD_DIM // ROT_FRAC // 2)


def _modulated(x_ref, sh_ref, sc_ref):
    return (x_ref[...] * (1.0 + sc_ref[...]) + sh_ref[...]).astype(BF16)


def _mlstm_in_kernel(x_ref, sh_ref, sc_ref, w_ref, wvt_ref, wot_ref, bg_ref, q_ref, k_ref, vt_ref, ot_ref, g_ref,
                     *, qk):
    h = _modulated(x_ref, sh_ref, sc_ref)
    dk = qk // A_HEADS
    q_ref[...] = _dot(h, w_ref[:, :qk]).astype(BF16)
    k_ref[...] = (_dot(h, w_ref[:, qk:2 * qk]) * (dk ** -0.5)).astype(BF16)
    vt_ref[...] = _dot_nt(wvt_ref[...], h).astype(BF16)
    ot_ref[...] = _sigmoid(_dot_nt(wot_ref[...], h))
    g_ref[...] = _dot(h, w_ref[:, 2 * qk:]) + bg_ref[...]


def _attn_in_body(x_ref, sh_ref, sc_ref, w_ref, wvt_ref, cos_ref, sin_ref, q_ref, k_ref, v_ref, *,
                  qw, kw, scale, kdup=False):
    h = _modulated(x_ref, sh_ref, sc_ref)
    cos = cos_ref[...]
    sin = sin_ref[...]
    lo = _rope_lo_mask(h.shape[0])
    vw = 0 if wvt_ref is not None else w_ref.shape[1] - qw - kw
    first_head = lax.broadcasted_iota(jnp.int32, (h.shape[0], LANES), 1) < B_HEAD_DIM

    def emit(col, slab):
        if col < qw:
            q_ref[:, col:col + LANES] = (_rope_slab(slab, cos, sin, lo) * scale).astype(BF16)
        elif col < qw + kw and kdup:
            r = _rope_slab(slab, cos, sin, lo)
            sw = pltpu.roll(r, B_HEAD_DIM, axis=1)
            c = 2 * (col - qw)
            k_ref[:, c:c + LANES] = jnp.where(first_head, r, sw).astype(BF16)
            k_ref[:, c + LANES:c + 2 * LANES] = jnp.where(first_head, sw, r).astype(BF16)
        elif col < qw + kw:
            k_ref[:, col - qw:col - qw + LANES] = _rope_slab(slab, cos, sin, lo).astype(BF16)
        else:
            v_ref[:, col - qw - kw:col - qw - kw + LANES] = slab.astype(BF16)

    total = qw + kw + vw
    for c0 in range(0, total, MXU_COLS):
        width = min(MXU_COLS, total - c0)
        wide = _dot(h, w_ref[:, c0:c0 + width])
        for off in range(0, width, LANES):
            emit(c0 + off, wide[:, off:off + LANES])
    if wvt_ref is not None:
        v_ref[...] = _dot_nt(wvt_ref[...], h).astype(BF16)


def _attn_in_vt_kernel(x_ref, sh_ref, sc_ref, w_ref, wvt_ref, cos_ref, sin_ref, q_ref, k_ref, v_ref, **kw):
    _attn_in_body(x_ref, sh_ref, sc_ref, w_ref, wvt_ref, cos_ref, sin_ref, q_ref, k_ref, v_ref, **kw)


def _hgrn_in_kernel(x_ref, sh_ref, sc_ref, w_ref, q_ref, f_ref, v_ref, g_ref, *, kw, vw):
    h = _modulated(x_ref, sh_ref, sc_ref)
    q_ref[...] = _dot(h, w_ref[:, :kw])
    f_ref[...] = _dot(h, w_ref[:, kw:2 * kw])
    v_ref[...] = _dot(h, w_ref[:, 2 * kw:2 * kw + vw]).astype(BF16)
    g = _dot(h, w_ref[:, 2 * kw + vw:])
    g_ref[...] = g * _sigmoid(g)


def _in_proj(body, x, mod, sub, seq, w, extra, extra_specs, outs):
    t, d = x.shape
    tm = min(TOKEN_TILE, seq)
    tpb = seq // tm
    row = lambda width: pl.BlockSpec((tm, width), lambda i: (i, 0))
    col = lambda width: pl.BlockSpec((width, tm), lambda i: (0, i))
    return pl.pallas_call(
        body,
        grid=(t // tm,),
        in_specs=[row(d), _mod_spec(sub, 0, tpb, d), _mod_spec(sub, 1, tpb, d), _const_spec(w.shape)] + extra_specs,
        out_specs=[col(wd) if tr else row(wd) for wd, _, tr in outs],
        out_shape=[jax.ShapeDtypeStruct((wd, t) if tr else (t, wd), dt) for wd, dt, tr in outs],
        compiler_params=_params("arbitrary"),
        name=body.func.__name__.strip("_") if isinstance(body, functools.partial) else body.__name__.strip("_"),
    )(x, mod, mod, w, *extra)


def _tail_kernel(y_ref, wp_ref, x_ref, gate0_ref, g0_ref, b0_ref, sh_ref, sc_ref, gate1_ref, wi_ref, wo_ref,
                 g1_ref, b1_ref, o_ref, *, hidden, chunk, alpha):
    hm = x_ref.shape[0] // TAIL_PARTS
    rows = [slice(p * hm, (p + 1) * hm) for p in range(TAIL_PARTS)]
    nchunk = hidden // chunk

    def norm1(i, y):
        return _layer_norm(alpha * x_ref[rows[i], :] + (1.0 + gate0_ref[...]) * y, g0_ref[...], b0_ref[...])

    def ffn_part(h, j):
        a = _dot(h, wi_ref[:, j * chunk:(j + 1) * chunk])
        u = _dot(h, wi_ref[:, hidden + j * chunk:hidden + (j + 1) * chunk])
        act = (a * _sigmoid(a) * u).astype(BF16)
        return _dot(act, wo_ref[j * chunk:(j + 1) * chunk, :])

    def norm2(i, x, acc):
        o_ref[rows[i], :] = _layer_norm(alpha * x + (1.0 + gate1_ref[...]) * acc, g1_ref[...], b1_ref[...])

    ys, xs, hs, accs = {}, {}, {}, {}
    for i in range(TAIL_PARTS + 2):
        if i < TAIL_PARTS:
            ys[i] = _dot(y_ref[rows[i], :], wp_ref[...])
        if 0 <= i - 1 < TAIL_PARTS:
            accs[i - 1] = ffn_part(hs[i - 1], 0)
        if i < TAIL_PARTS:
            xs[i] = norm1(i, ys[i])
            hs[i] = (xs[i] * (1.0 + sc_ref[...]) + sh_ref[...]).astype(BF16)
        if 0 <= i - 1 < TAIL_PARTS:
            for j in range(1, nchunk):
                accs[i - 1] = accs[i - 1] + ffn_part(hs[i - 1], j)
        if 0 <= i - 2 < TAIL_PARTS:
            norm2(i - 2, xs[i - 2], accs[i - 2])


def _ffn_chunk(hidden):
    best = None
    for c in range(MXU_COLS, min(hidden, FFN_MAX_CHUNK) + 1, MXU_COLS):
        if hidden % c == 0:
            best = c
    assert best is not None, hidden
    return best


def _resident_spec(shape):
    nd = len(shape)
    return pl.BlockSpec(shape, lambda *_: (0,) * nd, pipeline_mode=pl.Buffered(1))


def _layer_tail(y, w_proj, x, mod, sub, seq, w_in, w_out, ln_g, ln_b, alpha):
    t, d = x.shape
    hidden = w_out.shape[0]
    tm = min(TAIL_TILE, seq)
    tpb = seq // tm
    row = lambda width: pl.BlockSpec((tm, width), lambda i: (i, 0))
    vec = _const_spec((1, d))
    return pl.pallas_call(
        functools.partial(_tail_kernel, hidden=hidden, chunk=_ffn_chunk(hidden), alpha=alpha),
        grid=(t // tm,),
        in_specs=[row(y.shape[1]), _resident_spec(w_proj.shape), row(d), _mod_spec(sub, 2, tpb, d), vec, vec,
                  _mod_spec(sub + 1, 0, tpb, d), _mod_spec(sub + 1, 1, tpb, d), _mod_spec(sub + 1, 2, tpb, d),
                  _resident_spec(w_in.shape), _resident_spec(w_out.shape), vec, vec],
        out_specs=row(d),
        out_shape=jax.ShapeDtypeStruct((t, d), F32),
        compiler_params=_params("arbitrary"),
        name="out_proj_ffn",
    )(y, w_proj, x, mod, ln_g[0].reshape(1, d), ln_b[0].reshape(1, d), mod, mod, mod, w_in, w_out,
      ln_g[1].reshape(1, d), ln_b[1].reshape(1, d))


def _mlstm_kernel(q_ref, k_ref, vt_ref, ot_ref, g_ref, nw_ref, y_ref, ct_ref, m_ref, z_ref, zt_ref, *,
                  chunk, dk, dv):
    L = chunk
    nchunks = q_ref.shape[0] // L

    @pl.when(pl.program_id(1) == 0)
    def _():
        ct_ref[...] = jnp.zeros_like(ct_ref)
        m_ref[...] = jnp.zeros_like(m_ref)

    key_i = lax.broadcasted_iota(jnp.int32, (L, L), 0)
    qry_i = lax.broadcasted_iota(jnp.int32, (L, L), 1)
    causal_t = key_i <= qry_i

    tril = jnp.where(qry_i <= key_i, 1.0, 0.0).astype(BF16)
    lane = lax.broadcasted_iota(jnp.int32, (L, LANES), 1)
    for r0 in range(0, g_ref.shape[0], L):
        g = g_ref[r0:r0 + L, :]
        logf = jnp.where(lane >= A_HEADS, _log_sigmoid(g), 0.0)
        z = jnp.where(lane < A_HEADS, g, _dot_exact_lhs01(tril, logf))
        z_ref[r0:r0 + L, :] = z
        zt_ref[:, r0:r0 + L] = z.T

    def body(c, carry):
        start = pl.multiple_of(c * L, L)
        rows = pl.ds(start, L)
        z = z_ref[rows, :]
        early = []
        for h in range(A_HEADS):
            i_col = z[:, h:h + 1]
            b_col = z[:, A_HEADS + h:A_HEADS + h + 1]
            m_prev = m_ref[h, 0:1, 0:1]
            qh = q_ref[rows, h * dk:(h + 1) * dk]
            kh = k_ref[rows, h * dk:(h + 1) * dk]
            vt = vt_ref[h * dv:(h + 1) * dv, pl.ds(start, L)]
            s_t = _dot_nt(kh, qh)
            ct = ct_ref[h]
            inter = _dot_nt(ct.astype(BF16), qh)
            b_last = b_col[L - 1:L, :]
            g_col = b_last - b_col + i_col
            m_new = jnp.maximum(b_last + m_prev, jnp.max(g_col, axis=0, keepdims=True))
            decay = jnp.exp(b_last + m_prev - m_new)
            kws = kh.astype(F32) * jnp.exp(g_col - m_new)
            ct_ref[h, :dv, :] = decay * ct[:dv] + _dot(vt, kws.astype(BF16))
            ct_ref[h, dv:dv + 1, :] = decay * ct[dv:dv + 1] + jnp.sum(kws, axis=0, keepdims=True)
            m_ref[h] = jnp.broadcast_to(m_new, m_ref.shape[1:])
            early.append((i_col - b_col, m_prev, vt, s_t, inter))
        for h in range(A_HEADS):
            ib_col, m_prev, vt, s_t, inter = early[h]
            b_row = zt_ref[A_HEADS + h:A_HEADS + h + 1, pl.ds(start, L)]
            d_t = jnp.where(causal_t, b_row + ib_col, -jnp.inf)
            m_inter = b_row + m_prev
            m_t = jnp.maximum(m_inter, jnp.max(d_t, axis=0, keepdims=True))
            a_t = s_t * jnp.exp(d_t - m_t)
            w_inter = jnp.exp(m_inter - m_t)
            num_t = _dot(vt, a_t.astype(BF16)) + w_inter * inter[:dv]
            den = jnp.sum(a_t, axis=0, keepdims=True) + w_inter * inter[dv:dv + 1]
            hout_t = num_t * (1.0 / jnp.maximum(jnp.abs(den), jnp.exp(-m_t)))
            ms = jnp.mean(hout_t * hout_t, axis=0, keepdims=True)
            y_t = (hout_t * lax.rsqrt(ms + RMS_EPS) * nw_ref[h * dv:(h + 1) * dv, :]
                   * ot_ref[h * dv:(h + 1) * dv, pl.ds(start, L)])
            y_ref[rows, h * dv:(h + 1) * dv] = y_t.T.astype(y_ref.dtype)
        return carry

    lax.fori_loop(0, nchunks, body, 0, unroll=True)


def _mlstm(q, k, vt, ot, g, norm_w, bsz, seq):
    t, qk = q.shape
    vw = vt.shape[0]
    dk, dv = qk // A_HEADS, vw // A_HEADS
    ts = min(MLSTM_STEP, seq)
    spb = seq // ts
    chunk = min(MLSTM_CHUNK, ts)
    row = lambda width: pl.BlockSpec((ts, width), lambda b, s: (b * spb + s, 0))
    col = pl.BlockSpec((vw, ts), lambda b, s: (0, b * spb + s))
    nw_cols = jnp.broadcast_to(norm_w.astype(F32).reshape(vw, 1), (vw, chunk))
    return pl.pallas_call(
        functools.partial(_mlstm_kernel, chunk=chunk, dk=dk, dv=dv),
        grid=(bsz, spb),
        in_specs=[row(qk), row(qk), col, col, row(LANES), _const_spec((vw, chunk))],
        out_specs=row(vw),
        out_shape=jax.ShapeDtypeStruct((t, vw), BF16),
        scratch_shapes=[pltpu.VMEM((A_HEADS, dv + 8, dk), F32), pltpu.VMEM((A_HEADS, 8, LANES), F32),
                        pltpu.VMEM((ts, LANES), F32), pltpu.VMEM((LANES, ts), F32)],
        compiler_params=_params("arbitrary", "arbitrary"),
        name="mlstm_mixer",
    )(q, k, vt, ot, g, nw_cols)


def _swa_kernel(sink_ref, q_ref, kc_ref, kp_ref, vtc_ref, vtp_ref, o_ref, st_ref, *, heads, hd, group):
    n = pl.program_id(1)
    blk = q_ref.shape[0]
    hpd = SWA_HEADS_PER_DOT
    key_i = lax.broadcasted_iota(jnp.int32, (2 * blk, blk), 0)
    qry_i = lax.broadcasted_iota(jnp.int32, (2 * blk, blk), 1)
    valid = (key_i > qry_i) & (key_i <= qry_i + blk) & ((key_i >= blk) | (n > 0))
    bias = jnp.concatenate([jnp.where(valid, 0.0, -jnp.inf)] * hpd, axis=1)
    upper = lax.broadcasted_iota(jnp.int32, (blk, LANES), 1) >= hd
    ones = jnp.ones((ONES_ROWS, 2 * blk), BF16)

    def scores(j0):
        gi = j0 // group
        ks = slice(gi * LANES, (gi + 1) * LANES)
        k2 = jnp.concatenate([kp_ref[:, ks], kc_ref[:, ks]], axis=0)
        qz = []
        for j in range(j0, j0 + hpd):
            slab = q_ref[:, (j // 2) * LANES:(j // 2 + 1) * LANES]
            own = upper if j % 2 else jnp.logical_not(upper)
            qz.append(jnp.where(own, slab, jnp.zeros_like(slab)))
        return _dot_nt(k2, jnp.concatenate(qz, axis=0))

    for idx, j0 in enumerate(range(0, heads, hpd)):
        st_ref[idx] = scores(j0)
    for idx, j0 in enumerate(range(0, heads, hpd)):
        gi = j0 // group
        st = st_ref[idx] + bias
        off = (j0 % group) * blk
        sink = sink_ref[gi:gi + 1, off:off + hpd * blk] * LOG2_E
        m = jnp.maximum(jnp.max(st, axis=0, keepdims=True), sink)
        p = jnp.exp2(st - m).astype(BF16)
        vs = slice(gi * hd, (gi + 1) * hd)
        vte = jnp.concatenate([jnp.concatenate([vtp_ref[vs, :], vtc_ref[vs, :]], axis=1), ones], axis=0)
        acc = _dot(vte, p)
        ot = acc[:hd, :] * (1.0 / (acc[hd:hd + 1, :] + jnp.exp2(sink - m)))
        for pr in range(hpd // 2):
            pair = jnp.concatenate([ot[:, (2 * pr) * blk:(2 * pr + 1) * blk],
                                    ot[:, (2 * pr + 1) * blk:(2 * pr + 2) * blk]], axis=0)
            c0 = (j0 // 2 + pr) * LANES
            o_ref[:, c0:c0 + LANES] = pair.T.astype(o_ref.dtype)


def _swa(q, k2, vt, sinks, bsz, seq):
    t, qw = q.shape
    kw = vt.shape[0]
    heads = qw // B_HEAD_DIM
    blk = min(B_BLOCK, seq)
    nb = seq // blk
    sink_rows = jnp.repeat(sinks.astype(F32).reshape(heads // B_GROUP, B_GROUP), blk, axis=1)
    cur = lambda width: pl.BlockSpec((blk, width), lambda b, n: (b * nb + n, 0))
    prev = lambda width: pl.BlockSpec((blk, width), lambda b, n: (b * nb + jnp.maximum(n - 1, 0), 0))
    cur_t = pl.BlockSpec((kw, blk), lambda b, n: (0, b * nb + n))
    prev_t = pl.BlockSpec((kw, blk), lambda b, n: (0, b * nb + jnp.maximum(n - 1, 0)))
    return pl.pallas_call(
        functools.partial(_swa_kernel, heads=heads, hd=B_HEAD_DIM, group=B_GROUP),
        grid=(bsz, nb),
        in_specs=[_const_spec(sink_rows.shape), cur(qw), cur(2 * kw), prev(2 * kw), cur_t, prev_t],
        out_specs=cur(qw),
        out_shape=jax.ShapeDtypeStruct((t, qw), BF16),
        scratch_shapes=[pltpu.VMEM((heads // SWA_HEADS_PER_DOT, 2 * blk, SWA_HEADS_PER_DOT * blk), F32)],
        compiler_params=_params("arbitrary", "arbitrary"),
        name="swa_mixer",
    )(sink_rows, q, k2, k2, vt, vt)


def _hgrn_kernel(q_ref, f_ref, v_ref, g_ref, lbp_ref, nw_ref, y_ref, st_ref, cf_ref, ck_ref,
                 cfc_ref, ckc_ref, vsc_ref, *,
                 chunk, dk, dv, layer):
    L = chunk
    nchunks = q_ref.shape[0] // L
    heads = q_ref.shape[1] // dk

    @pl.when(pl.program_id(1) == 0)
    def _():
        st_ref[...] = jnp.zeros_like(st_ref)

    lbp = lbp_ref[...]
    e = jnp.exp(lbp - jnp.max(lbp, axis=0, keepdims=True))
    sm = e / jnp.sum(e, axis=0, keepdims=True)
    lb = jnp.zeros_like(sm[0:1])
    for r in range(1, layer + 1):
        lb = lb + sm[r:r + 1]
    log2_lb = jnp.log(lb) * LOG2_E
    log2_1m_lb = jnp.log1p(-lb) * LOG2_E

    ones = jnp.ones((dk, LANES), BF16)
    half = L // 2
    trow = lax.broadcasted_iota(jnp.int32, (half, dv), 0)

    rb = min(HGRN_CUMSUM_ROWS, q_ref.shape[0])
    row_i = lax.broadcasted_iota(jnp.int32, (rb, rb), 0)
    col_i = lax.broadcasted_iota(jnp.int32, (rb, rb), 1)
    tril = jnp.where((col_i <= row_i) & (col_i // L == row_i // L), 1.0, 0.0).astype(BF16)
    for r0 in range(0, q_ref.shape[0], rb):
        f2 = f_ref[r0:r0 + rb, :] * LOG2_E
        ls = jnp.minimum(f2, 0.0) - jnp.log2(1.0 + jnp.exp2(-jnp.abs(f2)))
        x2 = log2_1m_lb + ls
        logf = jnp.maximum(log2_lb, x2) + jnp.log2(1.0 + jnp.exp2(-jnp.abs(log2_lb - x2)))
        cf = _dot_exact_lhs01(tril, logf)
        cf_ref[r0:r0 + rb, :] = cf
        ck_ref[r0:r0 + rb, :] = cf - (x2 - f2)

    unroll = cfc_ref.shape[0]

    def body(c, carry):
        rows = [pl.ds(pl.multiple_of((c * unroll + u) * L, L), L) for u in range(unroll)]
        for u in range(unroll):
            cfc_ref[u] = cf_ref[rows[u], :]
            ckc_ref[u] = ck_ref[rows[u], :]
            vsc_ref[u] = v_ref[rows[u], :].astype(F32)
        def issue(h, u):
            cs = slice(h * dk, (h + 1) * dk)
            vs = slice(h * dv, (h + 1) * dv)
            cfh = cfc_ref[u, :, cs]
            qh = q_ref[rows[u], cs]
            ws = []
            for s in range(L):
                t0 = 0 if s < half else half
                ws.append(qh[t0:] * jnp.exp2(cfh[t0:] - ckc_ref[u, s:s + 1, cs]))
            w_all = jnp.concatenate(ws, axis=0).astype(BF16)
            r_all = _dot(w_all, ones)
            st = st_ref[h]
            inter = _dot_nt((qh * jnp.exp2(cfh)).astype(BF16), st.astype(BF16))
            last = cfc_ref[u, L - 1:L, cs]
            kdec = jnp.exp2(last - ckc_ref[u, :, cs]).astype(BF16)
            st_ref[h] = jnp.exp2(last) * st + _dot(vsc_ref[u, :, vs].T.astype(BF16), kdec)
            return r_all, inter

        def finish(h, u, r_all, o):
            vs = slice(h * dv, (h + 1) * dv)
            o_lo, o_hi = o[:half], o[half:]
            for s in range(half):
                r = r_all[s * L:(s + 1) * L, :]
                v_s = vsc_ref[u, s:s + 1, vs]
                o_lo = o_lo + jnp.where(trow >= s, r[:half], 0.0) * v_s
                o_hi = o_hi + r[half:] * v_s
            for s in range(half, L):
                r = r_all[half * L + (s - half) * half:half * L + (s - half + 1) * half, :]
                o_hi = o_hi + jnp.where(trow >= s - half, r, 0.0) * vsc_ref[u, s:s + 1, vs]
            o = jnp.concatenate([o_lo, o_hi], axis=0)
            y = _rms_norm(o, nw_ref[:, vs]) * g_ref[rows[u], vs]
            y_ref[rows[u], vs] = y.astype(y_ref.dtype)

        work = [(h, u) for u in range(unroll) for h in range(heads)]
        pending = []
        for hu in work:
            pending.append((hu, issue(*hu)))
            if len(pending) > HGRN_SKEW:
                (h0, u0), (r0, i0) = pending.pop(0)
                finish(h0, u0, r0, i0)
        for (h0, u0), (r0, i0) in pending:
            finish(h0, u0, r0, i0)
        return carry

    lax.fori_loop(0, nchunks // unroll, body, 0)


def _hgrn(q, f, v, g, lower_bounds, layer, norm_w, bsz, seq):
    t, kw = q.shape
    vw = v.shape[1]
    dk, dv = kw // C_HEADS, vw // C_HEADS
    ts = min(HGRN_STEP, seq)
    spb = seq // ts
    chunk = min(HGRN_CHUNK, ts)
    unroll = math.gcd(HGRN_UNROLL, ts // chunk)
    row = lambda width: pl.BlockSpec((ts, width), lambda b, s: (b * spb + s, 0))
    return pl.pallas_call(
        functools.partial(_hgrn_kernel, chunk=chunk, dk=dk, dv=dv, layer=layer),
        grid=(bsz, spb),
        in_specs=[row(kw), row(kw), row(vw), row(vw), _const_spec(lower_bounds.shape), _const_spec((1, vw))],
        out_specs=row(vw),
        out_shape=jax.ShapeDtypeStruct((t, vw), BF16),
        scratch_shapes=[pltpu.VMEM((C_HEADS, dv, dk), F32), pltpu.VMEM((ts, kw), F32), pltpu.VMEM((ts, kw), F32),
                        pltpu.VMEM((unroll, chunk, kw), F32), pltpu.VMEM((unroll, chunk, kw), F32),
                        pltpu.VMEM((unroll, chunk, vw), F32)],
        compiler_params=_params("arbitrary", "arbitrary"),
        name="hgrn2_mixer",
    )(q, f, v, g, lower_bounds, norm_w.reshape(1, vw))


def _diff_kernel(q_ref, k_ref, vt_ref, lam_ref, nw_ref, o_ref, acc_ref, st_ref, *, blk, hd, lam_init):
    qi = pl.program_id(2)
    lv = lam_ref[...]
    lam = (jnp.exp(jnp.sum(lv[0:1] * lv[1:2], axis=-1, keepdims=True))
           - jnp.exp(jnp.sum(lv[2:3] * lv[3:4], axis=-1, keepdims=True)) + lam_init)
    q = q_ref[...]
    lane = lax.broadcasted_iota(jnp.int32, q.shape, 1)
    zero = jnp.zeros_like(q)
    qz = (jnp.where(lane < hd, q, zero), jnp.where(lane >= hd, q, zero))
    acc_ref[...] = jnp.zeros_like(acc_ref)

    hw = 2 * hd
    ones = jnp.ones((ONES_ROWS, blk), BF16)

    def scores_into(slot, j):
        kb = k_ref[pl.ds(pl.multiple_of(j * blk, blk), blk), :]
        for c in range(2):
            st_ref[2 * slot + c] = _dot_nt(kb, qz[c])

    def step(j, j_next, m, masked, src):
        if j_next is not None:
            scores_into(1 - src, j_next)
        vte = jnp.concatenate([vt_ref[:, pl.ds(pl.multiple_of(j * blk, blk), blk)], ones], axis=0)
        if masked:
            key_i = lax.broadcasted_iota(jnp.int32, (blk, blk), 0)
            qry_i = lax.broadcasted_iota(jnp.int32, (blk, blk), 1)
            keep = key_i <= qry_i
        m_out = []
        for c in range(2):
            st = st_ref[2 * src + c]
            if masked:
                st = jnp.where(keep, st, -jnp.inf)
            m_new = jnp.maximum(m[c], jnp.max(st, axis=0, keepdims=True))
            alpha = jnp.exp2(m[c] - m_new)
            p = jnp.exp2(st - m_new).astype(BF16)
            acc_ref[c] = alpha * acc_ref[c] + _dot(vte, p)
            m_out.append(m_new)
        return tuple(m_out)

    scores_into(0, qi)
    neg = jnp.full((1, blk), -jnp.inf, F32)
    m = step(qi, 0, (neg, neg), True, 0)

    def pair(i, cr):
        cr = step(2 * i, 2 * i + 1, cr, False, 1)
        return step(2 * i + 1, 2 * i + 2, cr, False, 0)

    npairs = jnp.maximum(qi - 1, 0) // 2
    m = lax.fori_loop(0, npairs, pair, m)

    @pl.when(qi % 2 == 1)
    def _():
        step(qi - 1, None, m, False, 1)

    @pl.when((qi % 2 == 0) & (qi > 0))
    def _():
        m2 = step(qi - 2, qi - 1, m, False, 1)
        step(qi - 1, None, m2, False, 0)

    norm = [acc_ref[c, :hw, :] * (1.0 / acc_ref[c, hw:hw + 1, :]) for c in range(2)]
    out_t = norm[0] - lam * norm[1]
    o_ref[...] = (_rms_norm(out_t.T, nw_ref[...]) * (1.0 - lam_init)).astype(o_ref.dtype)


def _diff_attn(q, k, vt, lam_vec, norm_w, lam_init, bsz, seq):
    t, w = q.shape
    hw = 2 * D_HEAD_DIM
    heads = w // hw
    blk = min(DIFF_BLOCK, seq)
    nq = seq // blk
    return pl.pallas_call(
        functools.partial(_diff_kernel, blk=blk, hd=D_HEAD_DIM, lam_init=lam_init),
        grid=(bsz, heads, nq),
        in_specs=[pl.BlockSpec((blk, hw), lambda b, h, i: (b * nq + i, h)),
                  pl.BlockSpec((seq, hw), lambda b, h, i: (b, h)),
                  pl.BlockSpec((hw, seq), lambda b, h, i: (h, b)),
                  _const_spec(lam_vec.shape), _const_spec((1, hw))],
        out_specs=pl.BlockSpec((blk, hw), lambda b, h, i: (b * nq + i, h)),
        out_shape=jax.ShapeDtypeStruct((t, w), BF16),
        scratch_shapes=[pltpu.VMEM((2, hw + ONES_ROWS, blk), F32), pltpu.VMEM((4, blk, blk), F32)],
        compiler_params=_params("arbitrary", "arbitrary", "arbitrary"),
        name="diff_attn_mixer",
    )(q, k, vt, lam_vec, norm_w.reshape(1, hw))


def _pad_cols(w, width):
    return jnp.pad(w, ((0, 0), (0, width - w.shape[1])))


def kernel(x, c, positions, ada_w, ada_b, ln_g, ln_b, mlstm_w_in, mlstm_b_gate, mlstm_norm, mlstm_w_out, swa_w_in, swa_sinks, swa_w_out, hgrn_w_in, hgrn_lower_bounds, hgrn_norm, hgrn_w_out, diff_w_in, diff_lambda, diff_norm, diff_w_out, ffn_w_in, ffn_w_out):
    bsz, seq, d = x.shape
    depth = ada_w.shape[0]
    alpha = (2 * depth) ** 0.25
    t = bsz * seq
    xf = x.reshape(t, d)
    mod = _modulation(c, ada_w, ada_b)
    cos_t, sin_t = _rope_tables(positions)
    for i in range(depth):
        mixer, j = i % 4, i // 4
        sub = 2 * i
        if mixer == 0:
            vw = mlstm_w_out.shape[1]
            qk = vw // 2
            n_gate = 2 * A_HEADS
            w_in = mlstm_w_in[j]
            w = _pad_cols(jnp.concatenate([w_in[:, :2 * qk], w_in[:, 2 * qk + 2 * vw:]], axis=1),
                          2 * qk + LANES).astype(BF16)
            w_vt = w_in[:, 2 * qk:2 * qk + vw].T.astype(BF16)
            w_ot = w_in[:, 2 * qk + vw:2 * qk + 2 * vw].T.astype(BF16)
            bg = _pad_cols(mlstm_b_gate[j].reshape(1, n_gate), LANES)
            q, k, vt, ot, g = _in_proj(
                functools.partial(_mlstm_in_kernel, qk=qk), xf, mod, sub, seq, w, [w_vt, w_ot, bg],
                [_const_spec(w_vt.shape), _const_spec(w_ot.shape), _const_spec((1, LANES))],
                [(qk, BF16, False), (qk, BF16, False), (vw, BF16, True), (vw, F32, True), (LANES, F32, False)])
            y = _mlstm(q, k, vt, ot, g, mlstm_norm[j], bsz, seq)
            w_out = mlstm_w_out[j]
        elif mixer == 1:
            qw = swa_w_out.shape[1]
            kw = qw // B_GROUP
            w_qk = swa_w_in[j][:, :qw + kw].astype(BF16)
            w_vt = swa_w_in[j][:, qw + kw:].T.astype(BF16)
            q, k2, vt = _in_proj(
                functools.partial(_attn_in_vt_kernel, qw=qw, kw=kw, scale=B_HEAD_DIM ** -0.5 * LOG2_E, kdup=True),
                xf, mod, sub, seq, w_qk, [w_vt, cos_t, sin_t],
                [_const_spec(w_vt.shape)] + [pl.BlockSpec((min(TOKEN_TILE, seq), LANES), lambda i: (i, 0))] * 2,
                [(qw, BF16, False), (2 * kw, BF16, False), (kw, BF16, True)])
            y = _swa(q, k2, vt, swa_sinks[j], bsz, seq)
            w_out = swa_w_out[j]
        elif mixer == 2:
            vw = hgrn_w_out.shape[1]
            q, f, v, g = _in_proj(
                functools.partial(_hgrn_in_kernel, kw=vw, vw=vw), xf, mod, sub, seq,
                hgrn_w_in[j].astype(BF16), [], [],
                [(vw, F32, False), (vw, F32, False), (vw, BF16, False), (vw, F32, False)])
            y = _hgrn(q, f, v, g, hgrn_lower_bounds.astype(F32), i, hgrn_norm[j], bsz, seq)
            w_out = hgrn_w_out[j]
        else:
            w_ = diff_w_out.shape[1]
            lam_init = 0.8 - 0.6 * math.exp(-0.3 * i)
            w_qk = diff_w_in[j][:, :2 * w_].astype(BF16)
            w_vt = diff_w_in[j][:, 2 * w_:].T.astype(BF16)
            q, k, vt = _in_proj(
                functools.partial(_attn_in_vt_kernel, qw=w_, kw=w_, scale=D_HEAD_DIM ** -0.5 * math.log2(math.e)),
                xf, mod, sub, seq,
                w_qk, [w_vt, cos_t, sin_t],
                [_const_spec(w_vt.shape)] + [pl.BlockSpec((min(TOKEN_TILE, seq), LANES), lambda i: (i, 0))] * 2,
                [(w_, BF16, False), (w_, BF16, False), (w_, BF16, True)])
            y = _diff_attn(q, k, vt, diff_lambda[j].astype(F32), diff_norm[j], lam_init, bsz, seq)
            w_out = diff_w_out[j]
        xf = _layer_tail(y, w_out.astype(BF16), xf, mod, sub, seq, ffn_w_in[i].astype(BF16),
                         ffn_w_out[i].astype(BF16), ln_g[i], ln_b[i], alpha)
    return xf.reshape(bsz, seq, d)
```

```python
import functools
import math

import jax
import jax.numpy as jnp
from jax import lax
from jax.experimental import pallas as pl
from jax.experimental.pallas import tpu as pltpu

F32 = jnp.float32
BF16 = jnp.bfloat16

A_HEADS = 4
B_HEAD_DIM = 64
B_GROUP = 8
B_BLOCK = 128
C_HEADS = 8
D_HEAD_DIM = 64
ROPE_THETA = 500000.0
ROT_FRAC = 4
LN_EPS = 1e-5
RMS_EPS = 1e-6
LANES = 128
MXU_COLS = 256
LOG2_E = math.log2(math.e)
VMEM_LIMIT = 52 * 2 ** 20

TOKEN_TILE = 512
TAIL_TILE = 512
TAIL_PARTS = 2
FFN_MAX_CHUNK = 4096
MLSTM_CHUNK = 128
MLSTM_STEP = 512
HGRN_CHUNK = 16
HGRN_STEP = 512
HGRN_CUMSUM_ROWS = 128
HGRN_UNROLL = 8
HGRN_SKEW = 1
DIFF_BLOCK = 512
ONES_ROWS = 16
SWA_BLOCKS_PER_STEP = 4
SWA_HEADS_PER_DOT = 8


def _params(*sem):
    return pltpu.CompilerParams(dimension_semantics=sem, vmem_limit_bytes=VMEM_LIMIT)


def _dot(a, b):
    return jnp.dot(a, b, preferred_element_type=F32)


def _dot_nt(a, b):
    return lax.dot_general(a, b, (((1,), (1,)), ((), ())), preferred_element_type=F32)


def _split3(x):
    hi = x.astype(BF16)
    r = x - hi.astype(F32)
    mid = r.astype(BF16)
    lo = (r - mid.astype(F32)).astype(BF16)
    return hi, mid, lo


def _dot_exact_lhs01(mask_bf16, x):
    hi, mid, lo = _split3(x)
    return _dot(mask_bf16, hi) + (_dot(mask_bf16, mid) + _dot(mask_bf16, lo))


def _sigmoid(x):
    return 1.0 / (1.0 + jnp.exp(-x))


def _log_sigmoid(x):
    return jnp.minimum(x, 0.0) - jnp.log1p(jnp.exp(-jnp.abs(x)))


def _layer_norm(z, g, b):
    mu = jnp.mean(z, axis=-1, keepdims=True)
    zc = z - mu
    var = jnp.mean(zc * zc, axis=-1, keepdims=True)
    return zc * lax.rsqrt(var + LN_EPS) * g + b


def _rms_norm(h, w):
    return h * lax.rsqrt(jnp.mean(h * h, axis=-1, keepdims=True) + RMS_EPS) * w


def _const_spec(shape):
    nd = len(shape)
    return pl.BlockSpec(shape, lambda *_: (0,) * nd)


def _mod_spec(sub, part, tiles_per_batch, d):
    return pl.BlockSpec((None, None, None, 1, d), lambda i, *_: (sub, part, i // tiles_per_batch, 0, 0))


def _mod_kernel(c_ref, w_ref, b_ref, o_ref):
    c = c_ref[...]
    cond = c * _sigmoid(c)
    w = w_ref[...]
    a_hi = cond.astype(BF16)
    a_lo = (cond - a_hi.astype(F32)).astype(BF16)
    w_hi = w.astype(BF16)
    w_lo = (w - w_hi.astype(F32)).astype(BF16)
    acc = _dot(a_hi, w_hi) + (_dot(a_hi, w_lo) + _dot(a_lo, w_hi))
    o_ref[...] = acc + b_ref[...]


def _modulation(c, ada_w, ada_b):
    depth, _, d, d3 = ada_w.shape
    bsz = c.shape[0]
    nsub = depth * 2
    w = ada_w.reshape(nsub, d, d3)
    b = ada_b.reshape(nsub, 1, d3)
    out = pl.pallas_call(
        _mod_kernel,
        grid=(nsub, 3),
        in_specs=[
            pl.BlockSpec((bsz, d), lambda l, j: (0, 0)),
            pl.BlockSpec((None, d, d), lambda l, j: (l, 0, j)),
            pl.BlockSpec((None, 1, d), lambda l, j: (l, 0, j)),
        ],
        out_specs=pl.BlockSpec((None, None, bsz, d), lambda l, j: (l, j, 0, 0)),
        out_shape=jax.ShapeDtypeStruct((nsub, 3, bsz, d), F32),
        compiler_params=_params("arbitrary", "arbitrary"),
        name="adaln_mod",
    )(c, w, b)
    return out.reshape(nsub, 3, bsz, 1, d)


def _rope_kernel(pos_ref, inv_ref, ecos_ref, esin_ref, base_ref, cos_ref, sin_ref):
    ang = inv_ref[...] * pos_ref[...].astype(F32)
    cos_ref[...] = _spread(jnp.cos(ang), ecos_ref[...]) + base_ref[...]
    sin_ref[...] = _spread(jnp.sin(ang), esin_ref[...])


def _spread(vals_t, e):
    hi, mid, lo = _split3(vals_t.T)
    return _dot(hi, e) + (_dot(mid, e) + _dot(lo, e))


def _rope_tables(positions):
    t = positions.size
    rot = B_HEAD_DIM // ROT_FRAC
    half = rot // 2
    rows = 16
    inv = jnp.power(ROPE_THETA, -jnp.arange(half, dtype=F32) * 2.0 / rot)
    inv_c = jnp.zeros((rows, 1), F32).at[:half, 0].set(inv)
    lane = jnp.arange(LANES) % B_HEAD_DIM
    hit = (jnp.arange(rows)[:, None] == (lane % half)[None, :]) & (lane < rot)[None, :]
    e_cos = hit.astype(BF16)
    e_sin = (hit * jnp.where(lane < half, -1.0, 1.0)[None, :]).astype(BF16)
    base = (lane >= rot).astype(F32).reshape(1, LANES)
    tm = min(TOKEN_TILE, t)
    return pl.pallas_call(
        _rope_kernel,
        grid=(t // tm,),
        in_specs=[pl.BlockSpec((1, tm), lambda i: (0, i)), _const_spec((rows, 1)), _const_spec((rows, LANES)),
                  _const_spec((rows, LANES)), _const_spec((1, LANES))],
        out_specs=[pl.BlockSpec((tm, LANES), lambda i: (i, 0))] * 2,
        out_shape=[jax.ShapeDtypeStruct((t, LANES), F32)] * 2,
        compiler_params=_params("arbitrary"),
        name="rope_tables",
    )(positions.reshape(1, t), inv_c, e_cos, e_sin, base)


def _rope_slab(slab, cos, sin_signed, lo_mask):
    fwd = pltpu.roll(slab, LANES - 8, axis=1)
    bwd = pltpu.roll(slab, 8, axis=1)
    return slab * cos + jnp.where(lo_mask, fwd, bwd) * sin_signed


def _rope_lo_mask(tm):
    lane = lax.broadcasted_iota(jnp.int32, (tm, LANES), 1) % B_HEAD_DIM
    return lane < (B_HEAD_DIM // ROT_FRAC // 2)


def _modulated(x_ref, sh_ref, sc_ref):
    return (x_ref[...] * (1.0 + sc_ref[...]) + sh_ref[...]).astype(BF16)


def _mlstm_in_kernel(x_ref, sh_ref, sc_ref, w_ref, wvt_ref, wot_ref, bg_ref, q_ref, k_ref, vt_ref, ot_ref, g_ref,
                     *, qk):
    h = _modulated(x_ref, sh_ref, sc_ref)
    dk = qk // A_HEADS
    q_ref[...] = _dot(h, w_ref[:, :qk]).astype(BF16)
    k_ref[...] = (_dot(h, w_ref[:, qk:2 * qk]) * (dk ** -0.5)).astype(BF16)
    vt_ref[...] = _dot_nt(wvt_ref[...], h).astype(BF16)
    ot_ref[...] = _sigmoid(_dot_nt(wot_ref[...], h))
    g_ref[...] = _dot(h, w_ref[:, 2 * qk:]) + bg_ref[...]


def _attn_in_body(x_ref, sh_ref, sc_ref, w_ref, wvt_ref, cos_ref, sin_ref, q_ref, k_ref, v_ref, *,
                  qw, kw, scale, kdup=False):
    h = _modulated(x_ref, sh_ref, sc_ref)
    cos = cos_ref[...]
    sin = sin_ref[...]
    lo = _rope_lo_mask(h.shape[0])
    vw = 0 if wvt_ref is not None else w_ref.shape[1] - qw - kw
    first_head = lax.broadcasted_iota(jnp.int32, (h.shape[0], LANES), 1) < B_HEAD_DIM

    def emit(col, slab):
        if col < qw:
            q_ref[:, col:col + LANES] = (_rope_slab(slab, cos, sin, lo) * scale).astype(BF16)
        elif col < qw + kw and kdup:
            r = _rope_slab(slab, cos, sin, lo)
            sw = pltpu.roll(r, B_HEAD_DIM, axis=1)
            c = 2 * (col - qw)
            k_ref[:, c:c + LANES] = jnp.where(first_head, r, sw).astype(BF16)
            k_ref[:, c + LANES:c + 2 * LANES] = jnp.where(first_head, sw, r).astype(BF16)
        elif col < qw + kw:
            k_ref[:, col - qw:col - qw + LANES] = _rope_slab(slab, cos, sin, lo).astype(BF16)
        else:
            v_ref[:, col - qw - kw:col - qw - kw + LANES] = slab.astype(BF16)

    total = qw + kw + vw
    for c0 in range(0, total, MXU_COLS):
        width = min(MXU_COLS, total - c0)
        wide = _dot(h, w_ref[:, c0:c0 + width])
        for off in range(0, width, LANES):
            emit(c0 + off, wide[:, off:off + LANES])
    if wvt_ref is not None:
        v_ref[...] = _dot_nt(wvt_ref[...], h).astype(BF16)


def _attn_in_vt_kernel(x_ref, sh_ref, sc_ref, w_ref, wvt_ref, cos_ref, sin_ref, q_ref, k_ref, v_ref, **kw):
    _attn_in_body(x_ref, sh_ref, sc_ref, w_ref, wvt_ref, cos_ref, sin_ref, q_ref, k_ref, v_ref, **kw)


def _hgrn_in_kernel(x_ref, sh_ref, sc_ref, w_ref, q_ref, f_ref, v_ref, g_ref, *, kw, vw):
    h = _modulated(x_ref, sh_ref, sc_ref)
    q_ref[...] = _dot(h, w_ref[:, :kw])
    f_ref[...] = _dot(h, w_ref[:, kw:2 * kw])
    v_ref[...] = _dot(h, w_ref[:, 2 * kw:2 * kw + vw]).astype(BF16)
    g = _dot(h, w_ref[:, 2 * kw + vw:])
    g_ref[...] = g * _sigmoid(g)


def _in_proj(body, x, mod, sub, seq, w, extra, extra_specs, outs):
    t, d = x.shape
    tm = min(TOKEN_TILE, seq)
    tpb = seq // tm
    row = lambda width: pl.BlockSpec((tm, width), lambda i: (i, 0))
    col = lambda width: pl.BlockSpec((width, tm), lambda i: (0, i))
    return pl.pallas_call(
        body,
        grid=(t // tm,),
        in_specs=[row(d), _mod_spec(sub, 0, tpb, d), _mod_spec(sub, 1, tpb, d), _const_spec(w.shape)] + extra_specs,
        out_specs=[col(wd) if tr else row(wd) for wd, _, tr in outs],
        out_shape=[jax.ShapeDtypeStruct((wd, t) if tr else (t, wd), dt) for wd, dt, tr in outs],
        compiler_params=_params("arbitrary"),
        name=body.func.__name__.strip("_") if isinstance(body, functools.partial) else body.__name__.strip("_"),
    )(x, mod, mod, w, *extra)


def _tail_kernel(y_ref, wp_ref, x_ref, gate0_ref, g0_ref, b0_ref, sh_ref, sc_ref, gate1_ref, wi_ref, wo_ref,
                 g1_ref, b1_ref, o_ref, *, hidden, chunk, alpha):
    hm = x_ref.shape[0] // TAIL_PARTS
    rows = [slice(p * hm, (p + 1) * hm) for p in range(TAIL_PARTS)]
    nchunk = hidden // chunk

    def norm1(i, y):
        return _layer_norm(alpha * x_ref[rows[i], :] + (1.0 + gate0_ref[...]) * y, g0_ref[...], b0_ref[...])

    def ffn_part(h, j):
        a = _dot(h, wi_ref[:, j * chunk:(j + 1) * chunk])
        u = _dot(h, wi_ref[:, hidden + j * chunk:hidden + (j + 1) * chunk])
        act = (a * _sigmoid(a) * u).astype(BF16)
        return _dot(act, wo_ref[j * chunk:(j + 1) * chunk, :])

    def norm2(i, x, acc):
        o_ref[rows[i], :] = _layer_norm(alpha * x + (1.0 + gate1_ref[...]) * acc, g1_ref[...], b1_ref[...])

    ys, xs, hs, accs = {}, {}, {}, {}
    for i in range(TAIL_PARTS + 2):
        if i < TAIL_PARTS:
            ys[i] = _dot(y_ref[rows[i], :], wp_ref[...])
        if 0 <= i - 1 < TAIL_PARTS:
            accs[i - 1] = ffn_part(hs[i - 1], 0)
        if i < TAIL_PARTS:
            xs[i] = norm1(i, ys[i])
            hs[i] = (xs[i] * (1.0 + sc_ref[...]) + sh_ref[...]).astype(BF16)
        if 0 <= i - 1 < TAIL_PARTS:
            for j in range(1, nchunk):
                accs[i - 1] = accs[i - 1] + ffn_part(hs[i - 1], j)
        if 0 <= i - 2 < TAIL_PARTS:
            norm2(i - 2, xs[i - 2], accs[i - 2])


def _ffn_chunk(hidden):
    best = None
    for c in range(MXU_COLS, min(hidden, FFN_MAX_CHUNK) + 1, MXU_COLS):
        if hidden % c == 0:
            best = c
    assert best is not None, hidden
    return best


def _resident_spec(shape):
    nd = len(shape)
    return pl.BlockSpec(shape, lambda *_: (0,) * nd, pipeline_mode=pl.Buffered(1))


def _layer_tail(y, w_proj, x, mod, sub, seq, w_in, w_out, ln_g, ln_b, alpha):
    t, d = x.shape
    hidden = w_out.shape[0]
    tm = min(TAIL_TILE, seq)
    tpb = seq // tm
    row = lambda width: pl.BlockSpec((tm, width), lambda i: (i, 0))
    vec = _const_spec((1, d))
    return pl.pallas_call(
        functools.partial(_tail_kernel, hidden=hidden, chunk=_ffn_chunk(hidden), alpha=alpha),
        grid=(t // tm,),
        in_specs=[row(y.shape[1]), _resident_spec(w_proj.shape), row(d), _mod_spec(sub, 2, tpb, d), vec, vec,
                  _mod_spec(sub + 1, 0, tpb, d), _mod_spec(sub + 1, 1, tpb, d), _mod_spec(sub + 1, 2, tpb, d),
                  _resident_spec(w_in.shape), _resident_spec(w_out.shape), vec, vec],
        out_specs=row(d),
        out_shape=jax.ShapeDtypeStruct((t, d), F32),
        compiler_params=_params("arbitrary"),
        name="out_proj_ffn",
    )(y, w_proj, x, mod, ln_g[0].reshape(1, d), ln_b[0].reshape(1, d), mod, mod, mod, w_in, w_out,
      ln_g[1].reshape(1, d), ln_b[1].reshape(1, d))


def _mlstm_kernel(q_ref, k_ref, vt_ref, ot_ref, g_ref, nw_ref, y_ref, ct_ref, m_ref, z_ref, zt_ref, *,
                  chunk, dk, dv):
    L = chunk
    nchunks = q_ref.shape[0] // L

    @pl.when(pl.program_id(1) == 0)
    def _():
        ct_ref[...] = jnp.zeros_like(ct_ref)
        m_ref[...] = jnp.zeros_like(m_ref)

    key_i = lax.broadcasted_iota(jnp.int32, (L, L), 0)
    qry_i = lax.broadcasted_iota(jnp.int32, (L, L), 1)
    causal_t = key_i <= qry_i

    tril = jnp.where(qry_i <= key_i, 1.0, 0.0).astype(BF16)
    lane = lax.broadcasted_iota(jnp.int32, (L, LANES), 1)
    for r0 in range(0, g_ref.shape[0], L):
        g = g_ref[r0:r0 + L, :]
        logf = jnp.where(lane >= A_HEADS, _log_sigmoid(g), 0.0)
        z = jnp.where(lane < A_HEADS, g, _dot_exact_lhs01(tril, logf))
        z_ref[r0:r0 + L, :] = z
        zt_ref[:, r0:r0 + L] = z.T

    def body(c, carry):
        start = pl.multiple_of(c * L, L)
        rows = pl.ds(start, L)
        z = z_ref[rows, :]
        early = []
        for h in range(A_HEADS):
            i_col = z[:, h:h + 1]
            b_col = z[:, A_HEADS + h:A_HEADS + h + 1]
            m_prev = m_ref[h, 0:1, 0:1]
            qh = q_ref[rows, h * dk:(h + 1) * dk]
            kh = k_ref[rows, h * dk:(h + 1) * dk]
            vt = vt_ref[h * dv:(h + 1) * dv, pl.ds(start, L)]
            s_t = _dot_nt(kh, qh)
            ct = ct_ref[h]
            inter = _dot_nt(ct.astype(BF16), qh)
            b_last = b_col[L - 1:L, :]
            g_col = b_last - b_col + i_col
            m_new = jnp.maximum(b_last + m_prev, jnp.max(g_col, axis=0, keepdims=True))
            decay = jnp.exp(b_last + m_prev - m_new)
            kws = kh.astype(F32) * jnp.exp(g_col - m_new)
            ct_ref[h, :dv, :] = decay * ct[:dv] + _dot(vt, kws.astype(BF16))
            ct_ref[h, dv:dv + 1, :] = decay * ct[dv:dv + 1] + jnp.sum(kws, axis=0, keepdims=True)
            m_ref[h] = jnp.broadcast_to(m_new, m_ref.shape[1:])
            early.append((i_col - b_col, m_prev, vt, s_t, inter))
        for h in range(A_HEADS):
            ib_col, m_prev, vt, s_t, inter = early[h]
            b_row = zt_ref[A_HEADS + h:A_HEADS + h + 1, pl.ds(start, L)]
            d_t = jnp.where(causal_t, b_row + ib_col, -jnp.inf)
            m_inter = b_row + m_prev
            m_t = jnp.maximum(m_inter, jnp.max(d_t, axis=0, keepdims=True))
            a_t = s_t * jnp.exp(d_t - m_t)
            w_inter = jnp.exp(m_inter - m_t)
            num_t = _dot(vt, a_t.astype(BF16)) + w_inter * inter[:dv]
            den = jnp.sum(a_t, axis=0, keepdims=True) + w_inter * inter[dv:dv + 1]
            hout_t = num_t * (1.0 / jnp.maximum(jnp.abs(den), jnp.exp(-m_t)))
            ms = jnp.mean(hout_t * hout_t, axis=0, keepdims=True)
            y_t = (hout_t * lax.rsqrt(ms + RMS_EPS) * nw_ref[h * dv:(h + 1) * dv, :]
                   * ot_ref[h * dv:(h + 1) * dv, pl.ds(start, L)])
            y_ref[rows, h * dv:(h + 1) * dv] = y_t.T.astype(y_ref.dtype)
        return carry

    lax.fori_loop(0, nchunks, body, 0, unroll=True)


def _mlstm(q, k, vt, ot, g, norm_w, bsz, seq):
    t, qk = q.shape
    vw = vt.shape[0]
    dk, dv = qk // A_HEADS, vw // A_HEADS
    ts = min(MLSTM_STEP, seq)
    spb = seq // ts
    chunk = min(MLSTM_CHUNK, ts)
    row = lambda width: pl.BlockSpec((ts, width), lambda b, s: (b * spb + s, 0))
    col = pl.BlockSpec((vw, ts), lambda b, s: (0, b * spb + s))
    nw_cols = jnp.broadcast_to(norm_w.astype(F32).reshape(vw, 1), (vw, chunk))
    return pl.pallas_call(
        functools.partial(_mlstm_kernel, chunk=chunk, dk=dk, dv=dv),
        grid=(bsz, spb),
        in_specs=[row(qk), row(qk), col, col, row(LANES), _const_spec((vw, chunk))],
        out_specs=row(vw),
        out_shape=jax.ShapeDtypeStruct((t, vw), BF16),
        scratch_shapes=[pltpu.VMEM((A_HEADS, dv + 8, dk), F32), pltpu.VMEM((A_HEADS, 8, LANES), F32),
                        pltpu.VMEM((ts, LANES), F32), pltpu.VMEM((LANES, ts), F32)],
        compiler_params=_params("arbitrary", "arbitrary"),
        name="mlstm_mixer",
    )(q, k, vt, ot, g, nw_cols)


def _swa_kernel(sink_ref, q_ref, kc_ref, kp_ref, vtc_ref, vtp_ref, o_ref, st_ref, *, heads, hd, group, blk):
    n = pl.program_id(1)
    nsub = q_ref.shape[0] // blk
    hpd = SWA_HEADS_PER_DOT
    key_i = lax.broadcasted_iota(jnp.int32, (2 * blk, blk), 0)
    qry_i = lax.broadcasted_iota(jnp.int32, (2 * blk, blk), 1)
    band = (key_i > qry_i) & (key_i <= qry_i + blk)
    bias_any = jnp.concatenate([jnp.where(band, 0.0, -jnp.inf)] * hpd, axis=1)
    bias_first = jnp.concatenate([jnp.where(band & ((key_i >= blk) | (n > 0)), 0.0, -jnp.inf)] * hpd, axis=1)
    upper = lax.broadcasted_iota(jnp.int32, (blk, LANES), 1) >= hd
    ones = jnp.ones((ONES_ROWS, 2 * blk), BF16)

    def band_keys(ref_prev, ref_cur, j, axis):
        if axis == 0:
            prev = ref_prev[...] if j == 0 else ref_cur[(j - 1) * blk:j * blk, :]
            return prev, ref_cur[j * blk:(j + 1) * blk, :]
        prev = ref_prev[...] if j == 0 else ref_cur[:, (j - 1) * blk:j * blk]
        return prev, ref_cur[:, j * blk:(j + 1) * blk]

    def scores(j, j0):
        gi = j0 // group
        ks = slice(gi * LANES, (gi + 1) * LANES)
        kp, kc = band_keys(kp_ref, kc_ref, j, 0)
        k2 = jnp.concatenate([kp[:, ks], kc[:, ks]], axis=0)
        qz = []
        for hq in range(j0, j0 + hpd):
            slab = q_ref[j * blk:(j + 1) * blk, (hq // 2) * LANES:(hq // 2 + 1) * LANES]
            own = upper if hq % 2 else jnp.logical_not(upper)
            qz.append(jnp.where(own, slab, jnp.zeros_like(slab)))
        return _dot_nt(k2, jnp.concatenate(qz, axis=0))

    units = [(j, j0) for j in range(nsub) for j0 in range(0, heads, hpd)]
    for idx, (j, j0) in enumerate(units):
        st_ref[idx] = scores(j, j0)
    for idx, (j, j0) in enumerate(units):
        gi = j0 // group
        st = st_ref[idx] + (bias_first if j == 0 else bias_any)
        off = (j0 % group) * blk
        sink = sink_ref[gi:gi + 1, off:off + hpd * blk] * LOG2_E
        m = jnp.maximum(jnp.max(st, axis=0, keepdims=True), sink)
        p = jnp.exp2(st - m).astype(BF16)
        vs = slice(gi * hd, (gi + 1) * hd)
        vp, vc = band_keys(vtp_ref, vtc_ref, j, 1)
        vte = jnp.concatenate([jnp.concatenate([vp[vs, :], vc[vs, :]], axis=1), ones], axis=0)
        acc = _dot(vte, p)
        ot = acc[:hd, :] * (1.0 / (acc[hd:hd + 1, :] + jnp.exp2(sink - m)))
        for pr in range(hpd // 2):
            pair = jnp.concatenate([ot[:, (2 * pr) * blk:(2 * pr + 1) * blk],
                                    ot[:, (2 * pr + 1) * blk:(2 * pr + 2) * blk]], axis=0)
            c0 = (j0 // 2 + pr) * LANES
            o_ref[j * blk:(j + 1) * blk, c0:c0 + LANES] = pair.T.astype(o_ref.dtype)


def _swa(q, k2, vt, sinks, bsz, seq):
    t, qw = q.shape
    kw = vt.shape[0]
    heads = qw // B_HEAD_DIM
    blk = min(B_BLOCK, seq)
    nb = seq // blk
    per = math.gcd(SWA_BLOCKS_PER_STEP, nb)
    ns = nb // per
    sink_rows = jnp.repeat(sinks.astype(F32).reshape(heads // B_GROUP, B_GROUP), blk, axis=1)
    prev_blk = lambda b, n: b * nb + jnp.maximum(n * per - 1, 0)
    cur = lambda width: pl.BlockSpec((per * blk, width), lambda b, n: (b * ns + n, 0))
    prev = lambda width: pl.BlockSpec((blk, width), lambda b, n: (prev_blk(b, n), 0))
    cur_t = pl.BlockSpec((kw, per * blk), lambda b, n: (0, b * ns + n))
    prev_t = pl.BlockSpec((kw, blk), lambda b, n: (0, prev_blk(b, n)))
    nunits = per * (heads // SWA_HEADS_PER_DOT)
    return pl.pallas_call(
        functools.partial(_swa_kernel, heads=heads, hd=B_HEAD_DIM, group=B_GROUP, blk=blk),
        grid=(bsz, ns),
        in_specs=[_const_spec(sink_rows.shape), cur(qw), cur(2 * kw), prev(2 * kw), cur_t, prev_t],
        out_specs=cur(qw),
        out_shape=jax.ShapeDtypeStruct((t, qw), BF16),
        scratch_shapes=[pltpu.VMEM((nunits, 2 * blk, SWA_HEADS_PER_DOT * blk), F32)],
        compiler_params=_params("arbitrary", "arbitrary"),
        name="swa_mixer",
    )(sink_rows, q, k2, k2, vt, vt)


def _hgrn_kernel(q_ref, f_ref, v_ref, g_ref, lbp_ref, nw_ref, y_ref, st_ref, cf_ref, ck_ref,
                 cfc_ref, ckc_ref, vsc_ref, *,
                 chunk, dk, dv, layer):
    L = chunk
    nchunks = q_ref.shape[0] // L
    heads = q_ref.shape[1] // dk

    @pl.when(pl.program_id(1) == 0)
    def _():
        st_ref[...] = jnp.zeros_like(st_ref)

    lbp = lbp_ref[...]
    e = jnp.exp(lbp - jnp.max(lbp, axis=0, keepdims=True))
    sm = e / jnp.sum(e, axis=0, keepdims=True)
    lb = jnp.zeros_like(sm[0:1])
    for r in range(1, layer + 1):
        lb = lb + sm[r:r + 1]
    log2_lb = jnp.log(lb) * LOG2_E
    log2_1m_lb = jnp.log1p(-lb) * LOG2_E

    ones = jnp.ones((dk, LANES), BF16)
    half = L // 2
    trow = lax.broadcasted_iota(jnp.int32, (half, dv), 0)

    rb = min(HGRN_CUMSUM_ROWS, q_ref.shape[0])
    row_i = lax.broadcasted_iota(jnp.int32, (rb, rb), 0)
    col_i = lax.broadcasted_iota(jnp.int32, (rb, rb), 1)
    tril = jnp.where((col_i <= row_i) & (col_i // L == row_i // L), 1.0, 0.0).astype(BF16)
    for r0 in range(0, q_ref.shape[0], rb):
        f2 = f_ref[r0:r0 + rb, :] * LOG2_E
        ls = jnp.minimum(f2, 0.0) - jnp.log2(1.0 + jnp.exp2(-jnp.abs(f2)))
        x2 = log2_1m_lb + ls
        logf = jnp.maximum(log2_lb, x2) + jnp.log2(1.0 + jnp.exp2(-jnp.abs(log2_lb - x2)))
        cf = _dot_exact_lhs01(tril, logf)
        cf_ref[r0:r0 + rb, :] = cf
        ck_ref[r0:r0 + rb, :] = cf - (x2 - f2)

    unroll = cfc_ref.shape[0]

    def body(c, carry):
        rows = [pl.ds(pl.multiple_of((c * unroll + u) * L, L), L) for u in range(unroll)]
        for u in range(unroll):
            cfc_ref[u] = cf_ref[rows[u], :]
            ckc_ref[u] = ck_ref[rows[u], :]
            vsc_ref[u] = v_ref[rows[u], :].astype(F32)
        def issue(h, u):
            cs = slice(h * dk, (h + 1) * dk)
            vs = slice(h * dv, (h + 1) * dv)
            cfh = cfc_ref[u, :, cs]
            qh = q_ref[rows[u], cs]
            ws = []
            for s in range(L):
                t0 = 0 if s < half else half
                ws.append(qh[t0:] * jnp.exp2(cfh[t0:] - ckc_ref[u, s:s + 1, cs]))
            w_all = jnp.concatenate(ws, axis=0).astype(BF16)
            r_all = _dot(w_all, ones)
            st = st_ref[h]
            inter = _dot_nt((qh * jnp.exp2(cfh)).astype(BF16), st.astype(BF16))
            last = cfc_ref[u, L - 1:L, cs]
            kdec = jnp.exp2(last - ckc_ref[u, :, cs]).astype(BF16)
            st_ref[h] = jnp.exp2(last) * st + _dot(vsc_ref[u, :, vs].T.astype(BF16), kdec)
            return r_all, inter

        def finish(h, u, r_all, o):
            vs = slice(h * dv, (h + 1) * dv)
            o_lo, o_hi = o[:half], o[half:]
            for s in range(half):
                r = r_all[s * L:(s + 1) * L, :]
                v_s = vsc_ref[u, s:s + 1, vs]
                o_lo = o_lo + jnp.where(trow >= s, r[:half], 0.0) * v_s
                o_hi = o_hi + r[half:] * v_s
            for s in range(half, L):
                r = r_all[half * L + (s - half) * half:half * L + (s - half + 1) * half, :]
                o_hi = o_hi + jnp.where(trow >= s - half, r, 0.0) * vsc_ref[u, s:s + 1, vs]
            o = jnp.concatenate([o_lo, o_hi], axis=0)
            y = _rms_norm(o, nw_ref[:, vs]) * g_ref[rows[u], vs]
            y_ref[rows[u], vs] = y.astype(y_ref.dtype)

        work = [(h, u) for u in range(unroll) for h in range(heads)]
        pending = []
        for hu in work:
            pending.append((hu, issue(*hu)))
            if len(pending) > HGRN_SKEW:
                (h0, u0), (r0, i0) = pending.pop(0)
                finish(h0, u0, r0, i0)
        for (h0, u0), (r0, i0) in pending:
            finish(h0, u0, r0, i0)
        return carry

    lax.fori_loop(0, nchunks // unroll, body, 0)


def _hgrn(q, f, v, g, lower_bounds, layer, norm_w, bsz, seq):
    t, kw = q.shape
    vw = v.shape[1]
    dk, dv = kw // C_HEADS, vw // C_HEADS
    ts = min(HGRN_STEP, seq)
    spb = seq // ts
    chunk = min(HGRN_CHUNK, ts)
    unroll = math.gcd(HGRN_UNROLL, ts // chunk)
    row = lambda width: pl.BlockSpec((ts, width), lambda b, s: (b * spb + s, 0))
    return pl.pallas_call(
        functools.partial(_hgrn_kernel, chunk=chunk, dk=dk, dv=dv, layer=layer),
        grid=(bsz, spb),
        in_specs=[row(kw), row(kw), row(vw), row(vw), _const_spec(lower_bounds.shape), _const_spec((1, vw))],
        out_specs=row(vw),
        out_shape=jax.ShapeDtypeStruct((t, vw), BF16),
        scratch_shapes=[pltpu.VMEM((C_HEADS, dv, dk), F32), pltpu.VMEM((ts, kw), F32), pltpu.VMEM((ts, kw), F32),
                        pltpu.VMEM((unroll, chunk, kw), F32), pltpu.VMEM((unroll, chunk, kw), F32),
                        pltpu.VMEM((unroll, chunk, vw), F32)],
        compiler_params=_params("arbitrary", "arbitrary"),
        name="hgrn2_mixer",
    )(q, f, v, g, lower_bounds, norm_w.reshape(1, vw))


def _diff_kernel(q_ref, k_ref, vt_ref, lam_ref, nw_ref, o_ref, acc_ref, st_ref, *, blk, hd, lam_init):
    qi = pl.program_id(2)
    lv = lam_ref[...]
    lam = (jnp.exp(jnp.sum(lv[0:1] * lv[1:2], axis=-1, keepdims=True))
           - jnp.exp(jnp.sum(lv[2:3] * lv[3:4], axis=-1, keepdims=True)) + lam_init)
    q = q_ref[...]
    lane = lax.broadcasted_iota(jnp.int32, q.shape, 1)
    zero = jnp.zeros_like(q)
    qz = (jnp.where(lane < hd, q, zero), jnp.where(lane >= hd, q, zero))
    acc_ref[...] = jnp.zeros_like(acc_ref)

    hw = 2 * hd
    ones = jnp.ones((ONES_ROWS, blk), BF16)

    def scores_into(slot, j):
        kb = k_ref[pl.ds(pl.multiple_of(j * blk, blk), blk), :]
        for c in range(2):
            st_ref[2 * slot + c] = _dot_nt(kb, qz[c])

    def step(j, j_next, m, masked, src):
        if j_next is not None:
            scores_into(1 - src, j_next)
        vte = jnp.concatenate([vt_ref[:, pl.ds(pl.multiple_of(j * blk, blk), blk)], ones], axis=0)
        if masked:
            key_i = lax.broadcasted_iota(jnp.int32, (blk, blk), 0)
            qry_i = lax.broadcasted_iota(jnp.int32, (blk, blk), 1)
            keep = key_i <= qry_i
        m_out = []
        for c in range(2):
            st = st_ref[2 * src + c]
            if masked:
                st = jnp.where(keep, st, -jnp.inf)
            m_new = jnp.maximum(m[c], jnp.max(st, axis=0, keepdims=True))
            alpha = jnp.exp2(m[c] - m_new)
            p = jnp.exp2(st - m_new).astype(BF16)
            acc_ref[c] = alpha * acc_ref[c] + _dot(vte, p)
            m_out.append(m_new)
        return tuple(m_out)

    scores_into(0, qi)
    neg = jnp.full((1, blk), -jnp.inf, F32)
    m = step(qi, 0, (neg, neg), True, 0)

    def pair(i, cr):
        cr = step(2 * i, 2 * i + 1, cr, False, 1)
        return step(2 * i + 1, 2 * i + 2, cr, False, 0)

    m = lax.fori_loop(0, qi // 2, pair, m)

    @pl.when(qi % 2 == 1)
    def _():
        step(qi - 1, None, m, False, 1)

    norm = [acc_ref[c, :hw, :] * (1.0 / acc_ref[c, hw:hw + 1, :]) for c in range(2)]
    out_t = norm[0] - lam * norm[1]
    o_ref[...] = (_rms_norm(out_t.T, nw_ref[...]) * (1.0 - lam_init)).astype(o_ref.dtype)


def _diff_attn(q, k, vt, lam_vec, norm_w, lam_init, bsz, seq):
    t, w = q.shape
    hw = 2 * D_HEAD_DIM
    heads = w // hw
    blk = min(DIFF_BLOCK, seq)
    nq = seq // blk
    return pl.pallas_call(
        functools.partial(_diff_kernel, blk=blk, hd=D_HEAD_DIM, lam_init=lam_init),
        grid=(bsz, heads, nq),
        in_specs=[pl.BlockSpec((blk, hw), lambda b, h, i: (b * nq + i, h)),
                  pl.BlockSpec((seq, hw), lambda b, h, i: (b, h)),
                  pl.BlockSpec((hw, seq), lambda b, h, i: (h, b)),
                  _const_spec(lam_vec.shape), _const_spec((1, hw))],
        out_specs=pl.BlockSpec((blk, hw), lambda b, h, i: (b * nq + i, h)),
        out_shape=jax.ShapeDtypeStruct((t, w), BF16),
        scratch_shapes=[pltpu.VMEM((2, hw + ONES_ROWS, blk), F32), pltpu.VMEM((4, blk, blk), F32)],
        compiler_params=_params("arbitrary", "arbitrary", "arbitrary"),
        name="diff_attn_mixer",
    )(q, k, vt, lam_vec, norm_w.reshape(1, hw))


def _pad_cols(w, width):
    return jnp.pad(w, ((0, 0), (0, width - w.shape[1])))


def kernel(x, c, positions, ada_w, ada_b, ln_g, ln_b, mlstm_w_in, mlstm_b_gate, mlstm_norm, mlstm_w_out, swa_w_in, swa_sinks, swa_w_out, hgrn_w_in, hgrn_lower_bounds, hgrn_norm, hgrn_w_out, diff_w_in, diff_lambda, diff_norm, diff_w_out, ffn_w_in, ffn_w_out):
    bsz, seq, d = x.shape
    depth = ada_w.shape[0]
    alpha = (2 * depth) ** 0.25
    t = bsz * seq
    xf = x.reshape(t, d)
    mod = _modulation(c, ada_w, ada_b)
    cos_t, sin_t = _rope_tables(positions)
    for i in range(depth):
        mixer, j = i % 4, i // 4
        sub = 2 * i
        if mixer == 0:
            vw = mlstm_w_out.shape[1]
            qk = vw // 2
            n_gate = 2 * A_HEADS
            w_in = mlstm_w_in[j]
            w = _pad_cols(jnp.concatenate([w_in[:, :2 * qk], w_in[:, 2 * qk + 2 * vw:]], axis=1),
                          2 * qk + LANES).astype(BF16)
            w_vt = w_in[:, 2 * qk:2 * qk + vw].T.astype(BF16)
            w_ot = w_in[:, 2 * qk + vw:2 * qk + 2 * vw].T.astype(BF16)
            bg = _pad_cols(mlstm_b_gate[j].reshape(1, n_gate), LANES)
            q, k, vt, ot, g = _in_proj(
                functools.partial(_mlstm_in_kernel, qk=qk), xf, mod, sub, seq, w, [w_vt, w_ot, bg],
                [_const_spec(w_vt.shape), _const_spec(w_ot.shape), _const_spec((1, LANES))],
                [(qk, BF16, False), (qk, BF16, False), (vw, BF16, True), (vw, F32, True), (LANES, F32, False)])
            y = _mlstm(q, k, vt, ot, g, mlstm_norm[j], bsz, seq)
            w_out = mlstm_w_out[j]
        elif mixer == 1:
            qw = swa_w_out.shape[1]
            kw = qw // B_GROUP
            w_qk = swa_w_in[j][:, :qw + kw].astype(BF16)
            w_vt = swa_w_in[j][:, qw + kw:].T.astype(BF16)
            q, k2, vt = _in_proj(
                functools.partial(_attn_in_vt_kernel, qw=qw, kw=kw, scale=B_HEAD_DIM ** -0.5 * LOG2_E, kdup=True),
                xf, mod, sub, seq, w_qk, [w_vt, cos_t, sin_t],
                [_const_spec(w_vt.shape)] + [pl.BlockSpec((min(TOKEN_TILE, seq), LANES), lambda i: (i, 0))] * 2,
                [(qw, BF16, False), (2 * kw, BF16, False), (kw, BF16, True)])
            y = _swa(q, k2, vt, swa_sinks[j], bsz, seq)
            w_out = swa_w_out[j]
        elif mixer == 2:
            vw = hgrn_w_out.shape[1]
            q, f, v, g = _in_proj(
                functools.partial(_hgrn_in_kernel, kw=vw, vw=vw), xf, mod, sub, seq,
                hgrn_w_in[j].astype(BF16), [], [],
                [(vw, F32, False), (vw, F32, False), (vw, BF16, False), (vw, F32, False)])
            y = _hgrn(q, f, v, g, hgrn_lower_bounds.astype(F32), i, hgrn_norm[j], bsz, seq)
            w_out = hgrn_w_out[j]
        else:
            w_ = diff_w_out.shape[1]
            lam_init = 0.8 - 0.6 * math.exp(-0.3 * i)
            w_qk = diff_w_in[j][:, :2 * w_].astype(BF16)
            w_vt = diff_w_in[j][:, 2 * w_:].T.astype(BF16)
            q, k, vt = _in_proj(
                functools.partial(_attn_in_vt_kernel, qw=w_, kw=w_, scale=D_HEAD_DIM ** -0.5 * math.log2(math.e)),
                xf, mod, sub, seq,
                w_qk, [w_vt, cos_t, sin_t],
                [_const_spec(w_vt.shape)] + [pl.BlockSpec((min(TOKEN_TILE, seq), LANES), lambda i: (i, 0))] * 2,
                [(w_, BF16, False), (w_, BF16, False), (w_, BF16, True)])
            y = _diff_attn(q, k, vt, diff_lambda[j].astype(F32), diff_norm[j], lam_init, bsz, seq)
            w_out = diff_w_out[j]
        xf = _layer_tail(y, w_out.astype(BF16), xf, mod, sub, seq, ffn_w_in[i].astype(BF16),
                         ffn_w_out[i].astype(BF16), ln_g[i], ln_b[i], alpha)
    return xf.reshape(bsz, seq, d)
```

```python
import functools
import math

import jax
import jax.numpy as jnp
from jax import lax
from jax.experimental import pallas as pl
from jax.experimental.pallas import tpu as pltpu

F32 = jnp.float32
BF16 = jnp.bfloat16

A_HEADS = 4
B_HEAD_DIM = 64
B_GROUP = 8
B_BLOCK = 128
C_HEADS = 8
D_HEAD_DIM = 64
ROPE_THETA = 500000.0
ROT_FRAC = 4
LN_EPS = 1e-5
RMS_EPS = 1e-6
LANES = 128
MXU_COLS = 256
LOG2_E = math.log2(math.e)
VMEM_LIMIT = 52 * 2 ** 20

TOKEN_TILE = 512
TAIL_TILE = 512
TAIL_PARTS = 2
FFN_MAX_CHUNK = 4096
MLSTM_CHUNK = 128
MLSTM_STEP = 512
HGRN_CHUNK = 16
HGRN_STEP = 512
HGRN_CUMSUM_ROWS = 128
HGRN_UNROLL = 8
HGRN_SKEW = 1
DIFF_BLOCK = 512
ONES_ROWS = 16
SWA_BLOCKS_PER_STEP = 4
SWA_HEADS_PER_DOT = 8


def _params(*sem):
    return pltpu.CompilerParams(dimension_semantics=sem, vmem_limit_bytes=VMEM_LIMIT)


def _dot(a, b):
    return jnp.dot(a, b, preferred_element_type=F32)


def _dot_nt(a, b):
    return lax.dot_general(a, b, (((1,), (1,)), ((), ())), preferred_element_type=F32)


def _split3(x):
    hi = x.astype(BF16)
    r = x - hi.astype(F32)
    mid = r.astype(BF16)
    lo = (r - mid.astype(F32)).astype(BF16)
    return hi, mid, lo


def _dot_exact_lhs01(mask_bf16, x):
    hi, mid, lo = _split3(x)
    return _dot(mask_bf16, hi) + (_dot(mask_bf16, mid) + _dot(mask_bf16, lo))


def _sigmoid(x):
    return 1.0 / (1.0 + jnp.exp(-x))


def _log_sigmoid(x):
    return jnp.minimum(x, 0.0) - jnp.log1p(jnp.exp(-jnp.abs(x)))


def _layer_norm(z, g, b):
    mu = jnp.mean(z, axis=-1, keepdims=True)
    zc = z - mu
    var = jnp.mean(zc * zc, axis=-1, keepdims=True)
    return zc * lax.rsqrt(var + LN_EPS) * g + b


def _rms_norm(h, w):
    return h * lax.rsqrt(jnp.mean(h * h, axis=-1, keepdims=True) + RMS_EPS) * w


def _const_spec(shape):
    nd = len(shape)
    return pl.BlockSpec(shape, lambda *_: (0,) * nd)


def _mod_spec(sub, part, tiles_per_batch, d):
    return pl.BlockSpec((None, None, None, 1, d), lambda i, *_: (sub, part, i // tiles_per_batch, 0, 0))


def _mod_kernel(c_ref, w_ref, b_ref, o_ref):
    c = c_ref[...]
    cond = c * _sigmoid(c)
    w = w_ref[...]
    a_hi = cond.astype(BF16)
    a_lo = (cond - a_hi.astype(F32)).astype(BF16)
    w_hi = w.astype(BF16)
    w_lo = (w - w_hi.astype(F32)).astype(BF16)
    acc = _dot(a_hi, w_hi) + (_dot(a_hi, w_lo) + _dot(a_lo, w_hi))
    o_ref[...] = acc + b_ref[...]


def _modulation(c, ada_w, ada_b):
    depth, _, d, d3 = ada_w.shape
    bsz = c.shape[0]
    nsub = depth * 2
    w = ada_w.reshape(nsub, d, d3)
    b = ada_b.reshape(nsub, 1, d3)
    out = pl.pallas_call(
        _mod_kernel,
        grid=(nsub, 3),
        in_specs=[
            pl.BlockSpec((bsz, d), lambda l, j: (0, 0)),
            pl.BlockSpec((None, d, d), lambda l, j: (l, 0, j)),
            pl.BlockSpec((None, 1, d), lambda l, j: (l, 0, j)),
        ],
        out_specs=pl.BlockSpec((None, None, bsz, d), lambda l, j: (l, j, 0, 0)),
        out_shape=jax.ShapeDtypeStruct((nsub, 3, bsz, d), F32),
        compiler_params=_params("arbitrary", "arbitrary"),
        name="adaln_mod",
    )(c, w, b)
    return out.reshape(nsub, 3, bsz, 1, d)


def _rope_kernel(pos_ref, inv_ref, ecos_ref, esin_ref, base_ref, cos_ref, sin_ref):
    ang = inv_ref[...] * pos_ref[...].astype(F32)
    cos_ref[...] = _spread(jnp.cos(ang), ecos_ref[...]) + base_ref[...]
    sin_ref[...] = _spread(jnp.sin(ang), esin_ref[...])


def _spread(vals_t, e):
    hi, mid, lo = _split3(vals_t.T)
    return _dot(hi, e) + (_dot(mid, e) + _dot(lo, e))


def _rope_tables(positions):
    t = positions.size
    rot = B_HEAD_DIM // ROT_FRAC
    half = rot // 2
    rows = 16
    inv = jnp.power(ROPE_THETA, -jnp.arange(half, dtype=F32) * 2.0 / rot)
    inv_c = jnp.zeros((rows, 1), F32).at[:half, 0].set(inv)
    lane = jnp.arange(LANES) % B_HEAD_DIM
    hit = (jnp.arange(rows)[:, None] == (lane % half)[None, :]) & (lane < rot)[None, :]
    e_cos = hit.astype(BF16)
    e_sin = (hit * jnp.where(lane < half, -1.0, 1.0)[None, :]).astype(BF16)
    base = (lane >= rot).astype(F32).reshape(1, LANES)
    tm = min(TOKEN_TILE, t)
    return pl.pallas_call(
        _rope_kernel,
        grid=(t // tm,),
        in_specs=[pl.BlockSpec((1, tm), lambda i: (0, i)), _const_spec((rows, 1)), _const_spec((rows, LANES)),
                  _const_spec((rows, LANES)), _const_spec((1, LANES))],
        out_specs=[pl.BlockSpec((tm, LANES), lambda i: (i, 0))] * 2,
        out_shape=[jax.ShapeDtypeStruct((t, LANES), F32)] * 2,
        compiler_params=_params("arbitrary"),
        name="rope_tables",
    )(positions.reshape(1, t), inv_c, e_cos, e_sin, base)


def _rope_slab(slab, cos, sin_signed, lo_mask):
    fwd = pltpu.roll(slab, LANES - 8, axis=1)
    bwd = pltpu.roll(slab, 8, axis=1)
    return slab * cos + jnp.where(lo_mask, fwd, bwd) * sin_signed


def _rope_lo_mask(tm):
    lane = lax.broadcasted_iota(jnp.int32, (tm, LANES), 1) % B_HEAD_DIM
    return lane < (B_HEAD_DIM // ROT_FRAC // 2)


def _modulated(x_ref, sh_ref, sc_ref):
    return (x_ref[...] * (1.0 + sc_ref[...]) + sh_ref[...]).astype(BF16)


def _mlstm_in_kernel(x_ref, sh_ref, sc_ref, w_ref, wvt_ref, wot_ref, bg_ref, q_ref, k_ref, vt_ref, ot_ref, g_ref,
                     *, qk):
    h = _modulated(x_ref, sh_ref, sc_ref)
    dk = qk // A_HEADS
    q_ref[...] = _dot(h, w_ref[:, :qk]).astype(BF16)
    k_ref[...] = (_dot(h, w_ref[:, qk:2 * qk]) * (dk ** -0.5)).astype(BF16)
    vt_ref[...] = _dot_nt(wvt_ref[...], h).astype(BF16)
    ot_ref[...] = _sigmoid(_dot_nt(wot_ref[...], h))
    g_ref[...] = _dot(h, w_ref[:, 2 * qk:]) + bg_ref[...]


def _attn_in_body(x_ref, sh_ref, sc_ref, w_ref, wvt_ref, cos_ref, sin_ref, q_ref, k_ref, v_ref, *,
                  qw, kw, scale, kdup=False):
    h = _modulated(x_ref, sh_ref, sc_ref)
    cos = cos_ref[...]
    sin = sin_ref[...]
    lo = _rope_lo_mask(h.shape[0])
    vw = 0 if wvt_ref is not None else w_ref.shape[1] - qw - kw
    first_head = lax.broadcasted_iota(jnp.int32, (h.shape[0], LANES), 1) < B_HEAD_DIM

    def emit(col, slab):
        if col < qw:
            q_ref[:, col:col + LANES] = (_rope_slab(slab, cos, sin, lo) * scale).astype(BF16)
        elif col < qw + kw and kdup:
            r = _rope_slab(slab, cos, sin, lo)
            sw = pltpu.roll(r, B_HEAD_DIM, axis=1)
            c = 2 * (col - qw)
            k_ref[:, c:c + LANES] = jnp.where(first_head, r, sw).astype(BF16)
            k_ref[:, c + LANES:c + 2 * LANES] = jnp.where(first_head, sw, r).astype(BF16)
        elif col < qw + kw:
            k_ref[:, col - qw:col - qw + LANES] = _rope_slab(slab, cos, sin, lo).astype(BF16)
        else:
            v_ref[:, col - qw - kw:col - qw - kw + LANES] = slab.astype(BF16)

    total = qw + kw + vw
    for c0 in range(0, total, MXU_COLS):
        width = min(MXU_COLS, total - c0)
        wide = _dot(h, w_ref[:, c0:c0 + width])
        for off in range(0, width, LANES):
            emit(c0 + off, wide[:, off:off + LANES])
    if wvt_ref is not None:
        v_ref[...] = _dot_nt(wvt_ref[...], h).astype(BF16)


def _attn_in_vt_kernel(x_ref, sh_ref, sc_ref, w_ref, wvt_ref, cos_ref, sin_ref, q_ref, k_ref, v_ref, **kw):
    _attn_in_body(x_ref, sh_ref, sc_ref, w_ref, wvt_ref, cos_ref, sin_ref, q_ref, k_ref, v_ref, **kw)


def _hgrn_in_kernel(x_ref, sh_ref, sc_ref, w_ref, q_ref, f_ref, v_ref, g_ref, *, kw, vw):
    h = _modulated(x_ref, sh_ref, sc_ref)
    q_ref[...] = _dot(h, w_ref[:, :kw])
    f_ref[...] = _dot(h, w_ref[:, kw:2 * kw])
    v_ref[...] = _dot(h, w_ref[:, 2 * kw:2 * kw + vw]).astype(BF16)
    g = _dot(h, w_ref[:, 2 * kw + vw:])
    g_ref[...] = g * _sigmoid(g)


def _in_proj(body, x, mod, sub, seq, w, extra, extra_specs, outs):
    t, d = x.shape
    tm = min(TOKEN_TILE, seq)
    tpb = seq // tm
    row = lambda width: pl.BlockSpec((tm, width), lambda i: (i, 0))
    col = lambda width: pl.BlockSpec((width, tm), lambda i: (0, i))
    return pl.pallas_call(
        body,
        grid=(t // tm,),
        in_specs=[row(d), _mod_spec(sub, 0, tpb, d), _mod_spec(sub, 1, tpb, d), _const_spec(w.shape)] + extra_specs,
        out_specs=[col(wd) if tr else row(wd) for wd, _, tr in outs],
        out_shape=[jax.ShapeDtypeStruct((wd, t) if tr else (t, wd), dt) for wd, dt, tr in outs],
        compiler_params=_params("arbitrary"),
        name=body.func.__name__.strip("_") if isinstance(body, functools.partial) else body.__name__.strip("_"),
    )(x, mod, mod, w, *extra)


def _tail_kernel(y_ref, wp_ref, x_ref, gate0_ref, g0_ref, b0_ref, sh_ref, sc_ref, gate1_ref, wi_ref, wo_ref,
                 g1_ref, b1_ref, o_ref, *, hidden, chunk, alpha):
    hm = x_ref.shape[0] // TAIL_PARTS
    rows = [slice(p * hm, (p + 1) * hm) for p in range(TAIL_PARTS)]
    nchunk = hidden // chunk

    def norm1(i, y):
        return _layer_norm(alpha * x_ref[rows[i], :] + (1.0 + gate0_ref[...]) * y, g0_ref[...], b0_ref[...])

    def ffn_part(h, j):
        a = _dot(h, wi_ref[:, j * chunk:(j + 1) * chunk])
        u = _dot(h, wi_ref[:, hidden + j * chunk:hidden + (j + 1) * chunk])
        act = (a * _sigmoid(a) * u).astype(BF16)
        return _dot(act, wo_ref[j * chunk:(j + 1) * chunk, :])

    def norm2(i, x, acc):
        o_ref[rows[i], :] = _layer_norm(alpha * x + (1.0 + gate1_ref[...]) * acc, g1_ref[...], b1_ref[...])

    ys, xs, hs, accs = {}, {}, {}, {}
    for i in range(TAIL_PARTS + 2):
        if i < TAIL_PARTS:
            ys[i] = _dot(y_ref[rows[i], :], wp_ref[...])
        if 0 <= i - 1 < TAIL_PARTS:
            accs[i - 1] = ffn_part(hs[i - 1], 0)
        if i < TAIL_PARTS:
            xs[i] = norm1(i, ys[i])
            hs[i] = (xs[i] * (1.0 + sc_ref[...]) + sh_ref[...]).astype(BF16)
        if 0 <= i - 1 < TAIL_PARTS:
            for j in range(1, nchunk):
                accs[i - 1] = accs[i - 1] + ffn_part(hs[i - 1], j)
        if 0 <= i - 2 < TAIL_PARTS:
            norm2(i - 2, xs[i - 2], accs[i - 2])


def _ffn_chunk(hidden):
    best = None
    for c in range(MXU_COLS, min(hidden, FFN_MAX_CHUNK) + 1, MXU_COLS):
        if hidden % c == 0:
            best = c
    assert best is not None, hidden
    return best


def _resident_spec(shape):
    nd = len(shape)
    return pl.BlockSpec(shape, lambda *_: (0,) * nd, pipeline_mode=pl.Buffered(1))


def _layer_tail(y, w_proj, x, mod, sub, seq, w_in, w_out, ln_g, ln_b, alpha):
    t, d = x.shape
    hidden = w_out.shape[0]
    tm = min(TAIL_TILE, seq)
    tpb = seq // tm
    row = lambda width: pl.BlockSpec((tm, width), lambda i: (i, 0))
    vec = _const_spec((1, d))
    return pl.pallas_call(
        functools.partial(_tail_kernel, hidden=hidden, chunk=_ffn_chunk(hidden), alpha=alpha),
        grid=(t // tm,),
        in_specs=[row(y.shape[1]), _resident_spec(w_proj.shape), row(d), _mod_spec(sub, 2, tpb, d), vec, vec,
                  _mod_spec(sub + 1, 0, tpb, d), _mod_spec(sub + 1, 1, tpb, d), _mod_spec(sub + 1, 2, tpb, d),
                  _resident_spec(w_in.shape), _resident_spec(w_out.shape), vec, vec],
        out_specs=row(d),
        out_shape=jax.ShapeDtypeStruct((t, d), F32),
        compiler_params=_params("arbitrary"),
        name="out_proj_ffn",
    )(y, w_proj, x, mod, ln_g[0].reshape(1, d), ln_b[0].reshape(1, d), mod, mod, mod, w_in, w_out,
      ln_g[1].reshape(1, d), ln_b[1].reshape(1, d))


def _mlstm_kernel(q_ref, k_ref, vt_ref, ot_ref, g_ref, nw_ref, y_ref, ct_ref, m_ref, z_ref, zt_ref, *,
                  chunk, dk, dv):
    L = chunk
    nchunks = q_ref.shape[0] // L

    @pl.when(pl.program_id(1) == 0)
    def _():
        ct_ref[...] = jnp.zeros_like(ct_ref)
        m_ref[...] = jnp.zeros_like(m_ref)

    key_i = lax.broadcasted_iota(jnp.int32, (L, L), 0)
    qry_i = lax.broadcasted_iota(jnp.int32, (L, L), 1)
    causal_t = key_i <= qry_i

    tril = jnp.where(qry_i <= key_i, 1.0, 0.0).astype(BF16)
    lane = lax.broadcasted_iota(jnp.int32, (L, LANES), 1)
    for r0 in range(0, g_ref.shape[0], L):
        g = g_ref[r0:r0 + L, :]
        logf = jnp.where(lane >= A_HEADS, _log_sigmoid(g), 0.0)
        z = jnp.where(lane < A_HEADS, g, _dot_exact_lhs01(tril, logf))
        z_ref[r0:r0 + L, :] = z
        zt_ref[:, r0:r0 + L] = z.T

    def body(c, carry):
        start = pl.multiple_of(c * L, L)
        rows = pl.ds(start, L)
        z = z_ref[rows, :]
        early = []
        for h in range(A_HEADS):
            i_col = z[:, h:h + 1]
            b_col = z[:, A_HEADS + h:A_HEADS + h + 1]
            m_prev = m_ref[h, 0:1, 0:1]
            qh = q_ref[rows, h * dk:(h + 1) * dk]
            kh = k_ref[rows, h * dk:(h + 1) * dk]
            vt = vt_ref[h * dv:(h + 1) * dv, pl.ds(start, L)]
            s_t = _dot_nt(kh, qh)
            ct = ct_ref[h]
            inter = _dot_nt(ct.astype(BF16), qh)
            b_last = b_col[L - 1:L, :]
            g_col = b_last - b_col + i_col
            m_new = jnp.maximum(b_last + m_prev, jnp.max(g_col, axis=0, keepdims=True))
            decay = jnp.exp(b_last + m_prev - m_new)
            kws = kh.astype(F32) * jnp.exp(g_col - m_new)
            ct_ref[h, :dv, :] = decay * ct[:dv] + _dot(vt, kws.astype(BF16))
            ct_ref[h, dv:dv + 1, :] = decay * ct[dv:dv + 1] + jnp.sum(kws, axis=0, keepdims=True)
            m_ref[h] = jnp.broadcast_to(m_new, m_ref.shape[1:])
            early.append((i_col - b_col, m_prev, vt, s_t, inter))
        for h in range(A_HEADS):
            ib_col, m_prev, vt, s_t, inter = early[h]
            b_row = zt_ref[A_HEADS + h:A_HEADS + h + 1, pl.ds(start, L)]
            d_t = jnp.where(causal_t, b_row + ib_col, -jnp.inf)
            m_inter = b_row + m_prev
            m_t = jnp.maximum(m_inter, jnp.max(d_t, axis=0, keepdims=True))
            a_t = s_t * jnp.exp(d_t - m_t)
            w_inter = jnp.exp(m_inter - m_t)
            num_t = _dot(vt, a_t.astype(BF16)) + w_inter * inter[:dv]
            den = jnp.sum(a_t, axis=0, keepdims=True) + w_inter * inter[dv:dv + 1]
            hout_t = num_t * (1.0 / jnp.maximum(jnp.abs(den), jnp.exp(-m_t)))
            ms = jnp.mean(hout_t * hout_t, axis=0, keepdims=True)
            y_t = (hout_t * lax.rsqrt(ms + RMS_EPS) * nw_ref[h * dv:(h + 1) * dv, :]
                   * ot_ref[h * dv:(h + 1) * dv, pl.ds(start, L)])
            y_ref[rows, h * dv:(h + 1) * dv] = y_t.T.astype(y_ref.dtype)
        return carry

    lax.fori_loop(0, nchunks, body, 0, unroll=True)


def _mlstm(q, k, vt, ot, g, norm_w, bsz, seq):
    t, qk = q.shape
    vw = vt.shape[0]
    dk, dv = qk // A_HEADS, vw // A_HEADS
    ts = min(MLSTM_STEP, seq)
    spb = seq // ts
    chunk = min(MLSTM_CHUNK, ts)
    row = lambda width: pl.BlockSpec((ts, width), lambda b, s: (b * spb + s, 0))
    col = pl.BlockSpec((vw, ts), lambda b, s: (0, b * spb + s))
    nw_cols = jnp.broadcast_to(norm_w.astype(F32).reshape(vw, 1), (vw, chunk))
    return pl.pallas_call(
        functools.partial(_mlstm_kernel, chunk=chunk, dk=dk, dv=dv),
        grid=(bsz, spb),
        in_specs=[row(qk), row(qk), col, col, row(LANES), _const_spec((vw, chunk))],
        out_specs=row(vw),
        out_shape=jax.ShapeDtypeStruct((t, vw), BF16),
        scratch_shapes=[pltpu.VMEM((A_HEADS, dv + 8, dk), F32), pltpu.VMEM((A_HEADS, 8, LANES), F32),
                        pltpu.VMEM((ts, LANES), F32), pltpu.VMEM((LANES, ts), F32)],
        compiler_params=_params("arbitrary", "arbitrary"),
        name="mlstm_mixer",
    )(q, k, vt, ot, g, nw_cols)


def _swa_kernel(sink_ref, q_ref, kc_ref, kp_ref, vtc_ref, vtp_ref, o_ref, st_ref, *, heads, hd, group, blk):
    n = pl.program_id(1)
    nsub = q_ref.shape[0] // blk
    hpd = SWA_HEADS_PER_DOT
    key_i = lax.broadcasted_iota(jnp.int32, (2 * blk, blk), 0)
    qry_i = lax.broadcasted_iota(jnp.int32, (2 * blk, blk), 1)
    band = (key_i > qry_i) & (key_i <= qry_i + blk)
    bias_any = jnp.concatenate([jnp.where(band, 0.0, -jnp.inf)] * hpd, axis=1)
    bias_first = jnp.concatenate([jnp.where(band & ((key_i >= blk) | (n > 0)), 0.0, -jnp.inf)] * hpd, axis=1)
    upper = lax.broadcasted_iota(jnp.int32, (blk, LANES), 1) >= hd
    ones = jnp.ones((ONES_ROWS, 2 * blk), BF16)

    def band_keys(ref_prev, ref_cur, j, axis):
        if axis == 0:
            prev = ref_prev[...] if j == 0 else ref_cur[(j - 1) * blk:j * blk, :]
            return prev, ref_cur[j * blk:(j + 1) * blk, :]
        prev = ref_prev[...] if j == 0 else ref_cur[:, (j - 1) * blk:j * blk]
        return prev, ref_cur[:, j * blk:(j + 1) * blk]

    def scores(j, j0):
        gi = j0 // group
        ks = slice(gi * LANES, (gi + 1) * LANES)
        kp, kc = band_keys(kp_ref, kc_ref, j, 0)
        k2 = jnp.concatenate([kp[:, ks], kc[:, ks]], axis=0)
        qz = []
        for hq in range(j0, j0 + hpd):
            slab = q_ref[j * blk:(j + 1) * blk, (hq // 2) * LANES:(hq // 2 + 1) * LANES]
            own = upper if hq % 2 else jnp.logical_not(upper)
            qz.append(jnp.where(own, slab, jnp.zeros_like(slab)))
        return _dot_nt(k2, jnp.concatenate(qz, axis=0))

    units = [(j, j0) for j in range(nsub) for j0 in range(0, heads, hpd)]
    for idx, (j, j0) in enumerate(units):
        st_ref[idx] = scores(j, j0)
    for idx, (j, j0) in enumerate(units):
        gi = j0 // group
        st = st_ref[idx] + (bias_first if j == 0 else bias_any)
        off = (j0 % group) * blk
        sink = sink_ref[gi:gi + 1, off:off + hpd * blk] * LOG2_E
        m = jnp.maximum(jnp.max(st, axis=0, keepdims=True), sink)
        p = jnp.exp2(st - m).astype(BF16)
        vs = slice(gi * hd, (gi + 1) * hd)
        vp, vc = band_keys(vtp_ref, vtc_ref, j, 1)
        vte = jnp.concatenate([jnp.concatenate([vp[vs, :], vc[vs, :]], axis=1), ones], axis=0)
        acc = _dot(vte, p)
        ot = acc[:hd, :] * (1.0 / (acc[hd:hd + 1, :] + jnp.exp2(sink - m)))
        for pr in range(hpd // 2):
            pair = jnp.concatenate([ot[:, (2 * pr) * blk:(2 * pr + 1) * blk],
                                    ot[:, (2 * pr + 1) * blk:(2 * pr + 2) * blk]], axis=0)
            c0 = (j0 // 2 + pr) * LANES
            o_ref[j * blk:(j + 1) * blk, c0:c0 + LANES] = pair.T.astype(o_ref.dtype)


def _swa(q, k2, vt, sinks, bsz, seq):
    t, qw = q.shape
    kw = vt.shape[0]
    heads = qw // B_HEAD_DIM
    blk = min(B_BLOCK, seq)
    nb = seq // blk
    per = math.gcd(SWA_BLOCKS_PER_STEP, nb)
    ns = nb // per
    sink_rows = jnp.repeat(sinks.astype(F32).reshape(heads // B_GROUP, B_GROUP), blk, axis=1)
    prev_blk = lambda b, n: b * nb + jnp.maximum(n * per - 1, 0)
    cur = lambda width: pl.BlockSpec((per * blk, width), lambda b, n: (b * ns + n, 0))
    prev = lambda width: pl.BlockSpec((blk, width), lambda b, n: (prev_blk(b, n), 0))
    cur_t = pl.BlockSpec((kw, per * blk), lambda b, n: (0, b * ns + n))
    prev_t = pl.BlockSpec((kw, blk), lambda b, n: (0, prev_blk(b, n)))
    nunits = per * (heads // SWA_HEADS_PER_DOT)
    return pl.pallas_call(
        functools.partial(_swa_kernel, heads=heads, hd=B_HEAD_DIM, group=B_GROUP, blk=blk),
        grid=(bsz, ns),
        in_specs=[_const_spec(sink_rows.shape), cur(qw), cur(2 * kw), prev(2 * kw), cur_t, prev_t],
        out_specs=cur(qw),
        out_shape=jax.ShapeDtypeStruct((t, qw), BF16),
        scratch_shapes=[pltpu.VMEM((nunits, 2 * blk, SWA_HEADS_PER_DOT * blk), F32)],
        compiler_params=_params("arbitrary", "arbitrary"),
        name="swa_mixer",
    )(sink_rows, q, k2, k2, vt, vt)


def _hgrn_kernel(q_ref, f_ref, v_ref, g_ref, lbp_ref, nw_ref, y_ref, st_ref, cf_ref, ck_ref,
                 cfc_ref, ckc_ref, vsc_ref, *,
                 chunk, dk, dv, layer):
    L = chunk
    nchunks = q_ref.shape[0] // L
    heads = q_ref.shape[1] // dk

    @pl.when(pl.program_id(1) == 0)
    def _():
        st_ref[...] = jnp.zeros_like(st_ref)

    lbp = lbp_ref[...]
    e = jnp.exp(lbp - jnp.max(lbp, axis=0, keepdims=True))
    sm = e / jnp.sum(e, axis=0, keepdims=True)
    lb = jnp.zeros_like(sm[0:1])
    for r in range(1, layer + 1):
        lb = lb + sm[r:r + 1]
    log2_lb = jnp.log(lb) * LOG2_E
    log2_1m_lb = jnp.log1p(-lb) * LOG2_E

    ones = jnp.ones((dk, LANES), BF16)
    half = L // 2
    trow = lax.broadcasted_iota(jnp.int32, (half, dv), 0)

    rb = min(HGRN_CUMSUM_ROWS, q_ref.shape[0])
    row_i = lax.broadcasted_iota(jnp.int32, (rb, rb), 0)
    col_i = lax.broadcasted_iota(jnp.int32, (rb, rb), 1)
    tril = jnp.where((col_i <= row_i) & (col_i // L == row_i // L), 1.0, 0.0).astype(BF16)
    for r0 in range(0, q_ref.shape[0], rb):
        f2 = f_ref[r0:r0 + rb, :] * LOG2_E
        ls = jnp.minimum(f2, 0.0) - jnp.log2(1.0 + jnp.exp2(-jnp.abs(f2)))
        x2 = log2_1m_lb + ls
        logf = jnp.maximum(log2_lb, x2) + jnp.log2(1.0 + jnp.exp2(-jnp.abs(log2_lb - x2)))
        cf = _dot_exact_lhs01(tril, logf)
        cf_ref[r0:r0 + rb, :] = cf
        ck_ref[r0:r0 + rb, :] = cf - (x2 - f2)

    unroll = cfc_ref.shape[0]

    def body(c, carry):
        rows = [pl.ds(pl.multiple_of((c * unroll + u) * L, L), L) for u in range(unroll)]
        for u in range(unroll):
            cfc_ref[u] = cf_ref[rows[u], :]
            ckc_ref[u] = ck_ref[rows[u], :]
            vsc_ref[u] = v_ref[rows[u], :].astype(F32)
        def issue(h, u):
            cs = slice(h * dk, (h + 1) * dk)
            vs = slice(h * dv, (h + 1) * dv)
            cfh = cfc_ref[u, :, cs]
            qh = q_ref[rows[u], cs]
            st = st_ref[h]
            inter = _dot_nt((qh * jnp.exp2(cfh)).astype(BF16), st.astype(BF16))
            last = cfc_ref[u, L - 1:L, cs]
            kdec = jnp.exp2(last - ckc_ref[u, :, cs]).astype(BF16)
            st_ref[h] = jnp.exp2(last) * st + _dot(vsc_ref[u, :, vs].T.astype(BF16), kdec)
            ws = []
            for s in range(L):
                t0 = 0 if s < half else half
                ws.append(qh[t0:] * jnp.exp2(cfh[t0:] - ckc_ref[u, s:s + 1, cs]))
            w_all = jnp.concatenate(ws, axis=0).astype(BF16)
            r_all = _dot(w_all, ones)
            return r_all, inter

        def finish(h, u, r_all, o):
            vs = slice(h * dv, (h + 1) * dv)
            o_lo, o_hi = o[:half], o[half:]
            for s in range(half):
                r = r_all[s * L:(s + 1) * L, :]
                v_s = vsc_ref[u, s:s + 1, vs]
                o_lo = o_lo + jnp.where(trow >= s, r[:half], 0.0) * v_s
                o_hi = o_hi + r[half:] * v_s
            for s in range(half, L):
                r = r_all[half * L + (s - half) * half:half * L + (s - half + 1) * half, :]
                o_hi = o_hi + jnp.where(trow >= s - half, r, 0.0) * vsc_ref[u, s:s + 1, vs]
            o = jnp.concatenate([o_lo, o_hi], axis=0)
            y = _rms_norm(o, nw_ref[:, vs]) * g_ref[rows[u], vs]
            y_ref[rows[u], vs] = y.astype(y_ref.dtype)

        work = [(h, u) for u in range(unroll) for h in range(heads)]
        pending = []
        for hu in work:
            pending.append((hu, issue(*hu)))
            if len(pending) > HGRN_SKEW:
                (h0, u0), (r0, i0) = pending.pop(0)
                finish(h0, u0, r0, i0)
        for (h0, u0), (r0, i0) in pending:
            finish(h0, u0, r0, i0)
        return carry

    lax.fori_loop(0, nchunks // unroll, body, 0)


def _hgrn(q, f, v, g, lower_bounds, layer, norm_w, bsz, seq):
    t, kw = q.shape
    vw = v.shape[1]
    dk, dv = kw // C_HEADS, vw // C_HEADS
    ts = min(HGRN_STEP, seq)
    spb = seq // ts
    chunk = min(HGRN_CHUNK, ts)
    unroll = math.gcd(HGRN_UNROLL, ts // chunk)
    row = lambda width: pl.BlockSpec((ts, width), lambda b, s: (b * spb + s, 0))
    return pl.pallas_call(
        functools.partial(_hgrn_kernel, chunk=chunk, dk=dk, dv=dv, layer=layer),
        grid=(bsz, spb),
        in_specs=[row(kw), row(kw), row(vw), row(vw), _const_spec(lower_bounds.shape), _const_spec((1, vw))],
        out_specs=row(vw),
        out_shape=jax.ShapeDtypeStruct((t, vw), BF16),
        scratch_shapes=[pltpu.VMEM((C_HEADS, dv, dk), F32), pltpu.VMEM((ts, kw), F32), pltpu.VMEM((ts, kw), F32),
                        pltpu.VMEM((unroll, chunk, kw), F32), pltpu.VMEM((unroll, chunk, kw), F32),
                        pltpu.VMEM((unroll, chunk, vw), F32)],
        compiler_params=_params("arbitrary", "arbitrary"),
        name="hgrn2_mixer",
    )(q, f, v, g, lower_bounds, norm_w.reshape(1, vw))


def _diff_kernel(q_ref, k_ref, vt_ref, lam_ref, nw_ref, o_ref, acc_ref, st_ref, *, blk, hd, lam_init):
    qi = pl.program_id(2)
    lv = lam_ref[...]
    lam = (jnp.exp(jnp.sum(lv[0:1] * lv[1:2], axis=-1, keepdims=True))
           - jnp.exp(jnp.sum(lv[2:3] * lv[3:4], axis=-1, keepdims=True)) + lam_init)
    q = q_ref[...]
    lane = lax.broadcasted_iota(jnp.int32, q.shape, 1)
    zero = jnp.zeros_like(q)
    qz = (jnp.where(lane < hd, q, zero), jnp.where(lane >= hd, q, zero))
    acc_ref[...] = jnp.zeros_like(acc_ref)

    hw = 2 * hd
    ones = jnp.ones((ONES_ROWS, blk), BF16)

    def scores_into(slot, j):
        kb = k_ref[pl.ds(pl.multiple_of(j * blk, blk), blk), :]
        for c in range(2):
            st_ref[2 * slot + c] = _dot_nt(kb, qz[c])

    def step(j, j_next, m, masked, src):
        if j_next is not None:
            scores_into(1 - src, j_next)
        vte = jnp.concatenate([vt_ref[:, pl.ds(pl.multiple_of(j * blk, blk), blk)], ones], axis=0)
        if masked:
            key_i = lax.broadcasted_iota(jnp.int32, (blk, blk), 0)
            qry_i = lax.broadcasted_iota(jnp.int32, (blk, blk), 1)
            keep = key_i <= qry_i
        m_out = []
        for c in range(2):
            st = st_ref[2 * src + c]
            if masked:
                st = jnp.where(keep, st, -jnp.inf)
            m_new = jnp.maximum(m[c], jnp.max(st, axis=0, keepdims=True))
            alpha = jnp.exp2(m[c] - m_new)
            p = jnp.exp2(st - m_new).astype(BF16)
            acc_ref[c] = alpha * acc_ref[c] + _dot(vte, p)
            m_out.append(m_new)
        return tuple(m_out)

    scores_into(0, qi)
    neg = jnp.full((1, blk), -jnp.inf, F32)
    m = step(qi, 0, (neg, neg), True, 0)

    def pair(i, cr):
        cr = step(2 * i, 2 * i + 1, cr, False, 1)
        return step(2 * i + 1, 2 * i + 2, cr, False, 0)

    npairs = jnp.maximum(qi - 1, 0) // 2
    m = lax.fori_loop(0, npairs, pair, m)

    @pl.when(qi % 2 == 1)
    def _():
        step(qi - 1, None, m, False, 1)

    @pl.when((qi % 2 == 0) & (qi > 0))
    def _():
        m2 = step(qi - 2, qi - 1, m, False, 1)
        step(qi - 1, None, m2, False, 0)

    norm = [acc_ref[c, :hw, :] * (1.0 / acc_ref[c, hw:hw + 1, :]) for c in range(2)]
    out_t = norm[0] - lam * norm[1]
    o_ref[...] = (_rms_norm(out_t.T, nw_ref[...]) * (1.0 - lam_init)).astype(o_ref.dtype)


def _diff_attn(q, k, vt, lam_vec, norm_w, lam_init, bsz, seq):
    t, w = q.shape
    hw = 2 * D_HEAD_DIM
    heads = w // hw
    blk = min(DIFF_BLOCK, seq)
    nq = seq // blk
    return pl.pallas_call(
        functools.partial(_diff_kernel, blk=blk, hd=D_HEAD_DIM, lam_init=lam_init),
        grid=(bsz, heads, nq),
        in_specs=[pl.BlockSpec((blk, hw), lambda b, h, i: (b * nq + i, h)),
                  pl.BlockSpec((seq, hw), lambda b, h, i: (b, h)),
                  pl.BlockSpec((hw, seq), lambda b, h, i: (h, b)),
                  _const_spec(lam_vec.shape), _const_spec((1, hw))],
        out_specs=pl.BlockSpec((blk, hw), lambda b, h, i: (b * nq + i, h)),
        out_shape=jax.ShapeDtypeStruct((t, w), BF16),
        scratch_shapes=[pltpu.VMEM((2, hw + ONES_ROWS, blk), F32), pltpu.VMEM((4, blk, blk), F32)],
        compiler_params=_params("arbitrary", "arbitrary", "arbitrary"),
        name="diff_attn_mixer",
    )(q, k, vt, lam_vec, norm_w.reshape(1, hw))


def _pad_cols(w, width):
    return jnp.pad(w, ((0, 0), (0, width - w.shape[1])))


def kernel(x, c, positions, ada_w, ada_b, ln_g, ln_b, mlstm_w_in, mlstm_b_gate, mlstm_norm, mlstm_w_out, swa_w_in, swa_sinks, swa_w_out, hgrn_w_in, hgrn_lower_bounds, hgrn_norm, hgrn_w_out, diff_w_in, diff_lambda, diff_norm, diff_w_out, ffn_w_in, ffn_w_out):
    bsz, seq, d = x.shape
    depth = ada_w.shape[0]
    alpha = (2 * depth) ** 0.25
    t = bsz * seq
    xf = x.reshape(t, d)
    mod = _modulation(c, ada_w, ada_b)
    cos_t, sin_t = _rope_tables(positions)
    for i in range(depth):
        mixer, j = i % 4, i // 4
        sub = 2 * i
        if mixer == 0:
            vw = mlstm_w_out.shape[1]
            qk = vw // 2
            n_gate = 2 * A_HEADS
            w_in = mlstm_w_in[j]
            w = _pad_cols(jnp.concatenate([w_in[:, :2 * qk], w_in[:, 2 * qk + 2 * vw:]], axis=1),
                          2 * qk + LANES).astype(BF16)
            w_vt = w_in[:, 2 * qk:2 * qk + vw].T.astype(BF16)
            w_ot = w_in[:, 2 * qk + vw:2 * qk + 2 * vw].T.astype(BF16)
            bg = _pad_cols(mlstm_b_gate[j].reshape(1, n_gate), LANES)
            q, k, vt, ot, g = _in_proj(
                functools.partial(_mlstm_in_kernel, qk=qk), xf, mod, sub, seq, w, [w_vt, w_ot, bg],
                [_const_spec(w_vt.shape), _const_spec(w_ot.shape), _const_spec((1, LANES))],
                [(qk, BF16, False), (qk, BF16, False), (vw, BF16, True), (vw, F32, True), (LANES, F32, False)])
            y = _mlstm(q, k, vt, ot, g, mlstm_norm[j], bsz, seq)
            w_out = mlstm_w_out[j]
        elif mixer == 1:
            qw = swa_w_out.shape[1]
            kw = qw // B_GROUP
            w_qk = swa_w_in[j][:, :qw + kw].astype(BF16)
            w_vt = swa_w_in[j][:, qw + kw:].T.astype(BF16)
            q, k2, vt = _in_proj(
                functools.partial(_attn_in_vt_kernel, qw=qw, kw=kw, scale=B_HEAD_DIM ** -0.5 * LOG2_E, kdup=True),
                xf, mod, sub, seq, w_qk, [w_vt, cos_t, sin_t],
                [_const_spec(w_vt.shape)] + [pl.BlockSpec((min(TOKEN_TILE, seq), LANES), lambda i: (i, 0))] * 2,
                [(qw, BF16, False), (2 * kw, BF16, False), (kw, BF16, True)])
            y = _swa(q, k2, vt, swa_sinks[j], bsz, seq)
            w_out = swa_w_out[j]
        elif mixer == 2:
            vw = hgrn_w_out.shape[1]
            q, f, v, g = _in_proj(
                functools.partial(_hgrn_in_kernel, kw=vw, vw=vw), xf, mod, sub, seq,
                hgrn_w_in[j].astype(BF16), [], [],
                [(vw, F32, False), (vw, F32, False), (vw, BF16, False), (vw, F32, False)])
            y = _hgrn(q, f, v, g, hgrn_lower_bounds.astype(F32), i, hgrn_norm[j], bsz, seq)
            w_out = hgrn_w_out[j]
        else:
            w_ = diff_w_out.shape[1]
            lam_init = 0.8 - 0.6 * math.exp(-0.3 * i)
            w_qk = diff_w_in[j][:, :2 * w_].astype(BF16)
            w_vt = diff_w_in[j][:, 2 * w_:].T.astype(BF16)
            q, k, vt = _in_proj(
                functools.partial(_attn_in_vt_kernel, qw=w_, kw=w_, scale=D_HEAD_DIM ** -0.5 * math.log2(math.e)),
                xf, mod, sub, seq,
                w_qk, [w_vt, cos_t, sin_t],
                [_const_spec(w_vt.shape)] + [pl.BlockSpec((min(TOKEN_TILE, seq), LANES), lambda i: (i, 0))] * 2,
                [(w_, BF16, False), (w_, BF16, False), (w_, BF16, True)])
            y = _diff_attn(q, k, vt, diff_lambda[j].astype(F32), diff_norm[j], lam_init, bsz, seq)
            w_out = diff_w_out[j]
        xf = _layer_tail(y, w_out.astype(BF16), xf, mod, sub, seq, ffn_w_in[i].astype(BF16),
                         ffn_w_out[i].astype(BF16), ln_g[i], ln_b[i], alpha)
    return xf.reshape(bsz, seq, d)
```
